```python
import math
import jax, jax.numpy as jnp
from jax import lax
import numpy as np

D_MODEL = 1024
BATCH = 4
SEQ = 4096
DEPTH = 1
DEC_BATCH = 128
DEC_SEQ = 4
PAST_LEN = 2048
PAGE_SIZE = 128

HEAD_DIM = 64
MIX_W = D_MODEL
RW_W = MIX_W // 2
ATT_W = MIX_W - RW_W
RW_HEADS = RW_W // HEAD_DIM
ATT_HEADS = ATT_W // HEAD_DIM
IDX_HEADS = 4
IDX_DIM = 64
MAX_TOPK = 256
Q_BLOCK = 128
W_LORA = 64
A_LORA = 64
G_LORA = 128
GN_EPS = 64e-5
LN_EPS = 1e-5
D_FF = 11 * D_MODEL // 4
CONV_W = 3
NUM_BUCKETS = 32
MAX_DISTANCE = 128
DEEPNORM_ALPHA = (2 * DEPTH) ** 0.25
DEEPNORM_BETA = (8 * DEPTH) ** -0.25
SHIFT_COLS = 3 * RW_W + W_LORA + A_LORA + G_LORA
ATT_COLS = 3 * ATT_W + IDX_HEADS * IDX_DIM + IDX_DIM + IDX_HEADS
IN_COLS = SHIFT_COLS + ATT_COLS

kernel_name = 'rwkv7_dsa_hybrid_step'


def layer_norm(x, g, b):
    xf = x.astype(jnp.float32)
    mu = jnp.mean(xf, axis=-1, keepdims=True)
    var = jnp.mean(jnp.square(xf - mu), axis=-1, keepdims=True)
    return ((xf - mu) * lax.rsqrt(var + LN_EPS) * g + b).astype(x.dtype)


def t5_bucket(dist):
    n = jnp.maximum(dist, 0)
    max_exact = NUM_BUCKETS // 2
    nf = jnp.maximum(n, 1).astype(jnp.float32)
    large = max_exact + (jnp.log(nf / max_exact) / math.log(MAX_DISTANCE / max_exact)
                         * (NUM_BUCKETS - max_exact)).astype(jnp.int32)
    large = jnp.minimum(large, NUM_BUCKETS - 1)
    return jnp.where(n < max_exact, n, large)


def rwkv7_group(p, w0, w_up, a0, a_up, g_up, k_k, k_a, r_k, gn_g, gn_b, wkv0):
    B, T, _ = p.shape
    r = p[..., :RW_W]
    k = p[..., RW_W:2 * RW_W]
    v = p[..., 2 * RW_W:3 * RW_W]
    o = 3 * RW_W
    wd = p[..., o:o + W_LORA]
    o += W_LORA
    ad = p[..., o:o + A_LORA]
    o += A_LORA
    gd = p[..., o:o + G_LORA]
    w_log = -jax.nn.softplus(-(w0 + jnp.tanh(wd) @ w_up)) - 0.5
    decay = jnp.exp(-jnp.exp(w_log.astype(jnp.float32)))
    a = jax.nn.sigmoid(a0 + ad @ a_up)
    g = jax.nn.sigmoid(gd) @ g_up
    heads = lambda t: t.reshape(B, T, RW_HEADS, HEAD_DIM).astype(jnp.float32)
    kk = heads(k * k_k)
    kk = kk * lax.rsqrt(jnp.maximum(jnp.sum(kk * kk, axis=-1, keepdims=True), 1e-24))
    k = heads(k * (1 + (a - 1) * k_a))
    r, v, a, decay = heads(r), heads(v), heads(a), heads(decay)

    def step(S, inp):
        r_t, w_t, k_t, v_t, kk_t, a_t = inp
        sa = jnp.einsum('bhij,bhj->bhi', S, -kk_t)
        S = (S * w_t[:, :, None, :] + sa[..., None] * (kk_t * a_t)[:, :, None, :]
             + v_t[..., None] * k_t[:, :, None, :])
        return S, jnp.einsum('bhij,bhj->bhi', S, r_t)

    xs = tuple(jnp.moveaxis(t, 1, 0) for t in (r, decay, k, v, kk, a))
    S_fin, y = lax.scan(step, wkv0.astype(jnp.float32), xs)
    y = jnp.moveaxis(y, 0, 1)
    mu = jnp.mean(y, axis=-1, keepdims=True)
    var = jnp.mean(jnp.square(y - mu), axis=-1, keepdims=True)
    yn = ((y - mu) * lax.rsqrt(var + GN_EPS)).reshape(B, T, RW_W) * gn_g + gn_b
    bonus = (jnp.sum(r * k * r_k, axis=-1, keepdims=True) * v).reshape(B, T, RW_W)
    out = (yn + bonus) * g
    return out.astype(p.dtype), S_fin.astype(p.dtype)


def dsa_attend(q, qi, wi, q_pos, k_all, v_all, ki_all, rel_bias, topk):
    L = k_all.shape[1]
    scores = jnp.einsum('bthd,bsd->bths', qi, ki_all) * IDX_DIM ** -0.5
    index = jnp.einsum('bth,bths->bts', wi, jax.nn.relu(scores)).astype(jnp.float32)
    visible = jnp.arange(L, dtype=jnp.int32)[None, :] <= q_pos[:, None]
    index = jnp.where(visible[None], index, -jnp.inf)
    _, sel = lax.top_k(index, topk)
    gather = jax.vmap(lambda rows, ids: rows[ids])
    k_sel = gather(k_all, sel)
    v_sel = gather(v_all, sel)
    dist = q_pos[None, :, None] - sel
    bias = jnp.moveaxis(rel_bias[t5_bucket(dist)], -1, 2).astype(jnp.float32)
    logits = jnp.einsum('bthd,btkhd->bthk', q, k_sel).astype(jnp.float32) * HEAD_DIM ** -0.5 + bias
    logits = jnp.where((dist >= 0)[:, :, None, :], logits, -jnp.inf)
    p = jax.nn.softmax(logits, axis=-1).astype(v_all.dtype)
    return jnp.einsum('bthk,btkhd->bthd', p, v_sel)


def attention_group(q, k, v, qi, ki, wi, past_k, past_v, past_ki, rel_bias):
    B, T = q.shape[0], q.shape[1]
    if past_k is None:
        k_all, v_all, ki_all = k, v, ki
        past_len = 0
    else:
        k_all = jnp.concatenate([past_k.astype(k.dtype), k], axis=1)
        v_all = jnp.concatenate([past_v.astype(v.dtype), v], axis=1)
        ki_all = jnp.concatenate([past_ki.astype(ki.dtype), ki], axis=1)
        past_len = past_k.shape[1]
    topk = min(MAX_TOPK, (past_len + T) // 4)
    q_pos = past_len + jnp.arange(T, dtype=jnp.int32)

    def attend(args):
        qb, qib, wib, pb = args
        return dsa_attend(qb, qib, wib, pb, k_all, v_all, ki_all, rel_bias, topk)

    if T > Q_BLOCK:
        nb = T // Q_BLOCK
        blocks = lambda t: jnp.moveaxis(t.reshape((B, nb, Q_BLOCK) + t.shape[2:]), 1, 0)
        out = lax.map(attend, (blocks(q), blocks(qi), blocks(wi), q_pos.reshape(nb, Q_BLOCK)))
        out = jnp.moveaxis(out, 0, 1)
    else:
        out = attend((q, qi, wi, q_pos))
    return out.reshape(B, T, ATT_W)


def decoder_layer(x, shift_prev, wkv_prev, conv_prev, past_k, past_v, past_ki, rel_bias,
                  w_in, shift_mu, rw_w0, rw_w_up, rw_a0, rw_a_up, rw_g_up, rw_k_k, rw_k_a, rw_r_k,
                  rw_gn_g, rw_gn_b, w_out, ln1_g, ln1_b, ffn_w_up, ffn_conv_w, ffn_conv_b,
                  ffn_w_down, ln2_g, ln2_b):
    B, T, _ = x.shape
    proj = x @ w_in
    p_rw = proj[..., :SHIFT_COLS]
    padded = jnp.concatenate([shift_prev.astype(proj.dtype), p_rw], axis=1)
    p_shift = p_rw + (padded[:, :-1] - p_rw) * shift_mu
    new_shift = padded[:, -1:]
    rw_out, new_wkv = rwkv7_group(p_shift, rw_w0, rw_w_up, rw_a0, rw_a_up, rw_g_up, rw_k_k, rw_k_a,
                                  rw_r_k, rw_gn_g, rw_gn_b, wkv_prev)
    o = SHIFT_COLS
    q = proj[..., o:o + ATT_W].reshape(B, T, ATT_HEADS, HEAD_DIM)
    o += ATT_W
    k = proj[..., o:o + ATT_W].reshape(B, T, ATT_HEADS, HEAD_DIM)
    o += ATT_W
    v = proj[..., o:o + ATT_W].reshape(B, T, ATT_HEADS, HEAD_DIM)
    o += ATT_W
    qi = proj[..., o:o + IDX_HEADS * IDX_DIM].reshape(B, T, IDX_HEADS, IDX_DIM)
    o += IDX_HEADS * IDX_DIM
    ki = proj[..., o:o + IDX_DIM]
    o += IDX_DIM
    wi = proj[..., o:o + IDX_HEADS] * IDX_HEADS ** -0.5
    att_out = attention_group(q, k, v, qi, ki, wi, past_k, past_v, past_ki, rel_bias)
    mix = jnp.concatenate([rw_out, att_out], axis=-1) @ w_out
    x = layer_norm(DEEPNORM_ALPHA * x + mix, ln1_g, ln1_b)
    up = x @ ffn_w_up
    u_conv, u_lin = up[..., :D_FF], up[..., D_FF:]
    padded = jnp.concatenate([conv_prev.astype(up.dtype), u_conv], axis=1)
    conv = sum(padded[:, j:j + T] * ffn_conv_w[j] for j in range(CONV_W)) + ffn_conv_b
    h = jax.nn.silu(conv) * u_lin
    x = layer_norm(DEEPNORM_ALPHA * x + h @ ffn_w_down, ln2_g, ln2_b)
    new_conv = padded[:, T:]
    return x, (k, v, ki, new_wkv, new_shift, new_conv)


def setup_inputs(seed: int = 0) -> dict:
    key = jax.random.key(seed)
    ks = jax.random.split(key, 40)
    f32 = jnp.float32
    nrm = lambda i, shape, s=1.0: jax.random.normal(ks[i], shape, f32) * s
    n_pages = PAST_LEN // PAGE_SIZE
    n_used = DEC_BATCH * n_pages
    n_pool = n_used + max(1, n_used // 4)
    page_table = jax.random.permutation(ks[0], n_pool)[:n_used].reshape(DEC_BATCH, n_pages).astype(jnp.int32)
    return {
        'x_prompt': nrm(1, (BATCH, SEQ, D_MODEL)),
        'x_sample': nrm(2, (DEC_BATCH, DEC_SEQ, D_MODEL)),
        'cache_k': nrm(3, (DEPTH, n_pool, PAGE_SIZE, ATT_HEADS, HEAD_DIM)),
        'cache_v': nrm(4, (DEPTH, n_pool, PAGE_SIZE, ATT_HEADS, HEAD_DIM)),
        'cache_kidx': nrm(5, (DEPTH, n_pool, PAGE_SIZE, IDX_DIM)),
        'page_table': page_table,
        'state_wkv': nrm(6, (DEPTH, DEC_BATCH, RW_HEADS, HEAD_DIM, HEAD_DIM), 0.5),
        'state_shift': nrm(7, (DEPTH, DEC_BATCH, 1, SHIFT_COLS)),
        'state_conv': nrm(8, (DEPTH, DEC_BATCH, CONV_W - 1, D_FF)),
        'rel_bias': nrm(9, (NUM_BUCKETS, ATT_HEADS), 0.5),
        'w_in': nrm(10, (DEPTH, D_MODEL, IN_COLS), D_MODEL ** -0.5),
        'shift_mu': jax.random.uniform(ks[11], (DEPTH, SHIFT_COLS), f32),
        'rw_w0': jax.random.uniform(ks[12], (DEPTH, RW_W), f32, -4.0, 2.0),
        'rw_w_up': nrm(13, (DEPTH, W_LORA, RW_W), 0.5 * W_LORA ** -0.5),
        'rw_a0': nrm(14, (DEPTH, RW_W), 0.5),
        'rw_a_up': nrm(15, (DEPTH, A_LORA, RW_W), A_LORA ** -0.5),
        'rw_g_up': nrm(16, (DEPTH, G_LORA, RW_W), G_LORA ** -0.5),
        'rw_k_k': 0.85 + nrm(17, (DEPTH, RW_W), 0.05),
        'rw_k_a': 1.0 + nrm(18, (DEPTH, RW_W), 0.05),
        'rw_r_k': nrm(19, (DEPTH, RW_HEADS, HEAD_DIM), 0.1),
        'rw_gn_g': 1.0 + nrm(20, (DEPTH, RW_W), 0.05),
        'rw_gn_b': nrm(21, (DEPTH, RW_W), 0.02),
        'w_out': nrm(22, (DEPTH, MIX_W, D_MODEL), DEEPNORM_BETA * MIX_W ** -0.5),
        'ln1_g': 1.0 + nrm(23, (DEPTH, D_MODEL), 0.05),
        'ln1_b': nrm(24, (DEPTH, D_MODEL), 0.02),
        'ffn_w_up': nrm(25, (DEPTH, D_MODEL, 2 * D_FF), D_MODEL ** -0.5),
        'ffn_conv_w': nrm(26, (DEPTH, CONV_W, D_FF), CONV_W ** -0.5),
        'ffn_conv_b': nrm(27, (DEPTH, D_FF), 0.02),
        'ffn_w_down': nrm(28, (DEPTH, D_FF, D_MODEL), DEEPNORM_BETA * D_FF ** -0.5),
        'ln2_g': 1.0 + nrm(29, (DEPTH, D_MODEL), 0.05),
        'ln2_b': nrm(30, (DEPTH, D_MODEL), 0.02),
    }


def reference(x_prompt, x_sample, cache_k, cache_v, cache_kidx, page_table, state_wkv, state_shift,
              state_conv, rel_bias, w_in, shift_mu, rw_w0, rw_w_up, rw_a0, rw_a_up, rw_g_up, rw_k_k,
              rw_k_a, rw_r_k, rw_gn_g, rw_gn_b, w_out, ln1_g, ln1_b, ffn_w_up, ffn_conv_w, ffn_conv_b,
              ffn_w_down, ln2_g, ln2_b):
    n_pages = page_table.shape[1]
    past_len = n_pages * PAGE_SIZE

    def from_pages(pool):
        g = pool[page_table]
        return g.reshape((g.shape[0], past_len) + g.shape[3:])

    yp, ys = x_prompt, x_sample
    bp = x_prompt.shape[0]
    outs_p, outs_s = [], []
    for l in range(DEPTH):
        lw = (w_in[l], shift_mu[l], rw_w0[l], rw_w_up[l], rw_a0[l], rw_a_up[l], rw_g_up[l], rw_k_k[l],
              rw_k_a[l], rw_r_k[l], rw_gn_g[l], rw_gn_b[l], w_out[l], ln1_g[l], ln1_b[l], ffn_w_up[l],
              ffn_conv_w[l], ffn_conv_b[l], ffn_w_down[l], ln2_g[l], ln2_b[l])
        yp, st_p = decoder_layer(
            yp, jnp.zeros((bp, 1, SHIFT_COLS), yp.dtype),
            jnp.zeros((bp, RW_HEADS, HEAD_DIM, HEAD_DIM), yp.dtype),
            jnp.zeros((bp, CONV_W - 1, D_FF), yp.dtype), None, None, None, rel_bias, *lw)
        ys, st_s = decoder_layer(
            ys, state_shift[l], state_wkv[l], state_conv[l], from_pages(cache_k[l]),
            from_pages(cache_v[l]), from_pages(cache_kidx[l]), rel_bias, *lw)
        outs_p.append(st_p)
        outs_s.append(st_s)
    st = lambda outs, i: jnp.stack([o[i] for o in outs])
    return (yp, ys, st(outs_p, 0), st(outs_p, 1), st(outs_p, 2), st(outs_p, 3), st(outs_p, 4), st(outs_p, 5),
            st(outs_s, 0), st(outs_s, 1), st(outs_s, 2), st(outs_s, 3), st(outs_s, 4), st(outs_s, 5))
```

```python
import functools
import math

import jax
import jax.numpy as jnp
from jax import lax
from jax.experimental import pallas as pl
from jax.experimental.pallas import tpu as pltpu

F32 = jnp.float32
BF16 = jnp.bfloat16
I32 = jnp.int32
HIGHEST = lax.Precision.HIGHEST

D_MODEL = 1024
HEAD_DIM = 64
RW_W = 512
ATT_W = 512
RW_HEADS = 8
ATT_HEADS = 8
IDX_HEADS = 4
IDX_DIM = 64
MAX_TOPK = 256
W_LORA = 64
A_LORA = 64
G_LORA = 128
GN_EPS = 64e-5
LN_EPS = 1e-5
D_FF = 2816
CONV_W = 3
NUM_BUCKETS = 32
MAX_DISTANCE = 128
PAGE_SIZE = 128
SHIFT_COLS = 3 * RW_W + W_LORA + A_LORA + G_LORA
LANES = 128
SUBLANES = 8
C_Q = SHIFT_COLS
C_K = C_Q + ATT_W
C_V = C_K + ATT_W
C_QI = C_V + ATT_W
C_KI = C_QI + IDX_HEADS * IDX_DIM
C_WI = C_KI + LANES
C_END = C_WI + LANES
VMEM_LIMIT = 56 * 1024 * 1024
INT_MIN = -(2 ** 31)
NEG_BIG = -1e30


def _cparams(sem):
    return pltpu.CompilerParams(dimension_semantics=sem, vmem_limit_bytes=VMEM_LIMIT)


def _const_spec(shape):
    zeros = (0,) * len(shape)
    return pl.BlockSpec(shape, lambda *_: zeros, pipeline_mode=pl.Buffered(1))


def _proj_kernel(x_ref, w_ref, prw_ref, qb_ref, k_ref, v_ref, kb_ref, vb_ref, qi_ref, kd_ref, kdb_ref, wi_ref):
    x = x_ref[...].astype(BF16)

    def mm(lo, hi):
        return jnp.dot(x, w_ref[:, lo:hi], preferred_element_type=F32)

    prw_ref[...] = mm(0, C_Q)
    qb_ref[...] = mm(C_Q, C_K).astype(BF16)
    k = mm(C_K, C_V)
    k_ref[...] = k
    kb_ref[...] = k.astype(BF16)
    v = mm(C_V, C_QI)
    v_ref[...] = v
    vb_ref[...] = v.astype(BF16)
    qi_ref[...] = mm(C_QI, C_KI)
    kd = mm(C_KI, C_WI)
    kd_ref[...] = kd
    kdb_ref[...] = kd.astype(BF16)
    wi_ref[...] = mm(C_WI, C_END)


def _proj(x2d, wp):
    m = x2d.shape[0]
    tm = min(256, m)
    widths = [(C_Q, F32), (ATT_W, BF16), (ATT_W, F32), (ATT_W, F32), (ATT_W, BF16), (ATT_W, BF16),
              (IDX_HEADS * IDX_DIM, F32), (LANES, F32), (LANES, BF16), (LANES, F32)]
    return pl.pallas_call(
        _proj_kernel,
        grid=(m // tm,),
        in_specs=[pl.BlockSpec((tm, D_MODEL), lambda i: (i, 0)), _const_spec((D_MODEL, C_END))],
        out_specs=[pl.BlockSpec((tm, w), lambda i: (i, 0)) for w, _ in widths],
        out_shape=[jax.ShapeDtypeStruct((m, w), dt) for w, dt in widths],
        compiler_params=_cparams(("arbitrary",)),
        name="proj",
    )(x2d, wp)


def _softplus(z):
    return jnp.maximum(z, 0.0) + jnp.log1p(jnp.exp(-jnp.abs(z)))


def _sigmoid(z):
    return 1.0 / (1.0 + jnp.exp(-z))


def _seg_sum(x, ones_bd):
    parts = [jnp.dot(x[:, g * LANES:(g + 1) * LANES], ones_bd, precision=HIGHEST, preferred_element_type=F32)
             for g in range(x.shape[1] // LANES)]
    return jnp.concatenate(parts, axis=1)


def _rwkv_kernel(p_ref, sp_ref, s0_ref, mu_ref, w0_ref, lw_ref, a0_ref, gup_ref, kk_ref, ka_ref, rk_ref,
                 gng_ref, gnb_ref, ones2_ref, diag_ref,
                 out_ref, sfin_ref, shift_ref,
                 pad_ref, carry_ref, s_ref, kkv_ref, dec_ref, b_ref, k2_ref, r_ref, v_ref, g_ref, bon_ref, y_ref,
                 *, nb, tc_len):
    tc = pl.program_id(1)
    last = pl.num_programs(1) - 1
    npair = RW_HEADS // 2

    @pl.when(tc == 0)
    def _():
        carry_ref[...] = sp_ref[...]
        for b in range(nb):
            for hp in range(npair):
                s_ref[b, hp] = jnp.concatenate([s0_ref[b, 2 * hp], s0_ref[b, 2 * hp + 1]], axis=1)

    rows = lax.broadcasted_iota(I32, (LANES, LANES), 0)
    cols = lax.broadcasted_iota(I32, (LANES, LANES), 1)
    ones_bd = (rows // HEAD_DIM == cols // HEAD_DIM).astype(F32)
    lane = lax.broadcasted_iota(I32, (tc_len, LANES), 1)

    for b in range(nb):
        x = p_ref[b]
        pad_ref[b, 7:8, :] = carry_ref[b]
        pad_ref[b, 8:8 + tc_len, :] = x
        xm1 = pad_ref[b, 7:7 + tc_len, :]
        ps = x + (xm1 - x) * mu_ref[...]
        carry_ref[b] = x[tc_len - 1:tc_len, :]
        r = ps[:, 0:RW_W]
        k = ps[:, RW_W:2 * RW_W]
        v = ps[:, 2 * RW_W:3 * RW_W]
        o = 3 * RW_W
        z = ps[:, o:o + LANES]
        z = jnp.where(lane < W_LORA, jnp.tanh(z), z)
        lo = jnp.dot(z, lw_ref[...], precision=HIGHEST, preferred_element_type=F32)
        w_log = -_softplus(-(w0_ref[...] + lo[:, 0:RW_W])) - 0.5
        dec = jnp.exp(-jnp.exp(w_log))
        a = _sigmoid(a0_ref[...] + lo[:, RW_W:2 * RW_W])
        gd = ps[:, o + LANES:o + 2 * LANES]
        g = jnp.dot(_sigmoid(gd), gup_ref[...], precision=HIGHEST, preferred_element_type=F32)
        kkx = k * kk_ref[...]
        ss = _seg_sum(kkx * kkx, ones_bd)
        kkn = kkx * lax.rsqrt(jnp.maximum(ss, 1e-24))
        k2 = k * (1.0 + (a - 1.0) * ka_ref[...])
        bonus = _seg_sum(r * k2 * rk_ref[...], ones_bd) * v
        kkv_ref[b, 0:tc_len] = kkn
        dec_ref[b, 0:tc_len] = dec
        b_ref[b, 0:tc_len] = kkn * a
        k2_ref[b, 0:tc_len] = k2
        r_ref[b, 0:tc_len] = r
        v_ref[b, 0:tc_len] = v
        g_ref[b, 0:tc_len] = g
        bon_ref[b, 0:tc_len] = bonus

    diag = diag_ref[...]

    def bsum(xv):
        hi = xv.astype(BF16)
        mid = (xv - hi.astype(F32)).astype(BF16)
        return jnp.dot(jnp.concatenate([hi, mid], axis=1), ones2_ref[...], preferred_element_type=F32)

    sub = min(SUBLANES, tc_len)
    riota = lax.broadcasted_iota(I32, (SUBLANES, LANES), 0)

    def group(gi, carry):
        t0 = pl.multiple_of(gi * SUBLANES, SUBLANES)
        for b in range(nb):
            for hp in range(npair):
                sl = slice(hp * LANES, (hp + 1) * LANES)
                rows8 = [ref[b, pl.ds(t0, SUBLANES), sl] for ref in (kkv_ref, dec_ref, b_ref, k2_ref, r_ref, v_ref)]
                s = s_ref[b, hp]
                y8 = jnp.zeros((SUBLANES, LANES), F32)
                for i in range(sub):
                    kkr, wr, br, kr, rr, vr = [a[i:i + 1, :] for a in rows8]
                    sa = -bsum(s * kkr)
                    vb = bsum(diag * vr)
                    s = s * wr + sa * br + vb * kr
                    yrow = jnp.sum(bsum(s * rr) * diag, axis=0, keepdims=True)
                    y8 = jnp.where(riota == i, yrow, y8)
                s_ref[b, hp] = s
                y_ref[b, pl.ds(t0, SUBLANES), sl] = y8
        return carry

    lax.fori_loop(0, max(1, tc_len // SUBLANES), group, 0)

    for b in range(nb):
        y = y_ref[b, 0:tc_len]
        mu = _seg_sum(y, ones_bd) * (1.0 / HEAD_DIM)
        yc = y - mu
        var = _seg_sum(yc * yc, ones_bd) * (1.0 / HEAD_DIM)
        yn = yc * lax.rsqrt(var + GN_EPS) * gng_ref[...] + gnb_ref[...]
        out_ref[b] = (yn + bon_ref[b, 0:tc_len]) * g_ref[b, 0:tc_len]

    @pl.when(tc == last)
    def _():
        shift_ref[...] = carry_ref[...]
        for b in range(nb):
            for hp in range(npair):
                s = s_ref[b, hp]
                sfin_ref[b, 2 * hp] = s[:, 0:HEAD_DIM]
                sfin_ref[b, 2 * hp + 1] = s[:, HEAD_DIM:2 * HEAD_DIM]


def _rwkv(p, shift_prev, wkv_prev, lw):
    bsz, t, _ = p.shape
    nb = 4
    tc_len = min(128, t)
    row = lambda a: a.reshape(1, -1)
    w_up, a_up = lw["rw_w_up"], lw["rw_a_up"]
    lora = jnp.zeros((LANES, 2 * RW_W), F32)
    lora = lora.at[0:W_LORA, 0:RW_W].set(w_up).at[W_LORA:LANES, RW_W:2 * RW_W].set(a_up)
    rr = jnp.arange(2 * LANES)[:, None] % LANES
    cc = jnp.arange(LANES)[None, :]
    ones2 = (rr // HEAD_DIM == cc // HEAD_DIM).astype(BF16)
    diag = (jnp.arange(HEAD_DIM)[:, None] == cc % HEAD_DIM).astype(F32)
    blk3 = lambda w: pl.BlockSpec((nb, tc_len, w), lambda i, j: (i, j, 0))
    vec = lambda w: _const_spec((1, w))
    assert tc_len < SUBLANES or tc_len % SUBLANES == 0
    scr = lambda w: pltpu.VMEM((nb, max(SUBLANES, tc_len), w), F32)
    out, sfin, shift = pl.pallas_call(
        functools.partial(_rwkv_kernel, nb=nb, tc_len=tc_len),
        grid=(bsz // nb, t // tc_len),
        in_specs=[blk3(SHIFT_COLS),
                  pl.BlockSpec((nb, 1, SHIFT_COLS), lambda i, j: (i, 0, 0)),
                  pl.BlockSpec((nb, RW_HEADS, HEAD_DIM, HEAD_DIM), lambda i, j: (i, 0, 0, 0)),
                  vec(SHIFT_COLS), vec(RW_W), _const_spec((LANES, 2 * RW_W)), vec(RW_W),
                  _const_spec((G_LORA, RW_W)), vec(RW_W), vec(RW_W), vec(RW_W), vec(RW_W), vec(RW_W),
                  _const_spec((2 * LANES, LANES)), _const_spec((HEAD_DIM, LANES))],
        out_specs=[blk3(RW_W),
                   pl.BlockSpec((nb, RW_HEADS, HEAD_DIM, HEAD_DIM), lambda i, j: (i, 0, 0, 0)),
                   pl.BlockSpec((nb, 1, SHIFT_COLS), lambda i, j: (i, 0, 0))],
        out_shape=[jax.ShapeDtypeStruct((bsz, t, RW_W), F32),
                   jax.ShapeDtypeStruct((bsz, RW_HEADS, HEAD_DIM, HEAD_DIM), F32),
                   jax.ShapeDtypeStruct((bsz, 1, SHIFT_COLS), F32)],
        scratch_shapes=[pltpu.VMEM((nb, tc_len + 8, SHIFT_COLS), F32),
                        pltpu.VMEM((nb, 1, SHIFT_COLS), F32),
                        pltpu.VMEM((nb, RW_HEADS // 2, HEAD_DIM, LANES), F32)] + [scr(RW_W)] * 9,
        compiler_params=_cparams(("arbitrary", "arbitrary")),
        name="rwkv",
    )(p, shift_prev, wkv_prev, row(lw["shift_mu"]), row(lw["rw_w0"]), lora, row(lw["rw_a0"]), lw["rw_g_up"],
      row(lw["rw_k_k"]), row(lw["rw_k_a"]), row(lw["rw_r_k"]), row(lw["rw_gn_g"]), row(lw["rw_gn_b"]), ones2, diag)
    return out, sfin, shift


def _score_key(idx):
    idx = jnp.where(idx == 0.0, 0.0, idx)
    bits = lax.bitcast_convert_type(idx, I32)
    return jnp.where(bits >= 0, bits, bits ^ jnp.int32(0x7FFFFFFF))


def _fold_lanes(m):
    acc = m[:, 0:LANES]
    for c in range(1, m.shape[1] // LANES):
        acc = acc + m[:, c * LANES:(c + 1) * LANES]
    return acc


def _select_params(key_ref, nblk, blk, topk, rows, pos_bits):
    kf = jnp.float32(topk)

    def count(pred):
        def body(j, acc):
            off = pl.multiple_of(j * blk, blk)
            kb = key_ref[:, pl.ds(off, blk)]
            pos = off + lax.broadcasted_iota(I32, (rows, blk), 1)
            return acc + _fold_lanes(pred(kb, pos).astype(F32))
        acc = lax.fori_loop(0, nblk, body, jnp.zeros((rows, LANES), F32))
        return jnp.sum(acc, axis=1, keepdims=True)

    zero = jnp.zeros((rows, 1), I32)
    c0 = count(lambda kb, pos: kb >= zero)
    theta = jnp.where(c0 >= kf, zero, jnp.full((rows, 1), INT_MIN, I32))

    def bit_body(i, theta):
        cand = theta | jnp.left_shift(jnp.int32(1), jnp.int32(30) - i)
        c = count(lambda kb, pos: kb >= cand)
        return jnp.where(c >= kf, cand, theta)

    theta = lax.fori_loop(0, 31, bit_body, theta)
    n_gt = count(lambda kb, pos: kb > theta)
    n_eq = count(lambda kb, pos: kb == theta)
    need = kf - n_gt
    any_excess = jnp.max(n_eq - need) > 0.0

    def tie_cut():
        def pbody(i, cut):
            cand = cut | jnp.left_shift(jnp.int32(1), jnp.int32(pos_bits - 1) - i)
            c = count(lambda kb, pos: (kb == theta) & (pos < cand))
            return jnp.where(c < need, cand, cut)
        return lax.fori_loop(0, pos_bits, pbody, jnp.zeros((rows, 1), I32))

    cut = lax.cond(any_excess, tie_cut, lambda: jnp.full((rows, 1), 2 ** 30, I32))
    return theta, cut


def _t5_bucket(dist):
    n = jnp.maximum(dist, 0)
    max_exact = NUM_BUCKETS // 2
    nf = jnp.maximum(n, 1).astype(F32)
    large = max_exact + (jnp.log(nf / max_exact) / math.log(MAX_DISTANCE / max_exact)
                         * (NUM_BUCKETS - max_exact)).astype(I32)
    large = jnp.minimum(large, NUM_BUCKETS - 1)
    return jnp.where(n < max_exact, n, large)


def _near_bias_kernel(rb_ref, o_ref, *, tq):
    r = lax.broadcasted_iota(I32, (2 * tq, 2 * tq), 0)
    c = lax.broadcasted_iota(I32, (2 * tq, 2 * tq), 1)
    top = r < tq
    bucket = _t5_bucket(jnp.where(top, r, r - tq) - c + tq)
    for p in range(ATT_HEADS // 2):
        acc = jnp.zeros((2 * tq, 2 * tq), F32)
        for bk in range(NUM_BUCKETS):
            val = jnp.where(top, rb_ref[bk, 2 * p], rb_ref[bk, 2 * p + 1])
            acc = jnp.where(bucket == bk, val, acc)
        o_ref[p] = acc


def _near_bias(rel_bias, tq):
    return pl.pallas_call(
        functools.partial(_near_bias_kernel, tq=tq),
        in_specs=[pl.BlockSpec(memory_space=pltpu.SMEM)],
        out_specs=pl.BlockSpec(memory_space=pltpu.VMEM),
        out_shape=jax.ShapeDtypeStruct((ATT_HEADS // 2, 2 * tq, 2 * tq), F32),
        compiler_params=pltpu.CompilerParams(vmem_limit_bytes=VMEM_LIMIT),
        name="near_bias",
    )(rel_bias)


def _attn_prompt_kernel(rb_ref, qb_ref, qi_ref, wi_ref, kb_ref, vb_ref, kdb_ref, near_ref, o_ref,
                        key_ref, qbd_ref, qibd_ref, wsc_ref, m_ref, l_ref, acc_ref, *, tq, topk, pos_bits):
    i = pl.program_id(1)
    npair = ATT_HEADS // 2
    lane = lax.broadcasted_iota(I32, (tq, LANES), 1)
    left = lane < HEAD_DIM

    for p in range(npair):
        qp = qb_ref[0, :, p * LANES:(p + 1) * LANES]
        zero = jnp.zeros_like(qp)
        qbd_ref[p] = jnp.concatenate([jnp.where(left, qp, zero), jnp.where(left, zero, qp)], axis=0)
    for p in range(IDX_HEADS // 2):
        qp = qi_ref[0, :, p * LANES:(p + 1) * LANES].astype(BF16)
        zero = jnp.zeros_like(qp)
        qibd_ref[p] = jnp.concatenate([jnp.where(left, qp, zero), jnp.where(left, zero, qp)], axis=0)
    wi = wi_ref[0]
    scale = IDX_HEADS ** -0.5 * IDX_DIM ** -0.5
    for h in range(IDX_HEADS):
        wsc_ref[h] = jnp.broadcast_to(wi[:, h:h + 1] * scale, (tq, LANES))

    trow = i * tq + lax.broadcasted_iota(I32, (tq, tq), 0)
    ccol = lax.broadcasted_iota(I32, (tq, tq), 1)

    def idx_body(j, carry):
        off = pl.multiple_of(j * tq, tq)
        kd = kdb_ref[0, pl.ds(off, tq), :]
        idx = jnp.zeros((tq, tq), F32)
        for p in range(IDX_HEADS // 2):
            sc = lax.dot_general(qibd_ref[p], kd, (((1,), (1,)), ((), ())), preferred_element_type=F32)
            sc = jnp.maximum(sc, 0.0)
            w0 = wsc_ref[2 * p]
            w1 = wsc_ref[2 * p + 1]
            idx = idx + sc[0:tq] * jnp.concatenate([w0] * (tq // LANES), axis=1)
            idx = idx + sc[tq:2 * tq] * jnp.concatenate([w1] * (tq // LANES), axis=1)
        idx = jnp.where(off + ccol <= trow, idx, -jnp.inf)
        key_ref[:, pl.ds(off, tq)] = _score_key(idx)
        return carry

    lax.fori_loop(0, i + 1, idx_body, 0)

    theta, cut = _select_params(key_ref, i + 1, tq, topk, tq, pos_bits)

    m_ref[...] = jnp.full(m_ref.shape, NEG_BIG, F32)
    l_ref[...] = jnp.zeros(l_ref.shape, F32)
    acc_ref[...] = jnp.zeros(acc_ref.shape, F32)
    toprow = lax.broadcasted_iota(I32, (2 * tq, 1), 0) < tq

    def attend(j, bias_of_pair):
        off = pl.multiple_of(j * tq, tq)
        kb = key_ref[:, pl.ds(off, tq)]
        pos = off + ccol
        sel = ((kb > theta) | ((kb == theta) & (pos <= cut))) & (pos <= trow)
        mb = jnp.where(sel, 0.0, -jnp.inf)
        mb = jnp.concatenate([mb, mb], axis=0)
        for p in range(npair):
            kblk = kb_ref[0, pl.ds(off, tq), p * LANES:(p + 1) * LANES]
            vblk = vb_ref[0, pl.ds(off, tq), p * LANES:(p + 1) * LANES]
            s = lax.dot_general(qbd_ref[p], kblk, (((1,), (1,)), ((), ())), preferred_element_type=F32)
            logits = s * (HEAD_DIM ** -0.5) + bias_of_pair(p) + mb
            m_old = m_ref[p]
            m_new = jnp.maximum(m_old, jnp.max(logits, axis=1, keepdims=True))
            alpha = jnp.exp(m_old - m_new)
            pr = jnp.exp(logits - m_new)
            l_ref[p] = alpha * l_ref[p] + jnp.sum(pr, axis=1, keepdims=True)
            acc_ref[p] = alpha * acc_ref[p] + jnp.dot(pr.astype(BF16), vblk, preferred_element_type=F32)
            m_ref[p] = m_new

    def far_bias(p):
        return jnp.where(toprow, rb_ref[NUM_BUCKETS - 1, 2 * p], rb_ref[NUM_BUCKETS - 1, 2 * p + 1])

    def far_body(j, carry):
        attend(j, far_bias)
        return carry

    lax.fori_loop(0, jnp.maximum(i - 1, 0), far_body, 0)

    @pl.when(i >= 1)
    def _():
        attend(i - 1, lambda p: near_ref[p, :, 0:tq])

    attend(i, lambda p: near_ref[p, :, tq:2 * tq])

    for p in range(npair):
        o = acc_ref[p] / l_ref[p]
        o_ref[0, :, p * LANES:(p + 1) * LANES] = jnp.where(left, o[0:tq], o[tq:2 * tq])


def _attn_prompt(qb, qi, wi, kb, vb, kdb, rel_bias, bsz, t):
    tq = min(256, t)
    topk = min(MAX_TOPK, t // 4)
    pos_bits = max(1, (t - 1).bit_length())
    near = _near_bias(rel_bias, tq)
    r3 = lambda a: a.reshape(bsz, t, a.shape[-1])
    qtile = lambda w: pl.BlockSpec((1, tq, w), lambda b, i: (b, i, 0))
    full = lambda w: pl.BlockSpec((1, t, w), lambda b, i: (b, 0, 0))
    npair = ATT_HEADS // 2
    return pl.pallas_call(
        functools.partial(_attn_prompt_kernel, tq=tq, topk=topk, pos_bits=pos_bits),
        grid=(bsz, t // tq),
        in_specs=[pl.BlockSpec(memory_space=pltpu.SMEM),
                  qtile(ATT_W), qtile(IDX_HEADS * IDX_DIM), qtile(LANES),
                  full(ATT_W), full(ATT_W), full(LANES),
                  _const_spec((npair, 2 * tq, 2 * tq))],
        out_specs=qtile(ATT_W),
        out_shape=jax.ShapeDtypeStruct((bsz, t, ATT_W), F32),
        scratch_shapes=[pltpu.VMEM((tq, t), I32),
                        pltpu.VMEM((npair, 2 * tq, LANES), BF16),
                        pltpu.VMEM((IDX_HEADS // 2, 2 * tq, LANES), BF16),
                        pltpu.VMEM((IDX_HEADS, tq, LANES), F32),
                        pltpu.VMEM((npair, 2 * tq, 1), F32),
                        pltpu.VMEM((npair, 2 * tq, 1), F32),
                        pltpu.VMEM((npair, 2 * tq, LANES), F32)],
        compiler_params=_cparams(("arbitrary", "arbitrary")),
        name="attn_prompt",
    )(rel_bias, r3(qb), r3(qi), r3(wi), r3(kb), r3(vb), r3(kdb), near)


def _sample_index_kernel(pt_ref, qi_ref, wi_ref, *rest, n_pages, ts):
    page_refs = rest[:n_pages + 1]
    o_ref = rest[n_pages + 1]
    qi = qi_ref[0]
    q16 = jnp.concatenate([qi[:, h * IDX_DIM:(h + 1) * IDX_DIM] for h in range(IDX_HEADS)], axis=0).astype(BF16)
    wi = wi_ref[0]
    scale = IDX_HEADS ** -0.5 * IDX_DIM ** -0.5
    w16 = jnp.concatenate([wi[:, h:h + 1] for h in range(IDX_HEADS)], axis=0) * scale
    trow = lax.broadcasted_iota(I32, (ts, PAGE_SIZE), 0)
    ncol = lax.broadcasted_iota(I32, (ts, PAGE_SIZE), 1)
    for j in range(n_pages + 1):
        kp = page_refs[j][0].astype(BF16)
        sc = lax.dot_general(q16, kp, (((1,), (1,)), ((), ())), preferred_element_type=F32)
        rw = jnp.maximum(sc, 0.0) * w16
        idx = rw[0:ts]
        for h in range(1, IDX_HEADS):
            idx = idx + rw[h * ts:(h + 1) * ts]
        if j == n_pages:
            idx = jnp.where(ncol <= trow, idx, -jnp.inf)
        o_ref[0, :, j * PAGE_SIZE:(j + 1) * PAGE_SIZE] = idx


def _sample_select_kernel(idx_ref, sel_ref, key_ref, *, topk, pos_bits):
    rows, lp = idx_ref.shape
    key_ref[...] = _score_key(idx_ref[...])
    theta, cut = _select_params(key_ref, 1, lp, topk, rows, pos_bits)
    kb = key_ref[...]
    pos = lax.broadcasted_iota(I32, (rows, lp), 1)
    sel = ((kb > theta) | ((kb == theta) & (pos <= cut))) & (idx_ref[...] > -jnp.inf)
    sel_ref[...] = sel.astype(F32)


def _sample_bias_kernel(rb_ref, near_ref, new_ref, far_ref, *, ts):
    rows = ts * ATT_HEADS
    width = PAGE_SIZE * ATT_HEADS

    def gather(bucket, rhead, shape):
        acc = jnp.zeros(shape, F32)
        for h in range(ATT_HEADS):
            for bk in range(NUM_BUCKETS):
                acc = jnp.where((bucket == bk) & (rhead == h), rb_ref[bk, h], acc)
        return acc

    r = lax.broadcasted_iota(I32, (rows, width), 0)
    c = lax.broadcasted_iota(I32, (rows, width), 1)
    near_ref[...] = gather(_t5_bucket(PAGE_SIZE + r // ATT_HEADS - c // ATT_HEADS), r % ATT_HEADS, (rows, width))
    r = lax.broadcasted_iota(I32, (rows, LANES), 0)
    c = lax.broadcasted_iota(I32, (rows, LANES), 1)
    dist = r // ATT_HEADS - c // ATT_HEADS
    ok = (dist >= 0) & (c // ATT_HEADS < ts) & (r % ATT_HEADS == c % ATT_HEADS)
    new_ref[...] = jnp.where(ok, gather(_t5_bucket(dist), r % ATT_HEADS, (rows, LANES)), -jnp.inf)
    far_ref[...] = gather(jnp.full((rows, LANES), NUM_BUCKETS - 1, I32), r % ATT_HEADS, (rows, LANES))


def _sample_attn_kernel(pt_ref, q_ref, sel_ref, eexp_ref, near_ref, new_ref, far_ref, *rest, n_pages, ts):
    k_refs = rest[:n_pages + 1]
    v_refs = rest[n_pages + 1:2 * n_pages + 2]
    o_ref = rest[2 * n_pages + 2]
    l_ref = rest[2 * n_pages + 3]
    rows = ts * ATT_HEADS
    width = PAGE_SIZE * ATT_HEADS
    q = q_ref[0]
    r = lax.broadcasted_iota(I32, (rows, width), 0)
    c = lax.broadcasted_iota(I32, (rows, width), 1)
    match = (r % ATT_HEADS) == (c % ATT_HEADS)
    sel_all = sel_ref[0].astype(BF16)

    def expand_sel(j, ncols):
        s = jnp.dot(sel_all[:, j * PAGE_SIZE:(j + 1) * PAGE_SIZE], eexp_ref[:, 0:ncols], preferred_element_type=F32)
        return s > 0.5

    for j in range(n_pages):
        kp = k_refs[j][0].astype(BF16)
        s = lax.dot_general(q, kp, (((1,), (1,)), ((), ())), preferred_element_type=F32)
        bias = near_ref[...] if j == n_pages - 1 else far_ref[:, 0:1]
        ok = match & expand_sel(j, width)
        l_ref[:, j * width:(j + 1) * width] = jnp.where(ok, s * (HEAD_DIM ** -0.5) + bias, -jnp.inf)
    kn = k_refs[n_pages][0].astype(BF16)
    s = lax.dot_general(q, kn, (((1,), (1,)), ((), ())), preferred_element_type=F32)
    ok = expand_sel(n_pages, LANES)
    l_ref[:, n_pages * width:] = jnp.where(ok, s * (HEAD_DIM ** -0.5) + new_ref[...], -jnp.inf)

    logits = l_ref[...]
    m = jnp.max(logits, axis=1, keepdims=True)
    pr = jnp.exp(logits - m)
    den = jnp.sum(pr, axis=1, keepdims=True)
    prb = pr.astype(BF16)
    acc = jnp.dot(prb[:, n_pages * width:], v_refs[n_pages][0].astype(BF16), preferred_element_type=F32)
    for j in range(n_pages):
        acc = acc + jnp.dot(prb[:, j * width:(j + 1) * width], v_refs[j][0].astype(BF16), preferred_element_type=F32)
    o_ref[0] = acc / den


def _attn_sample(qb, qi, wi, k_new, v_new, kd_new, cache_k, cache_v, cache_kidx, page_table, rel_bias, bsz, ts):
    n_pages = page_table.shape[1]
    n_pool = cache_k.shape[0]
    past_len = n_pages * PAGE_SIZE
    topk = min(MAX_TOPK, (past_len + ts) // 4)
    lp = (n_pages + 1) * PAGE_SIZE
    rows = ts * ATT_HEADS
    width = PAGE_SIZE * ATT_HEADS
    pt = page_table.reshape(-1).astype(I32)

    ki_pool = cache_kidx.reshape(n_pool, PAGE_SIZE, IDX_DIM)
    ki_new = jnp.zeros((bsz, PAGE_SIZE, IDX_DIM), F32).at[:, 0:ts].set(kd_new.reshape(bsz, ts, LANES)[..., 0:IDX_DIM])
    page_spec = lambda j, shp: pl.BlockSpec((1,) + shp, lambda b, ptr: (ptr[b * n_pages + j],) + (0,) * len(shp))
    per_b = lambda shp: pl.BlockSpec((1,) + shp, lambda b, ptr: (b,) + (0,) * len(shp))
    idx = pl.pallas_call(
        functools.partial(_sample_index_kernel, n_pages=n_pages, ts=ts),
        grid_spec=pltpu.PrefetchScalarGridSpec(
            num_scalar_prefetch=1, grid=(bsz,),
            in_specs=[per_b((ts, IDX_HEADS * IDX_DIM)), per_b((ts, LANES))]
                     + [page_spec(j, (PAGE_SIZE, IDX_DIM)) for j in range(n_pages)]
                     + [per_b((PAGE_SIZE, IDX_DIM))],
            out_specs=per_b((ts, lp))),
        out_shape=jax.ShapeDtypeStruct((bsz, ts, lp), F32),
        compiler_params=_cparams(("arbitrary",)),
        name="sample_index",
    )(pt, qi.reshape(bsz, ts, -1), wi.reshape(bsz, ts, LANES), *([ki_pool] * n_pages), ki_new)

    sel = pl.pallas_call(
        functools.partial(_sample_select_kernel, topk=topk, pos_bits=max(1, (lp - 1).bit_length())),
        out_shape=jax.ShapeDtypeStruct((bsz * ts, lp), F32),
        scratch_shapes=[pltpu.VMEM((bsz * ts, lp), I32)],
        compiler_params=pltpu.CompilerParams(vmem_limit_bytes=VMEM_LIMIT),
        name="sample_select",
    )(idx.reshape(bsz * ts, lp))

    near, newb, far = pl.pallas_call(
        functools.partial(_sample_bias_kernel, ts=ts),
        in_specs=[pl.BlockSpec(memory_space=pltpu.SMEM)],
        out_shape=[jax.ShapeDtypeStruct((rows, width), F32), jax.ShapeDtypeStruct((rows, LANES), F32),
                   jax.ShapeDtypeStruct((rows, LANES), F32)],
        compiler_params=pltpu.CompilerParams(vmem_limit_bytes=VMEM_LIMIT),
        name="sample_bias",
    )(rel_bias)

    eexp = (jnp.arange(PAGE_SIZE)[:, None] == jnp.arange(width)[None, :] // ATT_HEADS).astype(BF16)
    k_pool = cache_k.reshape(n_pool, width, HEAD_DIM)
    v_pool = cache_v.reshape(n_pool, width, HEAD_DIM)
    pad_new = lambda a: jnp.zeros((bsz, LANES, HEAD_DIM), F32).at[:, 0:rows].set(a.reshape(bsz, rows, HEAD_DIM))
    const = lambda shp: pl.BlockSpec(shp, lambda b, ptr: (0,) * len(shp))
    out = pl.pallas_call(
        functools.partial(_sample_attn_kernel, n_pages=n_pages, ts=ts),
        grid_spec=pltpu.PrefetchScalarGridSpec(
            num_scalar_prefetch=1, grid=(bsz,),
            in_specs=[per_b((rows, HEAD_DIM)), per_b((rows, lp)),
                      const((PAGE_SIZE, width)), const((rows, width)), const((rows, LANES)), const((rows, LANES))]
                     + [page_spec(j, (width, HEAD_DIM)) for j in range(n_pages)] + [per_b((LANES, HEAD_DIM))]
                     + [page_spec(j, (width, HEAD_DIM)) for j in range(n_pages)] + [per_b((LANES, HEAD_DIM))],
            out_specs=per_b((rows, HEAD_DIM)),
            scratch_shapes=[pltpu.VMEM((rows, n_pages * width + LANES), F32)]),
        out_shape=jax.ShapeDtypeStruct((bsz, rows, HEAD_DIM), F32),
        compiler_params=_cparams(("arbitrary",)),
        name="sample_attn",
    )(pt, qb.reshape(bsz, rows, HEAD_DIM), jnp.repeat(sel.reshape(bsz, ts, lp), ATT_HEADS, axis=1), eexp, near, newb, far,
      *([k_pool] * n_pages), pad_new(k_new), *([v_pool] * n_pages), pad_new(v_new))
    return out.reshape(bsz * ts, ATT_W)


def _layer_norm(x, g, b):
    mu = jnp.mean(x, axis=-1, keepdims=True)
    xc = x - mu
    var = jnp.mean(xc * xc, axis=-1, keepdims=True)
    return xc * lax.rsqrt(var + LN_EPS) * g + b


def _ffn_kernel(x_ref, rw_ref, att_ref, prev_ref, wo_ref, g1_ref, b1_ref, wup_ref, cw_ref, cb_ref, wdn_ref,
                g2_ref, b2_ref, y_ref, nc_ref, pad_ref, *, tm, shift, off, alpha):
    ti = pl.program_id(1)
    data = off + 2 * shift

    @pl.when(ti == 0)
    def _():
        pad_ref[off:data, :] = prev_ref[0]

    x = x_ref[0]
    mix = (jnp.dot(rw_ref[0].astype(BF16), wo_ref[0:RW_W, :], preferred_element_type=F32)
           + jnp.dot(att_ref[0].astype(BF16), wo_ref[RW_W:RW_W + ATT_W, :], preferred_element_type=F32))
    x1 = _layer_norm(alpha * x + mix, g1_ref[...], b1_ref[...])
    x1b = x1.astype(BF16)
    u_conv = jnp.dot(x1b, wup_ref[:, 0:D_FF], preferred_element_type=F32)
    u_lin = jnp.dot(x1b, wup_ref[:, D_FF:2 * D_FF], preferred_element_type=F32)
    pad_ref[data:data + tm, :] = u_conv
    conv = (pad_ref[off:off + tm, :] * cw_ref[0:1, :] + pad_ref[off + shift:off + shift + tm, :] * cw_ref[1:2, :]
            + u_conv * cw_ref[2:3, :] + cb_ref[...])
    h = conv * _sigmoid(conv) * u_lin
    down = jnp.dot(h.astype(BF16), wdn_ref[...], preferred_element_type=F32)
    y_ref[0] = _layer_norm(alpha * x1 + down, g2_ref[...], b2_ref[...])
    tail = pad_ref[off + tm:data + tm, :]
    pad_ref[off:data, :] = tail
    nc_ref[0] = tail


def _ffn(x, rw, att, prev, lw, alpha, shift):
    nb, rows, _ = x.shape
    tm = min(256, rows) if shift == 1 else rows
    off = 6 if shift == 1 else 0
    row = lambda a: a.reshape(1, -1)
    tile = lambda w: pl.BlockSpec((1, tm, w), lambda b, i: (b, i, 0))
    y, nc = pl.pallas_call(
        functools.partial(_ffn_kernel, tm=tm, shift=shift, off=off, alpha=alpha),
        grid=(nb, rows // tm),
        in_specs=[tile(D_MODEL), tile(RW_W), tile(ATT_W),
                  pl.BlockSpec((1, 2 * shift, D_FF), lambda b, i: (b, 0, 0)),
                  _const_spec((D_MODEL, D_MODEL)), _const_spec((1, D_MODEL)), _const_spec((1, D_MODEL)),
                  _const_spec((D_MODEL, 2 * D_FF)), _const_spec((CONV_W, D_FF)), _const_spec((1, D_FF)),
                  _const_spec((D_FF, D_MODEL)), _const_spec((1, D_MODEL)), _const_spec((1, D_MODEL))],
        out_specs=[tile(D_MODEL), pl.BlockSpec((1, 2 * shift, D_FF), lambda b, i: (b, 0, 0))],
        out_shape=[jax.ShapeDtypeStruct((nb, rows, D_MODEL), F32), jax.ShapeDtypeStruct((nb, 2 * shift, D_FF), F32)],
        scratch_shapes=[pltpu.VMEM((off + 2 * shift + tm, D_FF), F32)],
        compiler_params=_cparams(("arbitrary", "arbitrary")),
        name="ffn",
    )(x, rw, att, prev, lw["w_out"].astype(BF16), row(lw["ln1_g"]), row(lw["ln1_b"]),
      lw["ffn_w_up"].astype(BF16), lw["ffn_conv_w"], row(lw["ffn_conv_b"]), lw["ffn_w_down"].astype(BF16),
      row(lw["ln2_g"]), row(lw["ln2_b"]))
    return y, nc


def _layer(x, shift_prev, wkv_prev, conv_prev, past, rel_bias, lw, alpha):
    bsz, t, _ = x.shape
    w_in = lw["w_in"]
    w_ki = w_in[:, C_KI:C_KI + IDX_DIM]
    w_wi = w_in[:, C_KI + IDX_DIM:]
    wp = jnp.concatenate([w_in[:, 0:C_KI], w_ki, w_ki, w_wi,
                          jnp.zeros((D_MODEL, LANES - IDX_HEADS), F32)], axis=1).astype(BF16)
    prw, qb, k, v, kb, vb, qi, kd, kdb, wi = _proj(x.reshape(bsz * t, D_MODEL), wp)
    rw_out, new_wkv, new_shift = _rwkv(prw.reshape(bsz, t, SHIFT_COLS), shift_prev, wkv_prev, lw)
    if past is None:
        att = _attn_prompt(qb, qi, wi, kb, vb, kdb, rel_bias, bsz, t)
        y, new_conv = _ffn(x, rw_out, att.reshape(bsz, t, ATT_W), conv_prev, lw, alpha, shift=1)
    else:
        att = _attn_sample(qb, qi, wi, k, v, kd, *past, rel_bias, bsz, t)
        tmaj = lambda a: jnp.swapaxes(a.reshape(bsz, t, -1), 0, 1).reshape(1, t * bsz, -1)
        prev = jnp.swapaxes(conv_prev, 0, 1).reshape(1, (CONV_W - 1) * bsz, D_FF)
        y, new_conv = _ffn(tmaj(x), tmaj(rw_out), tmaj(att), prev, lw, alpha, shift=bsz)
        y = jnp.swapaxes(y.reshape(t, bsz, D_MODEL), 0, 1)
        new_conv = jnp.swapaxes(new_conv.reshape(CONV_W - 1, bsz, D_FF), 0, 1)
    state = (k.reshape(bsz, t, ATT_HEADS, HEAD_DIM), v.reshape(bsz, t, ATT_HEADS, HEAD_DIM),
             kd.reshape(bsz, t, LANES)[..., 0:IDX_DIM], new_wkv, new_shift, new_conv)
    return y, state


_LAYER_WEIGHTS = ("w_in", "shift_mu", "rw_w0", "rw_w_up", "rw_a0", "rw_a_up", "rw_g_up", "rw_k_k", "rw_k_a",
                  "rw_r_k", "rw_gn_g", "rw_gn_b", "w_out", "ln1_g", "ln1_b", "ffn_w_up", "ffn_conv_w",
                  "ffn_conv_b", "ffn_w_down", "ln2_g", "ln2_b")


def kernel(x_prompt, x_sample, cache_k, cache_v, cache_kidx, page_table, state_wkv, state_shift, state_conv,
           rel_bias, w_in, shift_mu, rw_w0, rw_w_up, rw_a0, rw_a_up, rw_g_up, rw_k_k, rw_k_a, rw_r_k, rw_gn_g,
           rw_gn_b, w_out, ln1_g, ln1_b, ffn_w_up, ffn_conv_w, ffn_conv_b, ffn_w_down, ln2_g, ln2_b):
    stacked = dict(zip(_LAYER_WEIGHTS, (w_in, shift_mu, rw_w0, rw_w_up, rw_a0, rw_a_up, rw_g_up, rw_k_k, rw_k_a,
                                        rw_r_k, rw_gn_g, rw_gn_b, w_out, ln1_g, ln1_b, ffn_w_up, ffn_conv_w,
                                        ffn_conv_b, ffn_w_down, ln2_g, ln2_b)))
    depth = w_in.shape[0]
    alpha = (2 * depth) ** 0.25
    bp = x_prompt.shape[0]
    yp, ys = x_prompt, x_sample
    outs_p, outs_s = [], []
    for l in range(depth):
        lw = {name: w[l] for name, w in stacked.items()}
        yp, st_p = _layer(yp, jnp.zeros((bp, 1, SHIFT_COLS), F32),
                          jnp.zeros((bp, RW_HEADS, HEAD_DIM, HEAD_DIM), F32),
                          jnp.zeros((bp, CONV_W - 1, D_FF), F32), None, rel_bias, lw, alpha)
        ys, st_s = _layer(ys, state_shift[l], state_wkv[l], state_conv[l],
                          (cache_k[l], cache_v[l], cache_kidx[l], page_table), rel_bias, lw, alpha)
        outs_p.append(st_p)
        outs_s.append(st_s)
    st = lambda outs, i: jnp.stack([o[i] for o in outs])
    return (yp, ys) + tuple(st(outs_p, i) for i in range(6)) + tuple(st(outs_s, i) for i in range(6))
```

```python
import functools
import math

import jax
import jax.numpy as jnp
from jax import lax
from jax.experimental import pallas as pl
from jax.experimental.pallas import tpu as pltpu

F32 = jnp.float32
BF16 = jnp.bfloat16
I32 = jnp.int32
HIGHEST = lax.Precision.HIGHEST

D_MODEL = 1024
HEAD_DIM = 64
RW_W = 512
ATT_W = 512
RW_HEADS = 8
ATT_HEADS = 8
IDX_HEADS = 4
IDX_DIM = 64
MAX_TOPK = 256
W_LORA = 64
A_LORA = 64
G_LORA = 128
GN_EPS = 64e-5
LN_EPS = 1e-5
D_FF = 2816
CONV_W = 3
NUM_BUCKETS = 32
MAX_DISTANCE = 128
PAGE_SIZE = 128
SHIFT_COLS = 3 * RW_W + W_LORA + A_LORA + G_LORA
LANES = 128
SUBLANES = 8
C_Q = SHIFT_COLS
C_K = C_Q + ATT_W
C_V = C_K + ATT_W
C_QI = C_V + ATT_W
C_KI = C_QI + IDX_HEADS * IDX_DIM
C_WI = C_KI + LANES
C_END = C_WI + LANES
VMEM_LIMIT = 56 * 1024 * 1024
INT_MIN = -(2 ** 31)
NEG_BIG = -1e30


def _cparams(sem):
    return pltpu.CompilerParams(dimension_semantics=sem, vmem_limit_bytes=VMEM_LIMIT)


def _const_spec(shape):
    zeros = (0,) * len(shape)
    return pl.BlockSpec(shape, lambda *_: zeros, pipeline_mode=pl.Buffered(1))


def _proj_kernel(x_ref, w_ref, prw_ref, qb_ref, k_ref, v_ref, kb_ref, vb_ref, qi_ref, kd_ref, kdb_ref, wi_ref):
    x = x_ref[...].astype(BF16)

    def mm(lo, hi):
        return jnp.dot(x, w_ref[:, lo:hi], preferred_element_type=F32)

    prw_ref[...] = mm(0, C_Q)
    qb_ref[...] = mm(C_Q, C_K).astype(BF16)
    k = mm(C_K, C_V)
    k_ref[...] = k
    kb_ref[...] = k.astype(BF16)
    v = mm(C_V, C_QI)
    v_ref[...] = v
    vb_ref[...] = v.astype(BF16)
    qi_ref[...] = mm(C_QI, C_KI)
    kd = mm(C_KI, C_WI)
    kd_ref[...] = kd
    kdb_ref[...] = kd.astype(BF16)
    wi_ref[...] = mm(C_WI, C_END)


def _proj(x2d, wp):
    m = x2d.shape[0]
    tm = min(256, m)
    widths = [(C_Q, F32), (ATT_W, BF16), (ATT_W, F32), (ATT_W, F32), (ATT_W, BF16), (ATT_W, BF16),
              (IDX_HEADS * IDX_DIM, F32), (LANES, F32), (LANES, BF16), (LANES, F32)]
    return pl.pallas_call(
        _proj_kernel,
        grid=(m // tm,),
        in_specs=[pl.BlockSpec((tm, D_MODEL), lambda i: (i, 0)), _const_spec((D_MODEL, C_END))],
        out_specs=[pl.BlockSpec((tm, w), lambda i: (i, 0)) for w, _ in widths],
        out_shape=[jax.ShapeDtypeStruct((m, w), dt) for w, dt in widths],
        compiler_params=_cparams(("arbitrary",)),
        name="proj",
    )(x2d, wp)


def _softplus(z):
    return jnp.maximum(z, 0.0) + jnp.log1p(jnp.exp(-jnp.abs(z)))


def _sigmoid(z):
    return 1.0 / (1.0 + jnp.exp(-z))


def _seg_sum(x, ones_bd):
    parts = [jnp.dot(x[:, g * LANES:(g + 1) * LANES], ones_bd, precision=HIGHEST, preferred_element_type=F32)
             for g in range(x.shape[1] // LANES)]
    return jnp.concatenate(parts, axis=1)


def _rwkv_kernel(p_ref, sp_ref, s0_ref, mu_ref, w0_ref, lw_ref, a0_ref, gup_ref, kk_ref, ka_ref, rk_ref,
                 gng_ref, gnb_ref, ones2_ref, diag_ref,
                 out_ref, sfin_ref, shift_ref,
                 pad_ref, carry_ref, s_ref, kkv_ref, dec_ref, b_ref, k2_ref, r_ref, v_ref, g_ref, bon_ref, y_ref,
                 *, nb, tc_len):
    tc = pl.program_id(1)
    last = pl.num_programs(1) - 1
    npair = RW_HEADS // 2

    @pl.when(tc == 0)
    def _():
        carry_ref[...] = sp_ref[...]
        for b in range(nb):
            for hp in range(npair):
                s_ref[b, hp] = jnp.concatenate([s0_ref[b, 2 * hp], s0_ref[b, 2 * hp + 1]], axis=1)

    rows = lax.broadcasted_iota(I32, (LANES, LANES), 0)
    cols = lax.broadcasted_iota(I32, (LANES, LANES), 1)
    ones_bd = (rows // HEAD_DIM == cols // HEAD_DIM).astype(F32)
    lane = lax.broadcasted_iota(I32, (tc_len, LANES), 1)

    for b in range(nb):
        x = p_ref[b]
        pad_ref[b, 7:8, :] = carry_ref[b]
        pad_ref[b, 8:8 + tc_len, :] = x
        xm1 = pad_ref[b, 7:7 + tc_len, :]
        ps = x + (xm1 - x) * mu_ref[...]
        carry_ref[b] = x[tc_len - 1:tc_len, :]
        r = ps[:, 0:RW_W]
        k = ps[:, RW_W:2 * RW_W]
        v = ps[:, 2 * RW_W:3 * RW_W]
        o = 3 * RW_W
        z = ps[:, o:o + LANES]
        z = jnp.where(lane < W_LORA, jnp.tanh(z), z)
        lo = jnp.dot(z, lw_ref[...], precision=HIGHEST, preferred_element_type=F32)
        w_log = -_softplus(-(w0_ref[...] + lo[:, 0:RW_W])) - 0.5
        dec = jnp.exp(-jnp.exp(w_log))
        a = _sigmoid(a0_ref[...] + lo[:, RW_W:2 * RW_W])
        gd = ps[:, o + LANES:o + 2 * LANES]
        g = jnp.dot(_sigmoid(gd), gup_ref[...], precision=HIGHEST, preferred_element_type=F32)
        kkx = k * kk_ref[...]
        ss = _seg_sum(kkx * kkx, ones_bd)
        kkn = kkx * lax.rsqrt(jnp.maximum(ss, 1e-24))
        k2 = k * (1.0 + (a - 1.0) * ka_ref[...])
        bonus = _seg_sum(r * k2 * rk_ref[...], ones_bd) * v
        kkv_ref[b, 0:tc_len] = kkn
        dec_ref[b, 0:tc_len] = dec
        b_ref[b, 0:tc_len] = kkn * a
        k2_ref[b, 0:tc_len] = k2
        r_ref[b, 0:tc_len] = r
        v_ref[b, 0:tc_len] = v
        g_ref[b, 0:tc_len] = g
        bon_ref[b, 0:tc_len] = bonus

    diag = diag_ref[...]
    diag_b = diag.astype(BF16)

    sub = min(SUBLANES, tc_len)
    riota = lax.broadcasted_iota(I32, (SUBLANES, LANES), 0)

    tiles = [(b, hp) for b in range(nb) for hp in range(npair)]
    nt = len(tiles)

    def bsum(xs):
        lhs = jnp.concatenate([jnp.concatenate([xs[2 * m], xs[2 * m + 1]], axis=1) for m in range(nt // 2)], axis=0)
        red = jnp.dot(lhs, ones2_ref[...], preferred_element_type=F32)
        return [red[(k // 2) * HEAD_DIM:(k // 2 + 1) * HEAD_DIM, (k % 2) * LANES:(k % 2 + 1) * LANES]
                for k in range(nt)]

    def group(gi, carry):
        t0 = pl.multiple_of(gi * SUBLANES, SUBLANES)
        rows8 = [[ref[b, pl.ds(t0, SUBLANES), hp * LANES:(hp + 1) * LANES]
                  for ref in (kkv_ref, dec_ref, b_ref, k2_ref, r_ref, v_ref)] for b, hp in tiles]
        vhi8 = [rows8[k][5].astype(BF16).astype(F32) for k in range(nt)]
        vmid8 = [rows8[k][5] - vhi8[k] for k in range(nt)]
        s = [s_ref[b, hp] for b, hp in tiles]
        y8 = [jnp.zeros((SUBLANES, LANES), F32) for _ in tiles]
        tile_of = lambda row: jnp.broadcast_to(row, (HEAD_DIM, LANES)).astype(BF16)
        for i in range(sub):
            kkr, wr, br, kr, rr = [[rows8[k][q][i:i + 1, :] for k in range(nt)] for q in range(5)]
            sa = bsum([(s[k] * kkr[k]).astype(BF16) for k in range(nt)])
            vh = bsum([tile_of(vhi8[k][i:i + 1, :]) * diag_b for k in range(nt)])
            vm = bsum([tile_of(vmid8[k][i:i + 1, :]) * diag_b for k in range(nt)])
            for k in range(nt):
                s[k] = s[k] * wr[k] - sa[k] * br[k] + (vh[k] + vm[k]) * kr[k]
            yb = bsum([(s[k] * rr[k]).astype(BF16) for k in range(nt)])
            for k in range(nt):
                yrow = jnp.sum(yb[k] * diag, axis=0, keepdims=True)
                y8[k] = jnp.where(riota == i, yrow, y8[k])
        for k, (b, hp) in enumerate(tiles):
            s_ref[b, hp] = s[k]
            y_ref[b, pl.ds(t0, SUBLANES), hp * LANES:(hp + 1) * LANES] = y8[k]
        return carry

    lax.fori_loop(0, max(1, tc_len // SUBLANES), group, 0)

    for b in range(nb):
        y = y_ref[b, 0:tc_len]
        mu = _seg_sum(y, ones_bd) * (1.0 / HEAD_DIM)
        yc = y - mu
        var = _seg_sum(yc * yc, ones_bd) * (1.0 / HEAD_DIM)
        yn = yc * lax.rsqrt(var + GN_EPS) * gng_ref[...] + gnb_ref[...]
        out_ref[b] = (yn + bon_ref[b, 0:tc_len]) * g_ref[b, 0:tc_len]

    @pl.when(tc == last)
    def _():
        shift_ref[...] = carry_ref[...]
        for b in range(nb):
            for hp in range(npair):
                s = s_ref[b, hp]
                sfin_ref[b, 2 * hp] = s[:, 0:HEAD_DIM]
                sfin_ref[b, 2 * hp + 1] = s[:, HEAD_DIM:2 * HEAD_DIM]


def _rwkv(p, shift_prev, wkv_prev, lw):
    bsz, t, _ = p.shape
    nb = 4
    tc_len = min(128, t)
    row = lambda a: a.reshape(1, -1)
    w_up, a_up = lw["rw_w_up"], lw["rw_a_up"]
    lora = jnp.zeros((LANES, 2 * RW_W), F32)
    lora = lora.at[0:W_LORA, 0:RW_W].set(w_up).at[W_LORA:LANES, RW_W:2 * RW_W].set(a_up)
    rr = jnp.arange(2 * LANES)[:, None]
    cc = jnp.arange(2 * LANES)[None, :]
    ones2 = (rr // HEAD_DIM == cc // HEAD_DIM).astype(BF16)
    diag = (jnp.arange(HEAD_DIM)[:, None] == cc[:, 0:LANES] % HEAD_DIM).astype(F32)
    blk3 = lambda w: pl.BlockSpec((nb, tc_len, w), lambda i, j: (i, j, 0))
    vec = lambda w: _const_spec((1, w))
    assert tc_len < SUBLANES or tc_len % SUBLANES == 0
    scr = lambda w: pltpu.VMEM((nb, max(SUBLANES, tc_len), w), F32)
    out, sfin, shift = pl.pallas_call(
        functools.partial(_rwkv_kernel, nb=nb, tc_len=tc_len),
        grid=(bsz // nb, t // tc_len),
        in_specs=[blk3(SHIFT_COLS),
                  pl.BlockSpec((nb, 1, SHIFT_COLS), lambda i, j: (i, 0, 0)),
                  pl.BlockSpec((nb, RW_HEADS, HEAD_DIM, HEAD_DIM), lambda i, j: (i, 0, 0, 0)),
                  vec(SHIFT_COLS), vec(RW_W), _const_spec((LANES, 2 * RW_W)), vec(RW_W),
                  _const_spec((G_LORA, RW_W)), vec(RW_W), vec(RW_W), vec(RW_W), vec(RW_W), vec(RW_W),
                  _const_spec((2 * LANES, 2 * LANES)), _const_spec((HEAD_DIM, LANES))],
        out_specs=[blk3(RW_W),
                   pl.BlockSpec((nb, RW_HEADS, HEAD_DIM, HEAD_DIM), lambda i, j: (i, 0, 0, 0)),
                   pl.BlockSpec((nb, 1, SHIFT_COLS), lambda i, j: (i, 0, 0))],
        out_shape=[jax.ShapeDtypeStruct((bsz, t, RW_W), F32),
                   jax.ShapeDtypeStruct((bsz, RW_HEADS, HEAD_DIM, HEAD_DIM), F32),
                   jax.ShapeDtypeStruct((bsz, 1, SHIFT_COLS), F32)],
        scratch_shapes=[pltpu.VMEM((nb, tc_len + 8, SHIFT_COLS), F32),
                        pltpu.VMEM((nb, 1, SHIFT_COLS), F32),
                        pltpu.VMEM((nb, RW_HEADS // 2, HEAD_DIM, LANES), F32)] + [scr(RW_W)] * 9,
        compiler_params=_cparams(("arbitrary", "arbitrary")),
        name="rwkv",
    )(p, shift_prev, wkv_prev, row(lw["shift_mu"]), row(lw["rw_w0"]), lora, row(lw["rw_a0"]), lw["rw_g_up"],
      row(lw["rw_k_k"]), row(lw["rw_k_a"]), row(lw["rw_r_k"]), row(lw["rw_gn_g"]), row(lw["rw_gn_b"]), ones2, diag)
    return out, sfin, shift


def _score_key(idx):
    idx = jnp.where(idx == 0.0, 0.0, idx)
    bits = lax.bitcast_convert_type(idx, I32)
    return jnp.where(bits >= 0, bits, bits ^ jnp.int32(0x7FFFFFFF))


def _fold_lanes(m):
    acc = m[:, 0:LANES]
    for c in range(1, m.shape[1] // LANES):
        acc = acc + m[:, c * LANES:(c + 1) * LANES]
    return acc


def _select_params(key_ref, nblk, blk, topk, rows, pos_bits):
    kf = jnp.float32(topk)

    def count(pred):
        def body(j, acc):
            off = pl.multiple_of(j * blk, blk)
            kb = key_ref[:, pl.ds(off, blk)]
            pos = off + lax.broadcasted_iota(I32, (rows, blk), 1)
            return acc + _fold_lanes(pred(kb, pos).astype(F32))
        acc = lax.fori_loop(0, nblk, body, jnp.zeros((rows, LANES), F32))
        return jnp.sum(acc, axis=1, keepdims=True)

    zero = jnp.zeros((rows, 1), I32)
    c0 = count(lambda kb, pos: kb >= zero)
    theta = jnp.where(c0 >= kf, zero, jnp.full((rows, 1), INT_MIN, I32))

    def bit_body(i, theta):
        cand = theta | jnp.left_shift(jnp.int32(1), jnp.int32(30) - i)
        c = count(lambda kb, pos: kb >= cand)
        return jnp.where(c >= kf, cand, theta)

    theta = lax.fori_loop(0, 31, bit_body, theta)
    n_gt = count(lambda kb, pos: kb > theta)
    n_eq = count(lambda kb, pos: kb == theta)
    need = kf - n_gt
    any_excess = jnp.max(n_eq - need) > 0.0

    def tie_cut():
        def pbody(i, cut):
            cand = cut | jnp.left_shift(jnp.int32(1), jnp.int32(pos_bits - 1) - i)
            c = count(lambda kb, pos: (kb == theta) & (pos < cand))
            return jnp.where(c < need, cand, cut)
        return lax.fori_loop(0, pos_bits, pbody, jnp.zeros((rows, 1), I32))

    cut = lax.cond(any_excess, tie_cut, lambda: jnp.full((rows, 1), 2 ** 30, I32))
    return theta, cut


def _t5_bucket(dist):
    n = jnp.maximum(dist, 0)
    max_exact = NUM_BUCKETS // 2
    nf = jnp.maximum(n, 1).astype(F32)
    large = max_exact + (jnp.log(nf / max_exact) / math.log(MAX_DISTANCE / max_exact)
                         * (NUM_BUCKETS - max_exact)).astype(I32)
    large = jnp.minimum(large, NUM_BUCKETS - 1)
    return jnp.where(n < max_exact, n, large)


def _near_bias_kernel(rb_ref, o_ref, *, tq):
    r = lax.broadcasted_iota(I32, (2 * tq, 2 * tq), 0)
    c = lax.broadcasted_iota(I32, (2 * tq, 2 * tq), 1)
    top = r < tq
    bucket = _t5_bucket(jnp.where(top, r, r - tq) - c + tq)
    for p in range(ATT_HEADS // 2):
        acc = jnp.zeros((2 * tq, 2 * tq), F32)
        for bk in range(NUM_BUCKETS):
            val = jnp.where(top, rb_ref[bk, 2 * p], rb_ref[bk, 2 * p + 1])
            acc = jnp.where(bucket == bk, val, acc)
        o_ref[p] = acc


def _near_bias(rel_bias, tq):
    return pl.pallas_call(
        functools.partial(_near_bias_kernel, tq=tq),
        in_specs=[pl.BlockSpec(memory_space=pltpu.SMEM)],
        out_specs=pl.BlockSpec(memory_space=pltpu.VMEM),
        out_shape=jax.ShapeDtypeStruct((ATT_HEADS // 2, 2 * tq, 2 * tq), F32),
        compiler_params=pltpu.CompilerParams(vmem_limit_bytes=VMEM_LIMIT),
        name="near_bias",
    )(rel_bias)


def _attn_prompt_kernel(rb_ref, qb_ref, qi_ref, wi_ref, kb_ref, vb_ref, kdb_ref, near_ref, o_ref,
                        key_ref, qbd_ref, qibd_ref, wsc_ref, m_ref, l_ref, acc_ref, *, tq, topk, pos_bits):
    i = pl.program_id(1)
    npair = ATT_HEADS // 2
    lane = lax.broadcasted_iota(I32, (tq, LANES), 1)
    left = lane < HEAD_DIM

    for p in range(npair):
        qp = qb_ref[0, :, p * LANES:(p + 1) * LANES] * jnp.asarray(HEAD_DIM ** -0.5, BF16)
        zero = jnp.zeros_like(qp)
        qbd_ref[p] = jnp.concatenate([jnp.where(left, qp, zero), jnp.where(left, zero, qp)], axis=0)
    for p in range(IDX_HEADS // 2):
        qp = qi_ref[0, :, p * LANES:(p + 1) * LANES].astype(BF16)
        zero = jnp.zeros_like(qp)
        qibd_ref[p] = jnp.concatenate([jnp.where(left, qp, zero), jnp.where(left, zero, qp)], axis=0)
    wi = wi_ref[0]
    scale = IDX_HEADS ** -0.5 * IDX_DIM ** -0.5
    for h in range(IDX_HEADS):
        wsc_ref[h] = jnp.broadcast_to(wi[:, h:h + 1] * scale, (tq, LANES))

    trow = i * tq + lax.broadcasted_iota(I32, (tq, tq), 0)
    ccol = lax.broadcasted_iota(I32, (tq, tq), 1)

    def idx_body(j, carry):
        off = pl.multiple_of(j * tq, tq)
        kd = kdb_ref[0, pl.ds(off, tq), :]
        idx = jnp.zeros((tq, tq), F32)
        for p in range(IDX_HEADS // 2):
            sc = lax.dot_general(qibd_ref[p], kd, (((1,), (1,)), ((), ())), preferred_element_type=F32)
            sc = jnp.maximum(sc, 0.0)
            w0 = wsc_ref[2 * p]
            w1 = wsc_ref[2 * p + 1]
            idx = idx + sc[0:tq] * jnp.concatenate([w0] * (tq // LANES), axis=1)
            idx = idx + sc[tq:2 * tq] * jnp.concatenate([w1] * (tq // LANES), axis=1)
        idx = jnp.where(off + ccol <= trow, idx, -jnp.inf)
        key_ref[:, pl.ds(off, tq)] = _score_key(idx)
        return carry

    lax.fori_loop(0, i + 1, idx_body, 0)

    theta, cut = _select_params(key_ref, i + 1, tq, topk, tq, pos_bits)

    m_ref[...] = jnp.full(m_ref.shape, NEG_BIG, F32)
    l_ref[...] = jnp.zeros(l_ref.shape, F32)
    acc_ref[...] = jnp.zeros(acc_ref.shape, F32)
    toprow = lax.broadcasted_iota(I32, (2 * tq, 1), 0) < tq

    ones_v = jnp.ones((tq, LANES), BF16)
    wide = lambda a: jnp.concatenate([a] * (tq // LANES), axis=1)

    def attend(j, near_bias):
        off = pl.multiple_of(j * tq, tq)
        kb = key_ref[:, pl.ds(off, tq)]
        pos = off + ccol
        sel = ((kb > theta) | ((kb == theta) & (pos <= cut))) & (pos <= trow)
        mb = jnp.where(sel, 0.0, -jnp.inf)
        mb = jnp.concatenate([mb, mb], axis=0)
        for p in range(npair):
            kblk = kb_ref[0, pl.ds(off, tq), p * LANES:(p + 1) * LANES]
            vblk = vb_ref[0, pl.ds(off, tq), p * LANES:(p + 1) * LANES]
            s = lax.dot_general(qbd_ref[p], kblk, (((1,), (1,)), ((), ())), preferred_element_type=F32)
            if near_bias is None:
                logits = s + mb
                const = jnp.where(toprow, rb_ref[NUM_BUCKETS - 1, 2 * p], rb_ref[NUM_BUCKETS - 1, 2 * p + 1])
            else:
                logits = s + (near_bias(p) + mb)
                const = 0.0
            m_old = m_ref[p]
            m_new = jnp.maximum(m_old, jnp.max(logits, axis=1, keepdims=True) + const)
            alpha = jnp.exp(m_old - m_new)
            pr = jnp.exp(logits - wide(m_new - const))
            pv = jnp.dot(pr.astype(BF16), jnp.concatenate([vblk, ones_v], axis=1), preferred_element_type=F32)
            l_ref[p] = alpha * l_ref[p] + pv[:, LANES:2 * LANES]
            acc_ref[p] = alpha * acc_ref[p] + pv[:, 0:LANES]
            m_ref[p] = m_new

    def far_body(j, carry):
        attend(j, None)
        return carry

    lax.fori_loop(0, jnp.maximum(i - 1, 0), far_body, 0)

    @pl.when(i >= 1)
    def _():
        attend(i - 1, lambda p: near_ref[p, :, 0:tq])

    attend(i, lambda p: near_ref[p, :, tq:2 * tq])

    for p in range(npair):
        o = acc_ref[p] / l_ref[p]
        o_ref[0, :, p * LANES:(p + 1) * LANES] = jnp.where(left, o[0:tq], o[tq:2 * tq])


def _attn_prompt(qb, qi, wi, kb, vb, kdb, rel_bias, bsz, t):
    tq = min(256, t)
    topk = min(MAX_TOPK, t // 4)
    pos_bits = max(1, (t - 1).bit_length())
    near = _near_bias(rel_bias, tq)
    r3 = lambda a: a.reshape(bsz, t, a.shape[-1])
    qtile = lambda w: pl.BlockSpec((1, tq, w), lambda b, i: (b, i, 0))
    full = lambda w: pl.BlockSpec((1, t, w), lambda b, i: (b, 0, 0))
    npair = ATT_HEADS // 2
    return pl.pallas_call(
        functools.partial(_attn_prompt_kernel, tq=tq, topk=topk, pos_bits=pos_bits),
        grid=(bsz, t // tq),
        in_specs=[pl.BlockSpec(memory_space=pltpu.SMEM),
                  qtile(ATT_W), qtile(IDX_HEADS * IDX_DIM), qtile(LANES),
                  full(ATT_W), full(ATT_W), full(LANES),
                  _const_spec((npair, 2 * tq, 2 * tq))],
        out_specs=qtile(ATT_W),
        out_shape=jax.ShapeDtypeStruct((bsz, t, ATT_W), F32),
        scratch_shapes=[pltpu.VMEM((tq, t), I32),
                        pltpu.VMEM((npair, 2 * tq, LANES), BF16),
                        pltpu.VMEM((IDX_HEADS // 2, 2 * tq, LANES), BF16),
                        pltpu.VMEM((IDX_HEADS, tq, LANES), F32),
                        pltpu.VMEM((npair, 2 * tq, LANES), F32),
                        pltpu.VMEM((npair, 2 * tq, LANES), F32),
                        pltpu.VMEM((npair, 2 * tq, LANES), F32)],
        compiler_params=_cparams(("arbitrary", "arbitrary")),
        name="attn_prompt",
    )(rel_bias, r3(qb), r3(qi), r3(wi), r3(kb), r3(vb), r3(kdb), near)


def _sample_index_kernel(pt_ref, qi_ref, wi_ref, *rest, n_pages, ts):
    page_refs = rest[:n_pages + 1]
    o_ref = rest[n_pages + 1]
    qi = qi_ref[0]
    q16 = jnp.concatenate([qi[:, h * IDX_DIM:(h + 1) * IDX_DIM] for h in range(IDX_HEADS)], axis=0).astype(BF16)
    wi = wi_ref[0]
    scale = IDX_HEADS ** -0.5 * IDX_DIM ** -0.5
    w16 = jnp.concatenate([wi[:, h:h + 1] for h in range(IDX_HEADS)], axis=0) * scale
    trow = lax.broadcasted_iota(I32, (ts, PAGE_SIZE), 0)
    ncol = lax.broadcasted_iota(I32, (ts, PAGE_SIZE), 1)
    for j in range(n_pages + 1):
        kp = page_refs[j][0].astype(BF16)
        sc = lax.dot_general(q16, kp, (((1,), (1,)), ((), ())), preferred_element_type=F32)
        rw = jnp.maximum(sc, 0.0) * w16
        idx = rw[0:ts]
        for h in range(1, IDX_HEADS):
            idx = idx + rw[h * ts:(h + 1) * ts]
        if j == n_pages:
            idx = jnp.where(ncol <= trow, idx, -jnp.inf)
        o_ref[0, :, j * PAGE_SIZE:(j + 1) * PAGE_SIZE] = idx


def _sample_select_kernel(idx_ref, sel_ref, key_ref, *, topk, pos_bits):
    rows, lp = idx_ref.shape
    key_ref[...] = _score_key(idx_ref[...])
    theta, cut = _select_params(key_ref, 1, lp, topk, rows, pos_bits)
    kb = key_ref[...]
    pos = lax.broadcasted_iota(I32, (rows, lp), 1)
    sel = ((kb > theta) | ((kb == theta) & (pos <= cut))) & (idx_ref[...] > -jnp.inf)
    sel_ref[...] = sel.astype(F32)


def _sample_bias_kernel(rb_ref, near_ref, new_ref, far_ref, *, ts):
    rows = ts * ATT_HEADS
    width = PAGE_SIZE * ATT_HEADS

    def gather(bucket, rhead, shape):
        acc = jnp.zeros(shape, F32)
        for h in range(ATT_HEADS):
            for bk in range(NUM_BUCKETS):
                acc = jnp.where((bucket == bk) & (rhead == h), rb_ref[bk, h], acc)
        return acc

    r = lax.broadcasted_iota(I32, (rows, width), 0)
    c = lax.broadcasted_iota(I32, (rows, width), 1)
    near_ref[...] = gather(_t5_bucket(PAGE_SIZE + r // ATT_HEADS - c // ATT_HEADS), r % ATT_HEADS, (rows, width))
    r = lax.broadcasted_iota(I32, (rows, LANES), 0)
    c = lax.broadcasted_iota(I32, (rows, LANES), 1)
    dist = r // ATT_HEADS - c // ATT_HEADS
    ok = (dist >= 0) & (c // ATT_HEADS < ts) & (r % ATT_HEADS == c % ATT_HEADS)
    new_ref[...] = jnp.where(ok, gather(_t5_bucket(dist), r % ATT_HEADS, (rows, LANES)), -jnp.inf)
    far_ref[...] = gather(jnp.full((rows, LANES), NUM_BUCKETS - 1, I32), r % ATT_HEADS, (rows, LANES))


def _sample_attn_kernel(pt_ref, q_ref, sel_ref, eexp_ref, near_ref, new_ref, far_ref, *rest, n_pages, ts):
    k_refs = rest[:n_pages + 1]
    v_refs = rest[n_pages + 1:2 * n_pages + 2]
    o_ref = rest[2 * n_pages + 2]
    l_ref = rest[2 * n_pages + 3]
    rows = ts * ATT_HEADS
    width = PAGE_SIZE * ATT_HEADS
    q = q_ref[0]
    r = lax.broadcasted_iota(I32, (rows, width), 0)
    c = lax.broadcasted_iota(I32, (rows, width), 1)
    match = (r % ATT_HEADS) == (c % ATT_HEADS)
    sel_all = sel_ref[0].astype(BF16)

    def expand_sel(j, ncols):
        s = jnp.dot(sel_all[:, j * PAGE_SIZE:(j + 1) * PAGE_SIZE], eexp_ref[:, 0:ncols], preferred_element_type=F32)
        return s > 0.5

    for j in range(n_pages):
        kp = k_refs[j][0].reshape(width, HEAD_DIM).astype(BF16)
        s = lax.dot_general(q, kp, (((1,), (1,)), ((), ())), preferred_element_type=F32)
        bias = near_ref[...] if j == n_pages - 1 else far_ref[:, 0:1]
        ok = match & expand_sel(j, width)
        l_ref[:, j * width:(j + 1) * width] = jnp.where(ok, s * (HEAD_DIM ** -0.5) + bias, -jnp.inf)
    kn = k_refs[n_pages][0].astype(BF16)
    s = lax.dot_general(q, kn, (((1,), (1,)), ((), ())), preferred_element_type=F32)
    ok = expand_sel(n_pages, LANES)
    l_ref[:, n_pages * width:] = jnp.where(ok, s * (HEAD_DIM ** -0.5) + new_ref[...], -jnp.inf)

    logits = l_ref[...]
    m = jnp.max(logits, axis=1, keepdims=True)
    pr = jnp.exp(logits - m)
    den = jnp.sum(pr, axis=1, keepdims=True)
    prb = pr.astype(BF16)
    acc = jnp.dot(prb[:, n_pages * width:], v_refs[n_pages][0].astype(BF16), preferred_element_type=F32)
    for j in range(n_pages):
        vp = v_refs[j][0].reshape(width, HEAD_DIM).astype(BF16)
        acc = acc + jnp.dot(prb[:, j * width:(j + 1) * width], vp, preferred_element_type=F32)
    o_ref[0] = acc / den


def _attn_sample(qb, qi, wi, k_new, v_new, kd_new, cache_k, cache_v, cache_kidx, page_table, rel_bias, bsz, ts):
    n_pages = page_table.shape[1]
    n_pool = cache_k.shape[0]
    past_len = n_pages * PAGE_SIZE
    topk = min(MAX_TOPK, (past_len + ts) // 4)
    lp = (n_pages + 1) * PAGE_SIZE
    rows = ts * ATT_HEADS
    width = PAGE_SIZE * ATT_HEADS
    pt = page_table.reshape(-1).astype(I32)

    ki_pool = cache_kidx.reshape(n_pool, PAGE_SIZE, IDX_DIM)
    ki_new = jnp.zeros((bsz, PAGE_SIZE, IDX_DIM), F32).at[:, 0:ts].set(kd_new.reshape(bsz, ts, LANES)[..., 0:IDX_DIM])
    page_spec = lambda j, shp: pl.BlockSpec((1,) + shp, lambda b, ptr: (ptr[b * n_pages + j],) + (0,) * len(shp))
    per_b = lambda shp: pl.BlockSpec((1,) + shp, lambda b, ptr: (b,) + (0,) * len(shp))
    idx = pl.pallas_call(
        functools.partial(_sample_index_kernel, n_pages=n_pages, ts=ts),
        grid_spec=pltpu.PrefetchScalarGridSpec(
            num_scalar_prefetch=1, grid=(bsz,),
            in_specs=[per_b((ts, IDX_HEADS * IDX_DIM)), per_b((ts, LANES))]
                     + [page_spec(j, (PAGE_SIZE, IDX_DIM)) for j in range(n_pages)]
                     + [per_b((PAGE_SIZE, IDX_DIM))],
            out_specs=per_b((ts, lp))),
        out_shape=jax.ShapeDtypeStruct((bsz, ts, lp), F32),
        compiler_params=_cparams(("arbitrary",)),
        name="sample_index",
    )(pt, qi.reshape(bsz, ts, -1), wi.reshape(bsz, ts, LANES), *([ki_pool] * n_pages), ki_new)

    sel = pl.pallas_call(
        functools.partial(_sample_select_kernel, topk=topk, pos_bits=max(1, (lp - 1).bit_length())),
        out_shape=jax.ShapeDtypeStruct((bsz * ts, lp), F32),
        scratch_shapes=[pltpu.VMEM((bsz * ts, lp), I32)],
        compiler_params=pltpu.CompilerParams(vmem_limit_bytes=VMEM_LIMIT),
        name="sample_select",
    )(idx.reshape(bsz * ts, lp))

    near, newb, far = pl.pallas_call(
        functools.partial(_sample_bias_kernel, ts=ts),
        in_specs=[pl.BlockSpec(memory_space=pltpu.SMEM)],
        out_shape=[jax.ShapeDtypeStruct((rows, width), F32), jax.ShapeDtypeStruct((rows, LANES), F32),
                   jax.ShapeDtypeStruct((rows, LANES), F32)],
        compiler_params=pltpu.CompilerParams(vmem_limit_bytes=VMEM_LIMIT),
        name="sample_bias",
    )(rel_bias)

    eexp = (jnp.arange(PAGE_SIZE)[:, None] == jnp.arange(width)[None, :] // ATT_HEADS).astype(BF16)
    page4 = (PAGE_SIZE, ATT_HEADS, HEAD_DIM)
    pad_new = lambda a: jnp.zeros((bsz, LANES, HEAD_DIM), F32).at[:, 0:rows].set(a.reshape(bsz, rows, HEAD_DIM))
    const = lambda shp: pl.BlockSpec(shp, lambda b, ptr: (0,) * len(shp))
    out = pl.pallas_call(
        functools.partial(_sample_attn_kernel, n_pages=n_pages, ts=ts),
        grid_spec=pltpu.PrefetchScalarGridSpec(
            num_scalar_prefetch=1, grid=(bsz,),
            in_specs=[per_b((rows, HEAD_DIM)), per_b((rows, lp)),
                      const((PAGE_SIZE, width)), const((rows, width)), const((rows, LANES)), const((rows, LANES))]
                     + [page_spec(j, page4) for j in range(n_pages)] + [per_b((LANES, HEAD_DIM))]
                     + [page_spec(j, page4) for j in range(n_pages)] + [per_b((LANES, HEAD_DIM))],
            out_specs=per_b((rows, HEAD_DIM)),
            scratch_shapes=[pltpu.VMEM((rows, n_pages * width + LANES), F32)]),
        out_shape=jax.ShapeDtypeStruct((bsz, rows, HEAD_DIM), F32),
        compiler_params=_cparams(("arbitrary",)),
        name="sample_attn",
    )(pt, qb.reshape(bsz, rows, HEAD_DIM), jnp.repeat(sel.reshape(bsz, ts, lp), ATT_HEADS, axis=1), eexp, near, newb, far,
      *([cache_k] * n_pages), pad_new(k_new), *([cache_v] * n_pages), pad_new(v_new))
    return out.reshape(bsz * ts, ATT_W)


def _layer_norm(x, g, b):
    mu = jnp.mean(x, axis=-1, keepdims=True)
    xc = x - mu
    var = jnp.mean(xc * xc, axis=-1, keepdims=True)
    return xc * lax.rsqrt(var + LN_EPS) * g + b


def _ffn_kernel(x_ref, rw_ref, att_ref, prev_ref, wo_ref, g1_ref, b1_ref, wup_ref, cw_ref, cb_ref, wdn_ref,
                g2_ref, b2_ref, y_ref, nc_ref, pad_ref, *, tm, shift, off, alpha):
    ti = pl.program_id(1)
    data = off + 2 * shift

    @pl.when(ti == 0)
    def _():
        pad_ref[off:data, :] = prev_ref[0]

    x = x_ref[0]
    mix = (jnp.dot(rw_ref[0].astype(BF16), wo_ref[0:RW_W, :], preferred_element_type=F32)
           + jnp.dot(att_ref[0].astype(BF16), wo_ref[RW_W:RW_W + ATT_W, :], preferred_element_type=F32))
    x1 = _layer_norm(alpha * x + mix, g1_ref[...], b1_ref[...])
    x1b = x1.astype(BF16)
    u_conv = jnp.dot(x1b, wup_ref[:, 0:D_FF], preferred_element_type=F32)
    u_lin = jnp.dot(x1b, wup_ref[:, D_FF:2 * D_FF], preferred_element_type=F32)
    pad_ref[data:data + tm, :] = u_conv
    conv = (pad_ref[off:off + tm, :] * cw_ref[0:1, :] + pad_ref[off + shift:off + shift + tm, :] * cw_ref[1:2, :]
            + u_conv * cw_ref[2:3, :] + cb_ref[...])
    h = conv * _sigmoid(conv) * u_lin
    down = jnp.dot(h.astype(BF16), wdn_ref[...], preferred_element_type=F32)
    y_ref[0] = _layer_norm(alpha * x1 + down, g2_ref[...], b2_ref[...])
    tail = pad_ref[off + tm:data + tm, :]
    pad_ref[off:data, :] = tail
    nc_ref[0] = tail


def _ffn(x, rw, att, prev, lw, alpha, shift):
    nb, rows, _ = x.shape
    tm = min(256, rows) if shift == 1 else rows
    off = 6 if shift == 1 else 0
    row = lambda a: a.reshape(1, -1)
    tile = lambda w: pl.BlockSpec((1, tm, w), lambda b, i: (b, i, 0))
    y, nc = pl.pallas_call(
        functools.partial(_ffn_kernel, tm=tm, shift=shift, off=off, alpha=alpha),
        grid=(nb, rows // tm),
        in_specs=[tile(D_MODEL), tile(RW_W), tile(ATT_W),
                  pl.BlockSpec((1, 2 * shift, D_FF), lambda b, i: (b, 0, 0)),
                  _const_spec((D_MODEL, D_MODEL)), _const_spec((1, D_MODEL)), _const_spec((1, D_MODEL)),
                  _const_spec((D_MODEL, 2 * D_FF)), _const_spec((CONV_W, D_FF)), _const_spec((1, D_FF)),
                  _const_spec((D_FF, D_MODEL)), _const_spec((1, D_MODEL)), _const_spec((1, D_MODEL))],
        out_specs=[tile(D_MODEL), pl.BlockSpec((1, 2 * shift, D_FF), lambda b, i: (b, 0, 0))],
        out_shape=[jax.ShapeDtypeStruct((nb, rows, D_MODEL), F32), jax.ShapeDtypeStruct((nb, 2 * shift, D_FF), F32)],
        scratch_shapes=[pltpu.VMEM((off + 2 * shift + tm, D_FF), F32)],
        compiler_params=_cparams(("arbitrary", "arbitrary")),
        name="ffn",
    )(x, rw, att, prev, lw["w_out"].astype(BF16), row(lw["ln1_g"]), row(lw["ln1_b"]),
      lw["ffn_w_up"].astype(BF16), lw["ffn_conv_w"], row(lw["ffn_conv_b"]), lw["ffn_w_down"].astype(BF16),
      row(lw["ln2_g"]), row(lw["ln2_b"]))
    return y, nc


def _layer(x, shift_prev, wkv_prev, conv_prev, past, rel_bias, lw, alpha):
    bsz, t, _ = x.shape
    w_in = lw["w_in"]
    w_ki = w_in[:, C_KI:C_KI + IDX_DIM]
    w_wi = w_in[:, C_KI + IDX_DIM:]
    wp = jnp.concatenate([w_in[:, 0:C_KI], w_ki, w_ki, w_wi,
                          jnp.zeros((D_MODEL, LANES - IDX_HEADS), F32)], axis=1).astype(BF16)
    prw, qb, k, v, kb, vb, qi, kd, kdb, wi = _proj(x.reshape(bsz * t, D_MODEL), wp)
    rw_out, new_wkv, new_shift = _rwkv(prw.reshape(bsz, t, SHIFT_COLS), shift_prev, wkv_prev, lw)
    if past is None:
        att = _attn_prompt(qb, qi, wi, kb, vb, kdb, rel_bias, bsz, t)
        y, new_conv = _ffn(x, rw_out, att.reshape(bsz, t, ATT_W), conv_prev, lw, alpha, shift=1)
    else:
        att = _attn_sample(qb, qi, wi, k, v, kd, *past, rel_bias, bsz, t)
        tmaj = lambda a: jnp.swapaxes(a.reshape(bsz, t, -1), 0, 1).reshape(1, t * bsz, -1)
        prev = jnp.swapaxes(conv_prev, 0, 1).reshape(1, (CONV_W - 1) * bsz, D_FF)
        y, new_conv = _ffn(tmaj(x), tmaj(rw_out), tmaj(att), prev, lw, alpha, shift=bsz)
        y = jnp.swapaxes(y.reshape(t, bsz, D_MODEL), 0, 1)
        new_conv = jnp.swapaxes(new_conv.reshape(CONV_W - 1, bsz, D_FF), 0, 1)
    state = (k.reshape(bsz, t, ATT_HEADS, HEAD_DIM), v.reshape(bsz, t, ATT_HEADS, HEAD_DIM),
             kd.reshape(bsz, t, LANES)[..., 0:IDX_DIM], new_wkv, new_shift, new_conv)
    return y, state


_LAYER_WEIGHTS = ("w_in", "shift_mu", "rw_w0", "rw_w_up", "rw_a0", "rw_a_up", "rw_g_up", "rw_k_k", "rw_k_a",
                  "rw_r_k", "rw_gn_g", "rw_gn_b", "w_out", "ln1_g", "ln1_b", "ffn_w_up", "ffn_conv_w",
                  "ffn_conv_b", "ffn_w_down", "ln2_g", "ln2_b")


def kernel(x_prompt, x_sample, cache_k, cache_v, cache_kidx, page_table, state_wkv, state_shift, state_conv,
           rel_bias, w_in, shift_mu, rw_w0, rw_w_up, rw_a0, rw_a_up, rw_g_up, rw_k_k, rw_k_a, rw_r_k, rw_gn_g,
           rw_gn_b, w_out, ln1_g, ln1_b, ffn_w_up, ffn_conv_w, ffn_conv_b, ffn_w_down, ln2_g, ln2_b):
    stacked = dict(zip(_LAYER_WEIGHTS, (w_in, shift_mu, rw_w0, rw_w_up, rw_a0, rw_a_up, rw_g_up, rw_k_k, rw_k_a,
                                        rw_r_k, rw_gn_g, rw_gn_b, w_out, ln1_g, ln1_b, ffn_w_up, ffn_conv_w,
                                        ffn_conv_b, ffn_w_down, ln2_g, ln2_b)))
    depth = w_in.shape[0]
    alpha = (2 * depth) ** 0.25
    bp = x_prompt.shape[0]
    yp, ys = x_prompt, x_sample
    outs_p, outs_s = [], []
    for l in range(depth):
        lw = {name: w[l] for name, w in stacked.items()}
        yp, st_p = _layer(yp, jnp.zeros((bp, 1, SHIFT_COLS), F32),
                          jnp.zeros((bp, RW_HEADS, HEAD_DIM, HEAD_DIM), F32),
                          jnp.zeros((bp, CONV_W - 1, D_FF), F32), None, rel_bias, lw, alpha)
        ys, st_s = _layer(ys, state_shift[l], state_wkv[l], state_conv[l],
                          (cache_k[l], cache_v[l], cache_kidx[l], page_table), rel_bias, lw, alpha)
        outs_p.append(st_p)
        outs_s.append(st_s)
    st = lambda outs, i: jnp.stack([o[i] for o in outs])
    return (yp, ys) + tuple(st(outs_p, i) for i in range(6)) + tuple(st(outs_s, i) for i in range(6))
```

```python
import functools
import math

import jax
import jax.numpy as jnp
from jax import lax
from jax.experimental import pallas as pl
from jax.experimental.pallas import tpu as pltpu

F32 = jnp.float32
BF16 = jnp.bfloat16
I32 = jnp.int32
HIGHEST = lax.Precision.HIGHEST

D_MODEL = 1024
HEAD_DIM = 64
RW_W = 512
ATT_W = 512
RW_HEADS = 8
ATT_HEADS = 8
IDX_HEADS = 4
IDX_DIM = 64
MAX_TOPK = 256
W_LORA = 64
A_LORA = 64
G_LORA = 128
GN_EPS = 64e-5
LN_EPS = 1e-5
D_FF = 2816
CONV_W = 3
NUM_BUCKETS = 32
MAX_DISTANCE = 128
PAGE_SIZE = 128
SHIFT_COLS = 3 * RW_W + W_LORA + A_LORA + G_LORA
LANES = 128
SUBLANES = 8
C_Q = SHIFT_COLS
C_K = C_Q + ATT_W
C_V = C_K + ATT_W
C_QI = C_V + ATT_W
C_KI = C_QI + IDX_HEADS * IDX_DIM
C_WI = C_KI + LANES
C_END = C_WI + LANES
VMEM_LIMIT = 56 * 1024 * 1024
INT_MIN = -(2 ** 31)
NEG_BIG = -1e30


def _cparams(sem):
    return pltpu.CompilerParams(dimension_semantics=sem, vmem_limit_bytes=VMEM_LIMIT)


def _const_spec(shape):
    zeros = (0,) * len(shape)
    return pl.BlockSpec(shape, lambda *_: zeros, pipeline_mode=pl.Buffered(1))


def _proj_kernel(x_ref, w_ref, prw_ref, qb_ref, k_ref, v_ref, kb_ref, vb_ref, qi_ref, kd_ref, kdb_ref, wi_ref):
    x = x_ref[...].astype(BF16)

    def mm(lo, hi):
        return jnp.dot(x, w_ref[:, lo:hi], preferred_element_type=F32)

    prw_ref[...] = mm(0, C_Q)
    qb_ref[...] = mm(C_Q, C_K).astype(BF16)
    k = mm(C_K, C_V)
    k_ref[...] = k
    kb_ref[...] = k.astype(BF16)
    v = mm(C_V, C_QI)
    v_ref[...] = v
    vb_ref[...] = v.astype(BF16)
    qi_ref[...] = mm(C_QI, C_KI)
    kd = mm(C_KI, C_WI)
    kd_ref[...] = kd
    kdb_ref[...] = kd.astype(BF16)
    wi_ref[...] = mm(C_WI, C_END)


def _proj(x2d, wp):
    m = x2d.shape[0]
    tm = min(256, m)
    widths = [(C_Q, F32), (ATT_W, BF16), (ATT_W, F32), (ATT_W, F32), (ATT_W, BF16), (ATT_W, BF16),
              (IDX_HEADS * IDX_DIM, F32), (LANES, F32), (LANES, BF16), (LANES, F32)]
    return pl.pallas_call(
        _proj_kernel,
        grid=(m // tm,),
        in_specs=[pl.BlockSpec((tm, D_MODEL), lambda i: (i, 0)), _const_spec((D_MODEL, C_END))],
        out_specs=[pl.BlockSpec((tm, w), lambda i: (i, 0)) for w, _ in widths],
        out_shape=[jax.ShapeDtypeStruct((m, w), dt) for w, dt in widths],
        compiler_params=_cparams(("arbitrary",)),
        name="proj",
    )(x2d, wp)


def _softplus(z):
    return jnp.maximum(z, 0.0) + jnp.log1p(jnp.exp(-jnp.abs(z)))


def _sigmoid(z):
    return 1.0 / (1.0 + jnp.exp(-z))


def _seg_sum(x, ones_bd):
    parts = [jnp.dot(x[:, g * LANES:(g + 1) * LANES], ones_bd, precision=HIGHEST, preferred_element_type=F32)
             for g in range(x.shape[1] // LANES)]
    return jnp.concatenate(parts, axis=1)


def _rwkv_kernel(p_ref, sp_ref, s0_ref, mu_ref, w0_ref, lw_ref, a0_ref, gup_ref, kk_ref, ka_ref, rk_ref,
                 gng_ref, gnb_ref, ones2_ref, diag_ref,
                 out_ref, sfin_ref, shift_ref,
                 pad_ref, carry_ref, s_ref, kkv_ref, dec_ref, b_ref, k2_ref, r_ref, v_ref, g_ref, bon_ref, y_ref,
                 *, nb, tc_len):
    tc = pl.program_id(1)
    last = pl.num_programs(1) - 1
    npair = RW_HEADS // 2

    @pl.when(tc == 0)
    def _():
        carry_ref[...] = sp_ref[...]
        for b in range(nb):
            for hp in range(npair):
                s_ref[b, hp] = jnp.concatenate([s0_ref[b, 2 * hp], s0_ref[b, 2 * hp + 1]], axis=1)

    rows = lax.broadcasted_iota(I32, (LANES, LANES), 0)
    cols = lax.broadcasted_iota(I32, (LANES, LANES), 1)
    ones_bd = (rows // HEAD_DIM == cols // HEAD_DIM).astype(F32)
    lane = lax.broadcasted_iota(I32, (tc_len, LANES), 1)

    for b in range(nb):
        x = p_ref[b]
        pad_ref[b, 7:8, :] = carry_ref[b]
        pad_ref[b, 8:8 + tc_len, :] = x
        xm1 = pad_ref[b, 7:7 + tc_len, :]
        ps = x + (xm1 - x) * mu_ref[...]
        carry_ref[b] = x[tc_len - 1:tc_len, :]
        r = ps[:, 0:RW_W]
        k = ps[:, RW_W:2 * RW_W]
        v = ps[:, 2 * RW_W:3 * RW_W]
        o = 3 * RW_W
        z = ps[:, o:o + LANES]
        z = jnp.where(lane < W_LORA, jnp.tanh(z), z)
        lo = jnp.dot(z, lw_ref[...], precision=HIGHEST, preferred_element_type=F32)
        w_log = -_softplus(-(w0_ref[...] + lo[:, 0:RW_W])) - 0.5
        dec = jnp.exp(-jnp.exp(w_log))
        a = _sigmoid(a0_ref[...] + lo[:, RW_W:2 * RW_W])
        gd = ps[:, o + LANES:o + 2 * LANES]
        g = jnp.dot(_sigmoid(gd), gup_ref[...], precision=HIGHEST, preferred_element_type=F32)
        kkx = k * kk_ref[...]
        ss = _seg_sum(kkx * kkx, ones_bd)
        kkn = kkx * lax.rsqrt(jnp.maximum(ss, 1e-24))
        k2 = k * (1.0 + (a - 1.0) * ka_ref[...])
        bonus = _seg_sum(r * k2 * rk_ref[...], ones_bd) * v
        kkv_ref[b, 0:tc_len] = kkn
        dec_ref[b, 0:tc_len] = dec
        b_ref[b, 0:tc_len] = kkn * a
        k2_ref[b, 0:tc_len] = k2
        r_ref[b, 0:tc_len] = r
        v_ref[b, 0:tc_len] = v
        g_ref[b, 0:tc_len] = g
        bon_ref[b, 0:tc_len] = bonus

    diag = diag_ref[...]
    diag_b = diag.astype(BF16)

    sub = min(SUBLANES, tc_len)
    riota = lax.broadcasted_iota(I32, (SUBLANES, LANES), 0)

    tiles = [(b, hp) for b in range(nb) for hp in range(npair)]
    nt = len(tiles)

    def bsum(xs):
        lhs = jnp.concatenate([jnp.concatenate([xs[2 * m], xs[2 * m + 1]], axis=1) for m in range(nt // 2)], axis=0)
        red = jnp.dot(lhs, ones2_ref[...], preferred_element_type=F32)
        return [red[(k // 2) * HEAD_DIM:(k // 2 + 1) * HEAD_DIM, (k % 2) * LANES:(k % 2 + 1) * LANES]
                for k in range(nt)]

    def group(gi, carry):
        t0 = pl.multiple_of(gi * SUBLANES, SUBLANES)
        rows8 = [[ref[b, pl.ds(t0, SUBLANES), hp * LANES:(hp + 1) * LANES]
                  for ref in (kkv_ref, dec_ref, b_ref, k2_ref, r_ref, v_ref)] for b, hp in tiles]
        vhi8 = [rows8[k][5].astype(BF16).astype(F32) for k in range(nt)]
        vmid8 = [rows8[k][5] - vhi8[k] for k in range(nt)]
        s = [s_ref[b, hp] for b, hp in tiles]
        y8 = [jnp.zeros((SUBLANES, LANES), F32) for _ in tiles]
        tile_of = lambda row: jnp.broadcast_to(row, (HEAD_DIM, LANES)).astype(BF16)
        for i in range(sub):
            kkr, wr, br, kr, rr = [[rows8[k][q][i:i + 1, :] for k in range(nt)] for q in range(5)]
            sa = bsum([(s[k] * kkr[k]).astype(BF16) for k in range(nt)])
            vh = bsum([tile_of(vhi8[k][i:i + 1, :]) * diag_b for k in range(nt)])
            vm = bsum([tile_of(vmid8[k][i:i + 1, :]) * diag_b for k in range(nt)])
            for k in range(nt):
                s[k] = s[k] * wr[k] - sa[k] * br[k] + (vh[k] + vm[k]) * kr[k]
            yb = bsum([(s[k] * rr[k]).astype(BF16) for k in range(nt)])
            for k in range(nt):
                yrow = jnp.sum(yb[k] * diag, axis=0, keepdims=True)
                y8[k] = jnp.where(riota == i, yrow, y8[k])
        for k, (b, hp) in enumerate(tiles):
            s_ref[b, hp] = s[k]
            y_ref[b, pl.ds(t0, SUBLANES), hp * LANES:(hp + 1) * LANES] = y8[k]
        return carry

    lax.fori_loop(0, max(1, tc_len // SUBLANES), group, 0)

    for b in range(nb):
        y = y_ref[b, 0:tc_len]
        mu = _seg_sum(y, ones_bd) * (1.0 / HEAD_DIM)
        yc = y - mu
        var = _seg_sum(yc * yc, ones_bd) * (1.0 / HEAD_DIM)
        yn = yc * lax.rsqrt(var + GN_EPS) * gng_ref[...] + gnb_ref[...]
        out_ref[b] = (yn + bon_ref[b, 0:tc_len]) * g_ref[b, 0:tc_len]

    @pl.when(tc == last)
    def _():
        shift_ref[...] = carry_ref[...]
        for b in range(nb):
            for hp in range(npair):
                s = s_ref[b, hp]
                sfin_ref[b, 2 * hp] = s[:, 0:HEAD_DIM]
                sfin_ref[b, 2 * hp + 1] = s[:, HEAD_DIM:2 * HEAD_DIM]


def _rwkv(p, shift_prev, wkv_prev, lw):
    bsz, t, _ = p.shape
    nb = 4
    tc_len = min(128, t)
    row = lambda a: a.reshape(1, -1)
    w_up, a_up = lw["rw_w_up"], lw["rw_a_up"]
    lora = jnp.zeros((LANES, 2 * RW_W), F32)
    lora = lora.at[0:W_LORA, 0:RW_W].set(w_up).at[W_LORA:LANES, RW_W:2 * RW_W].set(a_up)
    rr = jnp.arange(2 * LANES)[:, None]
    cc = jnp.arange(2 * LANES)[None, :]
    ones2 = (rr // HEAD_DIM == cc // HEAD_DIM).astype(BF16)
    diag = (jnp.arange(HEAD_DIM)[:, None] == cc[:, 0:LANES] % HEAD_DIM).astype(F32)
    blk3 = lambda w: pl.BlockSpec((nb, tc_len, w), lambda i, j: (i, j, 0))
    vec = lambda w: _const_spec((1, w))
    assert tc_len < SUBLANES or tc_len % SUBLANES == 0
    scr = lambda w: pltpu.VMEM((nb, max(SUBLANES, tc_len), w), F32)
    out, sfin, shift = pl.pallas_call(
        functools.partial(_rwkv_kernel, nb=nb, tc_len=tc_len),
        grid=(bsz // nb, t // tc_len),
        in_specs=[blk3(SHIFT_COLS),
                  pl.BlockSpec((nb, 1, SHIFT_COLS), lambda i, j: (i, 0, 0)),
                  pl.BlockSpec((nb, RW_HEADS, HEAD_DIM, HEAD_DIM), lambda i, j: (i, 0, 0, 0)),
                  vec(SHIFT_COLS), vec(RW_W), _const_spec((LANES, 2 * RW_W)), vec(RW_W),
                  _const_spec((G_LORA, RW_W)), vec(RW_W), vec(RW_W), vec(RW_W), vec(RW_W), vec(RW_W),
                  _const_spec((2 * LANES, 2 * LANES)), _const_spec((HEAD_DIM, LANES))],
        out_specs=[blk3(RW_W),
                   pl.BlockSpec((nb, RW_HEADS, HEAD_DIM, HEAD_DIM), lambda i, j: (i, 0, 0, 0)),
                   pl.BlockSpec((nb, 1, SHIFT_COLS), lambda i, j: (i, 0, 0))],
        out_shape=[jax.ShapeDtypeStruct((bsz, t, RW_W), F32),
                   jax.ShapeDtypeStruct((bsz, RW_HEADS, HEAD_DIM, HEAD_DIM), F32),
                   jax.ShapeDtypeStruct((bsz, 1, SHIFT_COLS), F32)],
        scratch_shapes=[pltpu.VMEM((nb, tc_len + 8, SHIFT_COLS), F32),
                        pltpu.VMEM((nb, 1, SHIFT_COLS), F32),
                        pltpu.VMEM((nb, RW_HEADS // 2, HEAD_DIM, LANES), F32)] + [scr(RW_W)] * 9,
        compiler_params=_cparams(("arbitrary", "arbitrary")),
        name="rwkv",
    )(p, shift_prev, wkv_prev, row(lw["shift_mu"]), row(lw["rw_w0"]), lora, row(lw["rw_a0"]), lw["rw_g_up"],
      row(lw["rw_k_k"]), row(lw["rw_k_a"]), row(lw["rw_r_k"]), row(lw["rw_gn_g"]), row(lw["rw_gn_b"]), ones2, diag)
    return out, sfin, shift


def _score_key(idx):
    idx = jnp.where(idx == 0.0, 0.0, idx)
    bits = lax.bitcast_convert_type(idx, I32)
    return jnp.where(bits >= 0, bits, bits ^ jnp.int32(0x7FFFFFFF))


def _fold_lanes(m):
    acc = m[:, 0:LANES]
    for c in range(1, m.shape[1] // LANES):
        acc = acc + m[:, c * LANES:(c + 1) * LANES]
    return acc


def _select_params(key_ref, nblk, blk, topk, rows, pos_bits):
    kf = jnp.float32(topk)

    def count(pred):
        def body(j, acc):
            off = pl.multiple_of(j * blk, blk)
            kb = key_ref[:, pl.ds(off, blk)]
            pos = off + lax.broadcasted_iota(I32, (rows, blk), 1)
            return acc + _fold_lanes(pred(kb, pos).astype(F32))
        acc = lax.fori_loop(0, nblk, body, jnp.zeros((rows, LANES), F32))
        return jnp.sum(acc, axis=1, keepdims=True)

    zero = jnp.zeros((rows, 1), I32)
    c0 = count(lambda kb, pos: kb >= zero)
    theta = jnp.where(c0 >= kf, zero, jnp.full((rows, 1), INT_MIN, I32))

    def bit_body(i, theta):
        cand = theta | jnp.left_shift(jnp.int32(1), jnp.int32(30) - i)
        c = count(lambda kb, pos: kb >= cand)
        return jnp.where(c >= kf, cand, theta)

    theta = lax.fori_loop(0, 31, bit_body, theta)
    n_gt = count(lambda kb, pos: kb > theta)
    n_eq = count(lambda kb, pos: kb == theta)
    need = kf - n_gt
    any_excess = jnp.max(n_eq - need) > 0.0

    def tie_cut():
        def pbody(i, cut):
            cand = cut | jnp.left_shift(jnp.int32(1), jnp.int32(pos_bits - 1) - i)
            c = count(lambda kb, pos: (kb == theta) & (pos < cand))
            return jnp.where(c < need, cand, cut)
        return lax.fori_loop(0, pos_bits, pbody, jnp.zeros((rows, 1), I32))

    cut = lax.cond(any_excess, tie_cut, lambda: jnp.full((rows, 1), 2 ** 30, I32))
    return theta, cut


def _t5_bucket(dist):
    n = jnp.maximum(dist, 0)
    max_exact = NUM_BUCKETS // 2
    nf = jnp.maximum(n, 1).astype(F32)
    large = max_exact + (jnp.log(nf / max_exact) / math.log(MAX_DISTANCE / max_exact)
                         * (NUM_BUCKETS - max_exact)).astype(I32)
    large = jnp.minimum(large, NUM_BUCKETS - 1)
    return jnp.where(n < max_exact, n, large)


def _near_bias_kernel(rb_ref, o_ref, *, tq):
    r = lax.broadcasted_iota(I32, (2 * tq, 2 * tq), 0)
    c = lax.broadcasted_iota(I32, (2 * tq, 2 * tq), 1)
    top = r < tq
    bucket = _t5_bucket(jnp.where(top, r, r - tq) - c + tq)
    for p in range(ATT_HEADS // 2):
        acc = jnp.zeros((2 * tq, 2 * tq), F32)
        for bk in range(NUM_BUCKETS):
            val = jnp.where(top, rb_ref[bk, 2 * p], rb_ref[bk, 2 * p + 1])
            acc = jnp.where(bucket == bk, val, acc)
        o_ref[p] = acc


def _near_bias(rel_bias, tq):
    return pl.pallas_call(
        functools.partial(_near_bias_kernel, tq=tq),
        in_specs=[pl.BlockSpec(memory_space=pltpu.SMEM)],
        out_specs=pl.BlockSpec(memory_space=pltpu.VMEM),
        out_shape=jax.ShapeDtypeStruct((ATT_HEADS // 2, 2 * tq, 2 * tq), F32),
        compiler_params=pltpu.CompilerParams(vmem_limit_bytes=VMEM_LIMIT),
        name="near_bias",
    )(rel_bias)


def _attn_prompt_kernel(rb_ref, qb_ref, qi_ref, wi_ref, kb_ref, vb_ref, kdb_ref, near_ref, o_ref,
                        key_ref, qbd_ref, qibd_ref, wsc_ref, m_ref, l_ref, acc_ref, *, tq, topk, pos_bits):
    i = pl.program_id(1)
    npair = ATT_HEADS // 2
    lane = lax.broadcasted_iota(I32, (tq, LANES), 1)
    left = lane < HEAD_DIM

    for p in range(npair):
        qp = qb_ref[0, :, p * LANES:(p + 1) * LANES] * jnp.asarray(HEAD_DIM ** -0.5, BF16)
        zero = jnp.zeros_like(qp)
        qbd_ref[p] = jnp.concatenate([jnp.where(left, qp, zero), jnp.where(left, zero, qp)], axis=0)
    for p in range(IDX_HEADS // 2):
        qp = qi_ref[0, :, p * LANES:(p + 1) * LANES].astype(BF16)
        zero = jnp.zeros_like(qp)
        qibd_ref[p] = jnp.concatenate([jnp.where(left, qp, zero), jnp.where(left, zero, qp)], axis=0)
    wi = wi_ref[0]
    scale = IDX_HEADS ** -0.5 * IDX_DIM ** -0.5
    for h in range(IDX_HEADS):
        wsc_ref[h] = jnp.broadcast_to(wi[:, h:h + 1] * scale, (tq, LANES))

    trow = i * tq + lax.broadcasted_iota(I32, (tq, tq), 0)
    ccol = lax.broadcasted_iota(I32, (tq, tq), 1)

    def idx_body(j, carry):
        off = pl.multiple_of(j * tq, tq)
        kd = kdb_ref[0, pl.ds(off, tq), :]
        idx = jnp.zeros((tq, tq), F32)
        for p in range(IDX_HEADS // 2):
            sc = lax.dot_general(qibd_ref[p], kd, (((1,), (1,)), ((), ())), preferred_element_type=F32)
            sc = jnp.maximum(sc, 0.0)
            w0 = wsc_ref[2 * p]
            w1 = wsc_ref[2 * p + 1]
            idx = idx + sc[0:tq] * jnp.concatenate([w0] * (tq // LANES), axis=1)
            idx = idx + sc[tq:2 * tq] * jnp.concatenate([w1] * (tq // LANES), axis=1)
        idx = jnp.where(off + ccol <= trow, idx, -jnp.inf)
        key_ref[:, pl.ds(off, tq)] = _score_key(idx)
        return carry

    lax.fori_loop(0, i + 1, idx_body, 0)

    theta, cut = _select_params(key_ref, i + 1, tq, topk, tq, pos_bits)

    m_ref[...] = jnp.full(m_ref.shape, NEG_BIG, F32)
    l_ref[...] = jnp.zeros(l_ref.shape, F32)
    acc_ref[...] = jnp.zeros(acc_ref.shape, F32)
    toprow = lax.broadcasted_iota(I32, (2 * tq, 1), 0) < tq

    ones_v = jnp.ones((tq, LANES), BF16)
    wide = lambda a: jnp.concatenate([a] * (tq // LANES), axis=1)

    def attend(j, near_bias):
        off = pl.multiple_of(j * tq, tq)
        kb = key_ref[:, pl.ds(off, tq)]
        pos = off + ccol
        sel = ((kb > theta) | ((kb == theta) & (pos <= cut))) & (pos <= trow)
        mb = jnp.where(sel, 0.0, -jnp.inf)
        mb = jnp.concatenate([mb, mb], axis=0)
        for p in range(npair):
            kblk = kb_ref[0, pl.ds(off, tq), p * LANES:(p + 1) * LANES]
            vblk = vb_ref[0, pl.ds(off, tq), p * LANES:(p + 1) * LANES]
            s = lax.dot_general(qbd_ref[p], kblk, (((1,), (1,)), ((), ())), preferred_element_type=F32)
            if near_bias is None:
                logits = s + mb
                const = jnp.where(toprow, rb_ref[NUM_BUCKETS - 1, 2 * p], rb_ref[NUM_BUCKETS - 1, 2 * p + 1])
            else:
                logits = s + (near_bias(p) + mb)
                const = 0.0
            m_old = m_ref[p]
            m_new = jnp.maximum(m_old, jnp.max(logits, axis=1, keepdims=True) + const)
            alpha = jnp.exp(m_old - m_new)
            pr = jnp.exp(logits - wide(m_new - const))
            pv = jnp.dot(pr.astype(BF16), jnp.concatenate([vblk, ones_v], axis=1), preferred_element_type=F32)
            l_ref[p] = alpha * l_ref[p] + pv[:, LANES:2 * LANES]
            acc_ref[p] = alpha * acc_ref[p] + pv[:, 0:LANES]
            m_ref[p] = m_new

    def far_body(j, carry):
        attend(j, None)
        return carry

    lax.fori_loop(0, jnp.maximum(i - 1, 0), far_body, 0)

    @pl.when(i >= 1)
    def _():
        attend(i - 1, lambda p: near_ref[p, :, 0:tq])

    attend(i, lambda p: near_ref[p, :, tq:2 * tq])

    for p in range(npair):
        o = acc_ref[p] / l_ref[p]
        o_ref[0, :, p * LANES:(p + 1) * LANES] = jnp.where(left, o[0:tq], o[tq:2 * tq])


def _attn_prompt(qb, qi, wi, kb, vb, kdb, rel_bias, bsz, t):
    tq = min(256, t)
    topk = min(MAX_TOPK, t // 4)
    pos_bits = max(1, (t - 1).bit_length())
    near = _near_bias(rel_bias, tq)
    r3 = lambda a: a.reshape(bsz, t, a.shape[-1])
    qtile = lambda w: pl.BlockSpec((1, tq, w), lambda b, i: (b, i, 0))
    full = lambda w: pl.BlockSpec((1, t, w), lambda b, i: (b, 0, 0))
    npair = ATT_HEADS // 2
    return pl.pallas_call(
        functools.partial(_attn_prompt_kernel, tq=tq, topk=topk, pos_bits=pos_bits),
        grid=(bsz, t // tq),
        in_specs=[pl.BlockSpec(memory_space=pltpu.SMEM),
                  qtile(ATT_W), qtile(IDX_HEADS * IDX_DIM), qtile(LANES),
                  full(ATT_W), full(ATT_W), full(LANES),
                  _const_spec((npair, 2 * tq, 2 * tq))],
        out_specs=qtile(ATT_W),
        out_shape=jax.ShapeDtypeStruct((bsz, t, ATT_W), F32),
        scratch_shapes=[pltpu.VMEM((tq, t), I32),
                        pltpu.VMEM((npair, 2 * tq, LANES), BF16),
                        pltpu.VMEM((IDX_HEADS // 2, 2 * tq, LANES), BF16),
                        pltpu.VMEM((IDX_HEADS, tq, LANES), F32),
                        pltpu.VMEM((npair, 2 * tq, LANES), F32),
                        pltpu.VMEM((npair, 2 * tq, LANES), F32),
                        pltpu.VMEM((npair, 2 * tq, LANES), F32)],
        compiler_params=_cparams(("arbitrary", "arbitrary")),
        name="attn_prompt",
    )(rel_bias, r3(qb), r3(qi), r3(wi), r3(kb), r3(vb), r3(kdb), near)


def _sample_index_kernel(pt_ref, qi_ref, wi_ref, *rest, n_pages, ts):
    page_refs = rest[:n_pages + 1]
    o_ref = rest[n_pages + 1]
    qi = qi_ref[0]
    q16 = jnp.concatenate([qi[:, h * IDX_DIM:(h + 1) * IDX_DIM] for h in range(IDX_HEADS)], axis=0).astype(BF16)
    wi = wi_ref[0]
    scale = IDX_HEADS ** -0.5 * IDX_DIM ** -0.5
    w16 = jnp.concatenate([wi[:, h:h + 1] for h in range(IDX_HEADS)], axis=0) * scale
    trow = lax.broadcasted_iota(I32, (ts, PAGE_SIZE), 0)
    ncol = lax.broadcasted_iota(I32, (ts, PAGE_SIZE), 1)
    for j in range(n_pages + 1):
        kp = page_refs[j][0].astype(BF16)
        if j < n_pages:
            sc = jnp.dot(q16, kp, preferred_element_type=F32)
        else:
            sc = lax.dot_general(q16, kp, (((1,), (1,)), ((), ())), preferred_element_type=F32)
        rw = jnp.maximum(sc, 0.0) * w16
        idx = rw[0:ts]
        for h in range(1, IDX_HEADS):
            idx = idx + rw[h * ts:(h + 1) * ts]
        if j == n_pages:
            idx = jnp.where(ncol <= trow, idx, -jnp.inf)
        o_ref[0, :, j * PAGE_SIZE:(j + 1) * PAGE_SIZE] = idx


def _sample_select_kernel(idx_ref, sel_ref, key_ref, *, topk, pos_bits):
    rows, lp = idx_ref.shape
    key_ref[...] = _score_key(idx_ref[...])
    theta, cut = _select_params(key_ref, 1, lp, topk, rows, pos_bits)
    kb = key_ref[...]
    pos = lax.broadcasted_iota(I32, (rows, lp), 1)
    sel = ((kb > theta) | ((kb == theta) & (pos <= cut))) & (idx_ref[...] > -jnp.inf)
    sel_ref[...] = sel.astype(F32)


def _sample_bias_kernel(rb_ref, near_ref, new_ref, far_ref, *, ts):
    rows = ts * ATT_HEADS
    r = lax.broadcasted_iota(I32, (rows, PAGE_SIZE), 0)
    c = lax.broadcasted_iota(I32, (rows, PAGE_SIZE), 1)
    t = r // ATT_HEADS
    rhead = r % ATT_HEADS

    def gather(bucket):
        acc = jnp.zeros((rows, PAGE_SIZE), F32)
        for h in range(ATT_HEADS):
            for bk in range(NUM_BUCKETS):
                acc = jnp.where((bucket == bk) & (rhead == h), rb_ref[bk, h], acc)
        return acc

    near_ref[...] = gather(_t5_bucket(PAGE_SIZE + t - c))
    ok = (c <= t) & (c < ts)
    new_ref[...] = jnp.where(ok, gather(_t5_bucket(t - c)), -jnp.inf)
    far_ref[...] = gather(jnp.full((rows, PAGE_SIZE), NUM_BUCKETS - 1, I32))


def _sample_attn_kernel(pt_ref, q_ref, sel_ref, near_ref, new_ref, far_ref, *rest, n_pages, ts):
    k_refs = rest[:n_pages + 1]
    v_refs = rest[n_pages + 1:2 * n_pages + 2]
    o_ref = rest[2 * n_pages + 2]
    l_ref = rest[2 * n_pages + 3]
    rows = ts * ATT_HEADS
    rep = lambda a: jnp.concatenate([jnp.broadcast_to(a[t:t + 1], (ATT_HEADS, a.shape[1])) for t in range(ts)], axis=0)
    r = lax.broadcasted_iota(I32, (rows, ATT_W), 0)
    c = lax.broadcasted_iota(I32, (rows, ATT_W), 1)
    own = (r % ATT_HEADS) == (c // HEAD_DIM)
    q = rep(q_ref[0].astype(F32) * HEAD_DIM ** -0.5)
    q = jnp.where(own, q, 0.0).astype(BF16)
    sel = rep(sel_ref[0]) > 0.5

    for j in range(n_pages):
        kt = k_refs[j][0].reshape(ATT_W, PAGE_SIZE).astype(BF16)
        s = jnp.dot(q, kt, preferred_element_type=F32)
        bias = near_ref[...] if j == n_pages - 1 else far_ref[...]
        sl = slice(j * PAGE_SIZE, (j + 1) * PAGE_SIZE)
        l_ref[:, sl] = jnp.where(sel[:, sl], s + bias, -jnp.inf)
    kn = k_refs[n_pages][0].astype(BF16)
    s = lax.dot_general(q, kn, (((1,), (1,)), ((), ())), preferred_element_type=F32)
    sl = slice(n_pages * PAGE_SIZE, (n_pages + 1) * PAGE_SIZE)
    l_ref[:, sl] = jnp.where(sel[:, sl], s + new_ref[...], -jnp.inf)

    logits = l_ref[...]
    m = jnp.max(logits, axis=1, keepdims=True)
    pr = jnp.exp(logits - m)
    den = jnp.sum(pr, axis=1, keepdims=True)
    prb = pr.astype(BF16)
    acc = jnp.dot(prb[:, sl], v_refs[n_pages][0].astype(BF16), preferred_element_type=F32)
    for j in range(n_pages):
        vt = v_refs[j][0].reshape(ATT_W, PAGE_SIZE).astype(BF16)
        acc = acc + lax.dot_general(prb[:, j * PAGE_SIZE:(j + 1) * PAGE_SIZE], vt, (((1,), (1,)), ((), ())),
                                    preferred_element_type=F32)
    acc = jnp.where(own, acc / den, 0.0)
    o_ref[0] = jnp.sum(acc.reshape(ts, ATT_HEADS, ATT_W), axis=1)


def _attn_sample(qb, qi, wi, k_new, v_new, kd_new, cache_k, cache_v, cache_kidx, page_table, rel_bias, bsz, ts):
    n_pages = page_table.shape[1]
    n_pool = cache_k.shape[0]
    past_len = n_pages * PAGE_SIZE
    topk = min(MAX_TOPK, (past_len + ts) // 4)
    lp = (n_pages + 1) * PAGE_SIZE
    rows = ts * ATT_HEADS
    width = PAGE_SIZE * ATT_HEADS
    pt = page_table.reshape(-1).astype(I32)

    ki_pool = jnp.swapaxes(cache_kidx, 1, 2)
    ki_new = jnp.zeros((bsz, PAGE_SIZE, IDX_DIM), F32).at[:, 0:ts].set(kd_new.reshape(bsz, ts, LANES)[..., 0:IDX_DIM])
    page_spec = lambda j, shp: pl.BlockSpec((1,) + shp, lambda b, ptr: (ptr[b * n_pages + j],) + (0,) * len(shp))
    per_b = lambda shp: pl.BlockSpec((1,) + shp, lambda b, ptr: (b,) + (0,) * len(shp))
    idx = pl.pallas_call(
        functools.partial(_sample_index_kernel, n_pages=n_pages, ts=ts),
        grid_spec=pltpu.PrefetchScalarGridSpec(
            num_scalar_prefetch=1, grid=(bsz,),
            in_specs=[per_b((ts, IDX_HEADS * IDX_DIM)), per_b((ts, LANES))]
                     + [page_spec(j, (IDX_DIM, PAGE_SIZE)) for j in range(n_pages)]
                     + [per_b((PAGE_SIZE, IDX_DIM))],
            out_specs=per_b((ts, lp))),
        out_shape=jax.ShapeDtypeStruct((bsz, ts, lp), F32),
        compiler_params=_cparams(("arbitrary",)),
        name="sample_index",
    )(pt, qi.reshape(bsz, ts, -1), wi.reshape(bsz, ts, LANES), *([ki_pool] * n_pages), ki_new)

    sel = pl.pallas_call(
        functools.partial(_sample_select_kernel, topk=topk, pos_bits=max(1, (lp - 1).bit_length())),
        out_shape=jax.ShapeDtypeStruct((bsz * ts, lp), F32),
        scratch_shapes=[pltpu.VMEM((bsz * ts, lp), I32)],
        compiler_params=pltpu.CompilerParams(vmem_limit_bytes=VMEM_LIMIT),
        name="sample_select",
    )(idx.reshape(bsz * ts, lp))

    near, newb, far = pl.pallas_call(
        functools.partial(_sample_bias_kernel, ts=ts),
        in_specs=[pl.BlockSpec(memory_space=pltpu.SMEM)],
        out_shape=[jax.ShapeDtypeStruct((rows, PAGE_SIZE), F32)] * 3,
        compiler_params=pltpu.CompilerParams(vmem_limit_bytes=VMEM_LIMIT),
        name="sample_bias",
    )(rel_bias)

    page4 = (ATT_HEADS, HEAD_DIM, PAGE_SIZE)
    k_pool = jnp.transpose(cache_k, (0, 2, 3, 1))
    v_pool = jnp.transpose(cache_v, (0, 2, 3, 1))
    pad_new = lambda a: jnp.zeros((bsz, PAGE_SIZE, ATT_W), F32).at[:, 0:ts].set(a.reshape(bsz, ts, ATT_W))
    const = lambda shp: pl.BlockSpec(shp, lambda b, ptr: (0,) * len(shp))
    out = pl.pallas_call(
        functools.partial(_sample_attn_kernel, n_pages=n_pages, ts=ts),
        grid_spec=pltpu.PrefetchScalarGridSpec(
            num_scalar_prefetch=1, grid=(bsz,),
            in_specs=[per_b((ts, ATT_W)), per_b((ts, lp)),
                      const((rows, PAGE_SIZE)), const((rows, PAGE_SIZE)), const((rows, PAGE_SIZE))]
                     + [page_spec(j, page4) for j in range(n_pages)] + [per_b((PAGE_SIZE, ATT_W))]
                     + [page_spec(j, page4) for j in range(n_pages)] + [per_b((PAGE_SIZE, ATT_W))],
            out_specs=per_b((ts, ATT_W)),
            scratch_shapes=[pltpu.VMEM((rows, lp), F32)]),
        out_shape=jax.ShapeDtypeStruct((bsz, ts, ATT_W), F32),
        compiler_params=_cparams(("arbitrary",)),
        name="sample_attn",
    )(pt, qb.reshape(bsz, ts, ATT_W), sel.reshape(bsz, ts, lp), near, newb, far,
      *([k_pool] * n_pages), pad_new(k_new), *([v_pool] * n_pages), pad_new(v_new))
    return out.reshape(bsz * ts, ATT_W)


def _layer_norm(x, g, b):
    mu = jnp.mean(x, axis=-1, keepdims=True)
    xc = x - mu
    var = jnp.mean(xc * xc, axis=-1, keepdims=True)
    return xc * lax.rsqrt(var + LN_EPS) * g + b


def _ffn_kernel(x_ref, rw_ref, att_ref, prev_ref, wo_ref, g1_ref, b1_ref, wup_ref, cw_ref, cb_ref, wdn_ref,
                g2_ref, b2_ref, y_ref, nc_ref, pad_ref, *, tm, shift, off, alpha):
    ti = pl.program_id(1)
    data = off + 2 * shift

    @pl.when(ti == 0)
    def _():
        pad_ref[off:data, :] = prev_ref[0]

    x = x_ref[0]
    mix = (jnp.dot(rw_ref[0].astype(BF16), wo_ref[0:RW_W, :], preferred_element_type=F32)
           + jnp.dot(att_ref[0].astype(BF16), wo_ref[RW_W:RW_W + ATT_W, :], preferred_element_type=F32))
    x1 = _layer_norm(alpha * x + mix, g1_ref[...], b1_ref[...])
    x1b = x1.astype(BF16)
    u_conv = jnp.dot(x1b, wup_ref[:, 0:D_FF], preferred_element_type=F32)
    u_lin = jnp.dot(x1b, wup_ref[:, D_FF:2 * D_FF], preferred_element_type=F32)
    pad_ref[data:data + tm, :] = u_conv
    conv = (pad_ref[off:off + tm, :] * cw_ref[0:1, :] + pad_ref[off + shift:off + shift + tm, :] * cw_ref[1:2, :]
            + u_conv * cw_ref[2:3, :] + cb_ref[...])
    h = conv * _sigmoid(conv) * u_lin
    down = jnp.dot(h.astype(BF16), wdn_ref[...], preferred_element_type=F32)
    y_ref[0] = _layer_norm(alpha * x1 + down, g2_ref[...], b2_ref[...])
    tail = pad_ref[off + tm:data + tm, :]
    pad_ref[off:data, :] = tail
    nc_ref[0] = tail


def _ffn(x, rw, att, prev, lw, alpha, shift):
    nb, rows, _ = x.shape
    tm = min(256, rows) if shift == 1 else rows
    off = 6 if shift == 1 else 0
    row = lambda a: a.reshape(1, -1)
    tile = lambda w: pl.BlockSpec((1, tm, w), lambda b, i: (b, i, 0))
    y, nc = pl.pallas_call(
        functools.partial(_ffn_kernel, tm=tm, shift=shift, off=off, alpha=alpha),
        grid=(nb, rows // tm),
        in_specs=[tile(D_MODEL), tile(RW_W), tile(ATT_W),
                  pl.BlockSpec((1, 2 * shift, D_FF), lambda b, i: (b, 0, 0)),
                  _const_spec((D_MODEL, D_MODEL)), _const_spec((1, D_MODEL)), _const_spec((1, D_MODEL)),
                  _const_spec((D_MODEL, 2 * D_FF)), _const_spec((CONV_W, D_FF)), _const_spec((1, D_FF)),
                  _const_spec((D_FF, D_MODEL)), _const_spec((1, D_MODEL)), _const_spec((1, D_MODEL))],
        out_specs=[tile(D_MODEL), pl.BlockSpec((1, 2 * shift, D_FF), lambda b, i: (b, 0, 0))],
        out_shape=[jax.ShapeDtypeStruct((nb, rows, D_MODEL), F32), jax.ShapeDtypeStruct((nb, 2 * shift, D_FF), F32)],
        scratch_shapes=[pltpu.VMEM((off + 2 * shift + tm, D_FF), F32)],
        compiler_params=_cparams(("arbitrary", "arbitrary")),
        name="ffn",
    )(x, rw, att, prev, lw["w_out"].astype(BF16), row(lw["ln1_g"]), row(lw["ln1_b"]),
      lw["ffn_w_up"].astype(BF16), lw["ffn_conv_w"], row(lw["ffn_conv_b"]), lw["ffn_w_down"].astype(BF16),
      row(lw["ln2_g"]), row(lw["ln2_b"]))
    return y, nc


def _layer(x, shift_prev, wkv_prev, conv_prev, past, rel_bias, lw, alpha):
    bsz, t, _ = x.shape
    w_in = lw["w_in"]
    w_ki = w_in[:, C_KI:C_KI + IDX_DIM]
    w_wi = w_in[:, C_KI + IDX_DIM:]
    wp = jnp.concatenate([w_in[:, 0:C_KI], w_ki, w_ki, w_wi,
                          jnp.zeros((D_MODEL, LANES - IDX_HEADS), F32)], axis=1).astype(BF16)
    prw, qb, k, v, kb, vb, qi, kd, kdb, wi = _proj(x.reshape(bsz * t, D_MODEL), wp)
    rw_out, new_wkv, new_shift = _rwkv(prw.reshape(bsz, t, SHIFT_COLS), shift_prev, wkv_prev, lw)
    if past is None:
        att = _attn_prompt(qb, qi, wi, kb, vb, kdb, rel_bias, bsz, t)
        y, new_conv = _ffn(x, rw_out, att.reshape(bsz, t, ATT_W), conv_prev, lw, alpha, shift=1)
    else:
        att = _attn_sample(qb, qi, wi, k, v, kd, *past, rel_bias, bsz, t)
        tmaj = lambda a: jnp.swapaxes(a.reshape(bsz, t, -1), 0, 1).reshape(1, t * bsz, -1)
        prev = jnp.swapaxes(conv_prev, 0, 1).reshape(1, (CONV_W - 1) * bsz, D_FF)
        y, new_conv = _ffn(tmaj(x), tmaj(rw_out), tmaj(att), prev, lw, alpha, shift=bsz)
        y = jnp.swapaxes(y.reshape(t, bsz, D_MODEL), 0, 1)
        new_conv = jnp.swapaxes(new_conv.reshape(CONV_W - 1, bsz, D_FF), 0, 1)
    state = (k.reshape(bsz, t, ATT_HEADS, HEAD_DIM), v.reshape(bsz, t, ATT_HEADS, HEAD_DIM),
             kd.reshape(bsz, t, LANES)[..., 0:IDX_DIM], new_wkv, new_shift, new_conv)
    return y, state


_LAYER_WEIGHTS = ("w_in", "shift_mu", "rw_w0", "rw_w_up", "rw_a0", "rw_a_up", "rw_g_up", "rw_k_k", "rw_k_a",
                  "rw_r_k", "rw_gn_g", "rw_gn_b", "w_out", "ln1_g", "ln1_b", "ffn_w_up", "ffn_conv_w",
                  "ffn_conv_b", "ffn_w_down", "ln2_g", "ln2_b")


def kernel(x_prompt, x_sample, cache_k, cache_v, cache_kidx, page_table, state_wkv, state_shift, state_conv,
           rel_bias, w_in, shift_mu, rw_w0, rw_w_up, rw_a0, rw_a_up, rw_g_up, rw_k_k, rw_k_a, rw_r_k, rw_gn_g,
           rw_gn_b, w_out, ln1_g, ln1_b, ffn_w_up, ffn_conv_w, ffn_conv_b, ffn_w_down, ln2_g, ln2_b):
    stacked = dict(zip(_LAYER_WEIGHTS, (w_in, shift_mu, rw_w0, rw_w_up, rw_a0, rw_a_up, rw_g_up, rw_k_k, rw_k_a,
                                        rw_r_k, rw_gn_g, rw_gn_b, w_out, ln1_g, ln1_b, ffn_w_up, ffn_conv_w,
                                        ffn_conv_b, ffn_w_down, ln2_g, ln2_b)))
    depth = w_in.shape[0]
    alpha = (2 * depth) ** 0.25
    bp = x_prompt.shape[0]
    yp, ys = x_prompt, x_sample
    outs_p, outs_s = [], []
    for l in range(depth):
        lw = {name: w[l] for name, w in stacked.items()}
        yp, st_p = _layer(yp, jnp.zeros((bp, 1, SHIFT_COLS), F32),
                          jnp.zeros((bp, RW_HEADS, HEAD_DIM, HEAD_DIM), F32),
                          jnp.zeros((bp, CONV_W - 1, D_FF), F32), None, rel_bias, lw, alpha)
        ys, st_s = _layer(ys, state_shift[l], state_wkv[l], state_conv[l],
                          (cache_k[l], cache_v[l], cache_kidx[l], page_table), rel_bias, lw, alpha)
        outs_p.append(st_p)
        outs_s.append(st_s)
    st = lambda outs, i: jnp.stack([o[i] for o in outs])
    return (yp, ys) + tuple(st(outs_p, i) for i in range(6)) + tuple(st(outs_s, i) for i in range(6))
```

```python
import functools
import math

import jax
import jax.numpy as jnp
from jax import lax
from jax.experimental import pallas as pl
from jax.experimental.pallas import tpu as pltpu

F32 = jnp.float32
BF16 = jnp.bfloat16
I32 = jnp.int32
HIGHEST = lax.Precision.HIGHEST

D_MODEL = 1024
HEAD_DIM = 64
RW_W = 512
ATT_W = 512
RW_HEADS = 8
ATT_HEADS = 8
IDX_HEADS = 4
IDX_DIM = 64
MAX_TOPK = 256
W_LORA = 64
A_LORA = 64
G_LORA = 128
GN_EPS = 64e-5
LN_EPS = 1e-5
D_FF = 2816
CONV_W = 3
NUM_BUCKETS = 32
MAX_DISTANCE = 128
PAGE_SIZE = 128
SHIFT_COLS = 3 * RW_W + W_LORA + A_LORA + G_LORA
LANES = 128
SUBLANES = 8
C_Q = SHIFT_COLS
C_K = C_Q + ATT_W
C_V = C_K + ATT_W
C_QI = C_V + ATT_W
C_KI = C_QI + IDX_HEADS * IDX_DIM
C_WI = C_KI + LANES
C_END = C_WI + LANES
VMEM_LIMIT = 56 * 1024 * 1024
INT_MIN = -(2 ** 31)
ORD_NEG_INF = INT_MIN + 0x7FFFFF
NEG_BIG = -1e30


def _cparams(sem):
    return pltpu.CompilerParams(dimension_semantics=sem, vmem_limit_bytes=VMEM_LIMIT)


def _const_spec(shape):
    zeros = (0,) * len(shape)
    return pl.BlockSpec(shape, lambda *_: zeros, pipeline_mode=pl.Buffered(1))


def _proj_kernel(x_ref, w_ref, prw_ref, qb_ref, k_ref, v_ref, qi_ref, kd_ref, wi_ref):
    x = x_ref[...].astype(BF16)

    def mm(lo, hi):
        return jnp.dot(x, w_ref[:, lo:hi], preferred_element_type=F32)

    prw_ref[...] = mm(0, C_Q)
    qb_ref[...] = mm(C_Q, C_K).astype(BF16)
    k_ref[...] = mm(C_K, C_V)
    v_ref[...] = mm(C_V, C_QI)
    qi_ref[...] = mm(C_QI, C_KI)
    kd_ref[...] = mm(C_KI, C_WI)
    wi_ref[...] = mm(C_WI, C_END)


def _proj(x2d, wp):
    m = x2d.shape[0]
    tm = min(256, m)
    widths = [(C_Q, F32), (ATT_W, BF16), (ATT_W, F32), (ATT_W, F32), (IDX_HEADS * IDX_DIM, F32), (LANES, F32),
              (LANES, F32)]
    return pl.pallas_call(
        _proj_kernel,
        grid=(m // tm,),
        in_specs=[pl.BlockSpec((tm, D_MODEL), lambda i: (i, 0)), _const_spec((D_MODEL, C_END))],
        out_specs=[pl.BlockSpec((tm, w), lambda i: (i, 0)) for w, _ in widths],
        out_shape=[jax.ShapeDtypeStruct((m, w), dt) for w, dt in widths],
        compiler_params=_cparams(("arbitrary",)),
        name="proj",
    )(x2d, wp)


def _proj_t_kernel(x_ref, w_ref, wt_ref, prw_ref, qt_ref, kb_ref, kt_ref, vt_ref, vbt_ref, qit_ref, kdb_ref, kit_ref,
                   wit_ref):
    x = x_ref[...].astype(BF16)

    def mm(lo, hi):
        return jnp.dot(x, w_ref[:, lo:hi], preferred_element_type=F32)

    def mt(lo, hi):
        return lax.dot_general(wt_ref[lo:hi, :], x, (((1,), (1,)), ((), ())), preferred_element_type=F32)

    prw_ref[...] = mm(0, C_Q)
    kb_ref[...] = mm(C_K, C_V).astype(BF16)
    kdb_ref[...] = mm(C_KI, C_WI).astype(BF16)
    qt_ref[0] = (mt(C_Q, C_K) * HEAD_DIM ** -0.5).astype(BF16)
    kt_ref[0] = mt(C_K, C_V)
    vt = mt(C_V, C_QI)
    vt_ref[0] = vt
    vbt_ref[0, 0] = vt.astype(BF16)
    qit_ref[0] = mt(C_QI, C_KI).astype(BF16)
    kit_ref[0] = mt(C_KI, C_KI + IDX_DIM)
    wit_ref[0] = mt(C_WI, C_WI + SUBLANES)


def _proj_t(x2d, wp, bsz, t, tm):
    m = x2d.shape[0]
    nblk = t // tm
    rows = lambda w: pl.BlockSpec((tm, w), lambda b, i: (b * nblk + i, 0))
    cols = lambda r: pl.BlockSpec((1, r, tm), lambda b, i: (b, 0, i))
    tshape = lambda r, dt: jax.ShapeDtypeStruct((bsz, r, t), dt)
    return pl.pallas_call(
        _proj_t_kernel,
        grid=(bsz, nblk),
        in_specs=[rows(D_MODEL), _const_spec((D_MODEL, C_END)), _const_spec((C_END, D_MODEL))],
        out_specs=[rows(C_Q), cols(ATT_W), rows(ATT_W), cols(ATT_W), cols(ATT_W),
                   pl.BlockSpec((1, 1, ATT_W, tm), lambda b, i: (b, i, 0, 0)),
                   cols(IDX_HEADS * IDX_DIM), rows(LANES), cols(IDX_DIM), cols(SUBLANES)],
        out_shape=[jax.ShapeDtypeStruct((m, C_Q), F32), tshape(ATT_W, BF16), jax.ShapeDtypeStruct((m, ATT_W), BF16),
                   tshape(ATT_W, F32), tshape(ATT_W, F32), jax.ShapeDtypeStruct((bsz, nblk, ATT_W, tm), BF16),
                   tshape(IDX_HEADS * IDX_DIM, BF16), jax.ShapeDtypeStruct((m, LANES), BF16),
                   tshape(IDX_DIM, F32), tshape(SUBLANES, F32)],
        compiler_params=_cparams(("arbitrary", "arbitrary")),
        name="proj_t",
    )(x2d, wp, wp.T)


def _softplus(z):
    return jnp.maximum(z, 0.0) + jnp.log1p(jnp.exp(-jnp.abs(z)))


def _sigmoid(z):
    return 1.0 / (1.0 + jnp.exp(-z))


def _seg_sum(x, ones_bd):
    parts = [jnp.dot(x[:, g * LANES:(g + 1) * LANES], ones_bd, precision=HIGHEST, preferred_element_type=F32)
             for g in range(x.shape[1] // LANES)]
    return jnp.concatenate(parts, axis=1)


def _rwkv_kernel(p_ref, sp_ref, s0_ref, mu_ref, w0_ref, lw_ref, a0_ref, gup_ref, kk_ref, ka_ref, rk_ref,
                 gng_ref, gnb_ref, ones2_ref, diag_ref,
                 out_ref, sfin_ref, shift_ref,
                 pad_ref, carry_ref, s_ref, kkv_ref, dec_ref, b_ref, k2_ref, r_ref, v_ref, g_ref, bon_ref, y_ref,
                 *, nb, tc_len):
    tc = pl.program_id(1)
    last = pl.num_programs(1) - 1
    npair = RW_HEADS // 2

    @pl.when(tc == 0)
    def _():
        carry_ref[...] = sp_ref[...]
        for b in range(nb):
            for hp in range(npair):
                s_ref[b, hp] = jnp.concatenate([s0_ref[b, 2 * hp], s0_ref[b, 2 * hp + 1]], axis=1)

    rows = lax.broadcasted_iota(I32, (LANES, LANES), 0)
    cols = lax.broadcasted_iota(I32, (LANES, LANES), 1)
    ones_bd = (rows // HEAD_DIM == cols // HEAD_DIM).astype(F32)
    lane = lax.broadcasted_iota(I32, (tc_len, LANES), 1)

    for b in range(nb):
        x = p_ref[b]
        pad_ref[b, 7:8, :] = carry_ref[b]
        pad_ref[b, 8:8 + tc_len, :] = x
        xm1 = pad_ref[b, 7:7 + tc_len, :]
        ps = x + (xm1 - x) * mu_ref[...]
        carry_ref[b] = x[tc_len - 1:tc_len, :]
        r = ps[:, 0:RW_W]
        k = ps[:, RW_W:2 * RW_W]
        v = ps[:, 2 * RW_W:3 * RW_W]
        o = 3 * RW_W
        z = ps[:, o:o + LANES]
        z = jnp.where(lane < W_LORA, jnp.tanh(z), z)
        lo = jnp.dot(z, lw_ref[...], precision=HIGHEST, preferred_element_type=F32)
        w_log = -_softplus(-(w0_ref[...] + lo[:, 0:RW_W])) - 0.5
        dec = jnp.exp(-jnp.exp(w_log))
        a = _sigmoid(a0_ref[...] + lo[:, RW_W:2 * RW_W])
        gd = ps[:, o + LANES:o + 2 * LANES]
        g = jnp.dot(_sigmoid(gd), gup_ref[...], precision=HIGHEST, preferred_element_type=F32)
        kkx = k * kk_ref[...]
        ss = _seg_sum(kkx * kkx, ones_bd)
        kkn = kkx * lax.rsqrt(jnp.maximum(ss, 1e-24))
        k2 = k * (1.0 + (a - 1.0) * ka_ref[...])
        bonus = _seg_sum(r * k2 * rk_ref[...], ones_bd) * v
        kkv_ref[b, 0:tc_len] = kkn
        dec_ref[b, 0:tc_len] = dec
        b_ref[b, 0:tc_len] = kkn * a
        k2_ref[b, 0:tc_len] = k2
        r_ref[b, 0:tc_len] = r
        v_ref[b, 0:tc_len] = v
        g_ref[b, 0:tc_len] = g
        bon_ref[b, 0:tc_len] = bonus

    diag = diag_ref[...]
    diag_b = diag.astype(BF16)
    drow = lax.broadcasted_iota(I32, (HEAD_DIM, LANES), 0)
    dlane = lax.broadcasted_iota(I32, (HEAD_DIM, LANES), 1)
    diag2_b = ((drow + 1) % HEAD_DIM == dlane % HEAD_DIM).astype(BF16)
    seg_start = lax.broadcasted_iota(I32, (SUBLANES, LANES), 1) % HEAD_DIM == 0

    sub = min(SUBLANES, tc_len)
    riota = lax.broadcasted_iota(I32, (SUBLANES, LANES), 0)

    tiles = [(b, hp) for b in range(nb) for hp in range(npair)]
    nt = len(tiles)

    def bsum(xs):
        lhs = jnp.concatenate([jnp.concatenate([xs[2 * m], xs[2 * m + 1]], axis=1) for m in range(nt // 2)], axis=0)
        red = jnp.dot(lhs, ones2_ref[...], preferred_element_type=F32)
        return [red[(k // 2) * HEAD_DIM:(k // 2 + 1) * HEAD_DIM, (k % 2) * LANES:(k % 2 + 1) * LANES]
                for k in range(nt)]

    def group(gi, carry):
        t0 = pl.multiple_of(gi * SUBLANES, SUBLANES)
        rows8 = [[ref[b, pl.ds(t0, SUBLANES), hp * LANES:(hp + 1) * LANES]
                  for ref in (kkv_ref, dec_ref, b_ref, k2_ref, r_ref, v_ref)] for b, hp in tiles]
        vhi8 = [rows8[k][5].astype(BF16).astype(F32) for k in range(nt)]
        vmid8 = [rows8[k][5] - vhi8[k] for k in range(nt)]
        vmid8 = [jnp.where(seg_start, pltpu.roll(x, LANES - HEAD_DIM + 1, 1), pltpu.roll(x, 1, 1)) for x in vmid8]
        s = [s_ref[b, hp] for b, hp in tiles]
        y8 = [jnp.zeros((SUBLANES, LANES), F32) for _ in tiles]
        tile_of = lambda row: jnp.broadcast_to(row, (HEAD_DIM, LANES)).astype(BF16)
        for i in range(sub):
            kkr, wr, br, kr, rr = [[rows8[k][q][i:i + 1, :] for k in range(nt)] for q in range(5)]
            sa = bsum([(s[k] * kkr[k]).astype(BF16) for k in range(nt)])
            vb = bsum([tile_of(vhi8[k][i:i + 1, :]) * diag_b + tile_of(vmid8[k][i:i + 1, :]) * diag2_b
                       for k in range(nt)])
            for k in range(nt):
                s[k] = s[k] * wr[k] - sa[k] * br[k] + vb[k] * kr[k]
            yb = bsum([(s[k] * rr[k]).astype(BF16) for k in range(nt)])
            for k in range(nt):
                yrow = jnp.sum(yb[k] * diag, axis=0, keepdims=True)
                y8[k] = jnp.where(riota == i, yrow, y8[k])
        for k, (b, hp) in enumerate(tiles):
            s_ref[b, hp] = s[k]
            y_ref[b, pl.ds(t0, SUBLANES), hp * LANES:(hp + 1) * LANES] = y8[k]
        return carry

    lax.fori_loop(0, max(1, tc_len // SUBLANES), group, 0)

    for b in range(nb):
        y = y_ref[b, 0:tc_len]
        mu = _seg_sum(y, ones_bd) * (1.0 / HEAD_DIM)
        yc = y - mu
        var = _seg_sum(yc * yc, ones_bd) * (1.0 / HEAD_DIM)
        yn = yc * lax.rsqrt(var + GN_EPS) * gng_ref[...] + gnb_ref[...]
        out_ref[b] = (yn + bon_ref[b, 0:tc_len]) * g_ref[b, 0:tc_len]

    @pl.when(tc == last)
    def _():
        shift_ref[...] = carry_ref[...]
        for b in range(nb):
            for hp in range(npair):
                s = s_ref[b, hp]
                sfin_ref[b, 2 * hp] = s[:, 0:HEAD_DIM]
                sfin_ref[b, 2 * hp + 1] = s[:, HEAD_DIM:2 * HEAD_DIM]


def _rwkv(p, shift_prev, wkv_prev, lw):
    bsz, t, _ = p.shape
    nb = 4
    tc_len = min(128, t)
    row = lambda a: a.reshape(1, -1)
    w_up, a_up = lw["rw_w_up"], lw["rw_a_up"]
    lora = jnp.zeros((LANES, 2 * RW_W), F32)
    lora = lora.at[0:W_LORA, 0:RW_W].set(w_up).at[W_LORA:LANES, RW_W:2 * RW_W].set(a_up)
    rr = jnp.arange(2 * LANES)[:, None]
    cc = jnp.arange(2 * LANES)[None, :]
    ones2 = (rr // HEAD_DIM == cc // HEAD_DIM).astype(BF16)
    diag = (jnp.arange(HEAD_DIM)[:, None] == cc[:, 0:LANES] % HEAD_DIM).astype(F32)
    blk3 = lambda w: pl.BlockSpec((nb, tc_len, w), lambda i, j: (i, j, 0))
    vec = lambda w: _const_spec((1, w))
    assert tc_len < SUBLANES or tc_len % SUBLANES == 0
    scr = lambda w: pltpu.VMEM((nb, max(SUBLANES, tc_len), w), F32)
    out, sfin, shift = pl.pallas_call(
        functools.partial(_rwkv_kernel, nb=nb, tc_len=tc_len),
        grid=(bsz // nb, t // tc_len),
        in_specs=[blk3(SHIFT_COLS),
                  pl.BlockSpec((nb, 1, SHIFT_COLS), lambda i, j: (i, 0, 0)),
                  pl.BlockSpec((nb, RW_HEADS, HEAD_DIM, HEAD_DIM), lambda i, j: (i, 0, 0, 0)),
                  vec(SHIFT_COLS), vec(RW_W), _const_spec((LANES, 2 * RW_W)), vec(RW_W),
                  _const_spec((G_LORA, RW_W)), vec(RW_W), vec(RW_W), vec(RW_W), vec(RW_W), vec(RW_W),
                  _const_spec((2 * LANES, 2 * LANES)), _const_spec((HEAD_DIM, LANES))],
        out_specs=[blk3(RW_W),
                   pl.BlockSpec((nb, RW_HEADS, HEAD_DIM, HEAD_DIM), lambda i, j: (i, 0, 0, 0)),
                   pl.BlockSpec((nb, 1, SHIFT_COLS), lambda i, j: (i, 0, 0))],
        out_shape=[jax.ShapeDtypeStruct((bsz, t, RW_W), F32),
                   jax.ShapeDtypeStruct((bsz, RW_HEADS, HEAD_DIM, HEAD_DIM), F32),
                   jax.ShapeDtypeStruct((bsz, 1, SHIFT_COLS), F32)],
        scratch_shapes=[pltpu.VMEM((nb, tc_len + 8, SHIFT_COLS), F32),
                        pltpu.VMEM((nb, 1, SHIFT_COLS), F32),
                        pltpu.VMEM((nb, RW_HEADS // 2, HEAD_DIM, LANES), F32)] + [scr(RW_W)] * 9,
        compiler_params=_cparams(("arbitrary", "arbitrary")),
        name="rwkv",
    )(p, shift_prev, wkv_prev, row(lw["shift_mu"]), row(lw["rw_w0"]), lora, row(lw["rw_a0"]), lw["rw_g_up"],
      row(lw["rw_k_k"]), row(lw["rw_k_a"]), row(lw["rw_r_k"]), row(lw["rw_gn_g"]), row(lw["rw_gn_b"]), ones2, diag)
    return out, sfin, shift


def _ordinal_to_score(o):
    o = jnp.maximum(o, ORD_NEG_INF)
    bits = jnp.where(o >= 0, o, o ^ jnp.int32(0x7FFFFFFF))
    return lax.bitcast_convert_type(bits, F32)


def _select_params(score_ref, nblk, blk, topk, nq, pos_bits):
    kf = jnp.float32(topk)

    def count(pred):
        def body(j, acc):
            off = pl.multiple_of(j * blk, blk)
            kb = score_ref[pl.ds(off, blk), :]
            pos = off + lax.broadcasted_iota(I32, (blk, nq), 0)
            hit = pred(kb, pos).astype(F32)
            return acc + jnp.sum(hit.reshape(blk // SUBLANES, SUBLANES, nq), axis=0)
        acc = lax.fori_loop(0, nblk, body, jnp.zeros((SUBLANES, nq), F32))
        return jnp.sum(acc, axis=0, keepdims=True)

    zero = jnp.zeros((1, nq), I32)
    c0 = count(lambda kb, pos: kb >= 0.0)
    ordinal = jnp.where(c0 >= kf, zero, jnp.full((1, nq), INT_MIN, I32))

    def bit_body(i, ordinal):
        cand = ordinal | jnp.left_shift(jnp.int32(1), jnp.int32(30) - i)
        cand_score = _ordinal_to_score(cand)
        c = count(lambda kb, pos: kb >= cand_score)
        return jnp.where(c >= kf, cand, ordinal)

    theta = _ordinal_to_score(lax.fori_loop(0, 31, bit_body, ordinal))
    n_gt = count(lambda kb, pos: kb > theta)
    n_eq = count(lambda kb, pos: kb == theta)
    need = kf - n_gt
    any_excess = jnp.max(n_eq - need) > 0.0

    def tie_cut():
        def pbody(i, cut):
            cand = cut | jnp.left_shift(jnp.int32(1), jnp.int32(pos_bits - 1) - i)
            c = count(lambda kb, pos: (kb == theta) & (pos < cand))
            return jnp.where(c < need, cand, cut)
        return lax.fori_loop(0, pos_bits, pbody, jnp.zeros((1, nq), I32))

    cut = lax.cond(any_excess, tie_cut, lambda: jnp.full((1, nq), 2 ** 30, I32))
    return theta, cut


def _t5_bucket(dist):
    n = jnp.maximum(dist, 0)
    max_exact = NUM_BUCKETS // 2
    nf = jnp.maximum(n, 1).astype(F32)
    large = max_exact + jnp.floor(jnp.log(nf / max_exact) / math.log(MAX_DISTANCE / max_exact)
                                  * (NUM_BUCKETS - max_exact)).astype(I32)
    large = jnp.minimum(large, NUM_BUCKETS - 1)
    return jnp.where(n < max_exact, n, large)


def _near_bias_kernel(rb_ref, o_ref, *, tq):
    c = lax.broadcasted_iota(I32, (2 * tq, 2 * tq), 0)
    l = lax.broadcasted_iota(I32, (2 * tq, 2 * tq), 1)
    first = l < tq
    bucket = _t5_bucket(jnp.where(first, l, l - tq) - c + tq)
    for p in range(ATT_HEADS // 2):
        acc = jnp.zeros((2 * tq, 2 * tq), F32)
        for bk in range(NUM_BUCKETS):
            val = jnp.where(first, rb_ref[bk, 2 * p], rb_ref[bk, 2 * p + 1])
            acc = jnp.where(bucket == bk, val, acc)
        o_ref[p] = acc


def _near_bias(rel_bias, tq):
    return pl.pallas_call(
        functools.partial(_near_bias_kernel, tq=tq),
        in_specs=[pl.BlockSpec(memory_space=pltpu.SMEM)],
        out_specs=pl.BlockSpec(memory_space=pltpu.VMEM),
        out_shape=jax.ShapeDtypeStruct((ATT_HEADS // 2, 2 * tq, 2 * tq), F32),
        compiler_params=pltpu.CompilerParams(vmem_limit_bytes=VMEM_LIMIT),
        name="near_bias",
    )(rel_bias)


def _attn_prompt_kernel(rb_ref, qt_ref, qit_ref, wit_ref, kb_ref, vbt_ref, kdb_ref, near_ref, o_ref,
                        key_ref, qbd_ref, qibd_ref, m_ref, l_ref, acc_ref, *, tq, topk, pos_bits):
    i = pl.program_id(1)
    npair = ATT_HEADS // 2
    upper = lax.broadcasted_iota(I32, (LANES, tq), 0) < HEAD_DIM

    def stack_pair(x):
        zero = jnp.zeros_like(x)
        return jnp.concatenate([jnp.where(upper, x, zero), jnp.where(upper, zero, x)], axis=1)

    for p in range(npair):
        qbd_ref[p] = stack_pair(qt_ref[0, p * LANES:(p + 1) * LANES, :])
    for p in range(IDX_HEADS // 2):
        qibd_ref[p] = stack_pair(qit_ref[0, p * LANES:(p + 1) * LANES, :])
    wrow = wit_ref[0] * (IDX_HEADS ** -0.5 * IDX_DIM ** -0.5)

    kpos = lax.broadcasted_iota(I32, (tq, tq), 0)
    qpos = i * tq + lax.broadcasted_iota(I32, (tq, tq), 1)

    def idx_body(j, carry):
        off = pl.multiple_of(j * tq, tq)
        kd = kdb_ref[0, pl.ds(off, tq), :]
        idx = jnp.zeros((tq, tq), F32)
        for p in range(IDX_HEADS // 2):
            sc = jnp.maximum(jnp.dot(kd, qibd_ref[p], preferred_element_type=F32), 0.0)
            idx = idx + sc[:, 0:tq] * wrow[2 * p:2 * p + 1, :] + sc[:, tq:2 * tq] * wrow[2 * p + 1:2 * p + 2, :]
        idx = jnp.where(idx == 0.0, 0.0, idx)
        key_ref[pl.ds(off, tq), :] = jnp.where(off + kpos <= qpos, idx, -jnp.inf)
        return carry

    lax.fori_loop(0, i + 1, idx_body, 0)

    theta, cut = _select_params(key_ref, i + 1, tq, topk, tq, pos_bits)

    m_ref[...] = jnp.full(m_ref.shape, NEG_BIG, F32)
    l_ref[...] = jnp.zeros(l_ref.shape, F32)
    acc_ref[...] = jnp.zeros(acc_ref.shape, F32)
    first = lax.broadcasted_iota(I32, (1, 2 * tq), 1) < tq
    ones_rows = jnp.ones((2 * SUBLANES, tq), BF16)

    def attend(j, near_rows):
        off = pl.multiple_of(j * tq, tq)
        kb = key_ref[pl.ds(off, tq), :]
        pos = off + kpos
        sel = ((kb > theta) | ((kb == theta) & (pos <= cut))) & (pos <= qpos)
        mb = jnp.where(sel, 0.0, -jnp.inf)
        mb = jnp.concatenate([mb, mb], axis=1)
        ss = [jnp.dot(kb_ref[0, pl.ds(off, tq), p * LANES:(p + 1) * LANES], qbd_ref[p], preferred_element_type=F32)
              for p in range(npair)]
        prs, alphas = [], []
        for p in range(npair):
            if near_rows is None:
                logits = ss[p] + mb
                const = jnp.where(first, rb_ref[NUM_BUCKETS - 1, 2 * p], rb_ref[NUM_BUCKETS - 1, 2 * p + 1])
            else:
                logits = ss[p] + (near_ref[p, near_rows:near_rows + tq, :] + mb)
                const = jnp.zeros((1, 2 * tq), F32)
            m_old = m_ref[p]
            m_new = jnp.maximum(m_old, jnp.max(logits, axis=0, keepdims=True) + const)
            alphas.append(jnp.exp(m_old - m_new))
            prs.append(jnp.exp(logits - (m_new[0:1, :] - const)).astype(BF16))
            m_ref[p] = m_new
        for p in range(npair):
            vt = jnp.concatenate([vbt_ref[0, j, p * LANES:(p + 1) * LANES, :], ones_rows], axis=0)
            pv = jnp.dot(vt, prs[p], preferred_element_type=F32)
            l_ref[p] = alphas[p] * l_ref[p] + pv[LANES:LANES + SUBLANES, :]
            acc_ref[p] = alphas[p][0:1, :] * acc_ref[p] + pv[0:LANES, :]

    def far_body(j, carry):
        attend(j, None)
        return carry

    lax.fori_loop(0, jnp.maximum(i - 1, 0), far_body, 0)

    @pl.when(i >= 1)
    def _():
        attend(i - 1, 0)

    attend(i, tq)

    for p in range(npair):
        o = acc_ref[p] / l_ref[p][0:1, :]
        ot = jnp.concatenate([o[0:HEAD_DIM, 0:tq], o[HEAD_DIM:LANES, tq:2 * tq]], axis=0)
        o_ref[0, :, p * LANES:(p + 1) * LANES] = ot.T


def _attn_prompt(qt, qit, wit, kb, vbt, kdb, rel_bias, bsz, t, tq):
    topk = min(MAX_TOPK, t // 4)
    pos_bits = max(1, (t - 1).bit_length())
    near = _near_bias(rel_bias, tq)
    npair = ATT_HEADS // 2
    nblk = t // tq
    qtile_t = lambda rows: pl.BlockSpec((1, rows, tq), lambda b, i: (b, 0, i))
    full = lambda w: pl.BlockSpec((1, t, w), lambda b, i: (b, 0, 0))
    return pl.pallas_call(
        functools.partial(_attn_prompt_kernel, tq=tq, topk=topk, pos_bits=pos_bits),
        grid=(bsz, nblk),
        in_specs=[pl.BlockSpec(memory_space=pltpu.SMEM),
                  qtile_t(ATT_W), qtile_t(IDX_HEADS * IDX_DIM), qtile_t(SUBLANES),
                  full(ATT_W), pl.BlockSpec((1, nblk, ATT_W, tq), lambda b, i: (b, 0, 0, 0)), full(LANES),
                  _const_spec((npair, 2 * tq, 2 * tq))],
        out_specs=pl.BlockSpec((1, tq, ATT_W), lambda b, i: (b, i, 0)),
        out_shape=jax.ShapeDtypeStruct((bsz, t, ATT_W), F32),
        scratch_shapes=[pltpu.VMEM((t, tq), F32),
                        pltpu.VMEM((npair, LANES, 2 * tq), BF16),
                        pltpu.VMEM((IDX_HEADS // 2, LANES, 2 * tq), BF16),
                        pltpu.VMEM((npair, SUBLANES, 2 * tq), F32),
                        pltpu.VMEM((npair, SUBLANES, 2 * tq), F32),
                        pltpu.VMEM((npair, LANES, 2 * tq), F32)],
        compiler_params=_cparams(("arbitrary", "arbitrary")),
        name="attn_prompt",
    )(rel_bias, qt, qit, wit, kb.reshape(bsz, t, ATT_W), vbt, kdb.reshape(bsz, t, LANES), near)


def _sample_index_kernel(pt_ref, qi_ref, wi_ref, *rest, n_pages, ts):
    page_refs = rest[:n_pages + 1]
    o_ref = rest[n_pages + 1]
    qi = qi_ref[0]
    q16 = jnp.concatenate([qi[:, h * IDX_DIM:(h + 1) * IDX_DIM] for h in range(IDX_HEADS)], axis=0).astype(BF16)
    wi = wi_ref[0]
    scale = IDX_HEADS ** -0.5 * IDX_DIM ** -0.5
    w16 = jnp.concatenate([wi[:, h:h + 1] for h in range(IDX_HEADS)], axis=0) * scale
    trow = lax.broadcasted_iota(I32, (ts, PAGE_SIZE), 0)
    ncol = lax.broadcasted_iota(I32, (ts, PAGE_SIZE), 1)
    for j in range(n_pages + 1):
        kp = page_refs[j][0].astype(BF16)
        if j < n_pages:
            sc = jnp.dot(q16, kp, preferred_element_type=F32)
        else:
            sc = lax.dot_general(q16, kp, (((1,), (1,)), ((), ())), preferred_element_type=F32)
        rw = jnp.maximum(sc, 0.0) * w16
        idx = rw[0:ts]
        for h in range(1, IDX_HEADS):
            idx = idx + rw[h * ts:(h + 1) * ts]
        if j == n_pages:
            idx = jnp.where(ncol <= trow, idx, -jnp.inf)
        o_ref[0, :, j * PAGE_SIZE:(j + 1) * PAGE_SIZE] = idx


def _sample_select_kernel(idx_ref, sel_ref, key_ref, *, topk, pos_bits):
    lp, nq = idx_ref.shape
    idx = idx_ref[...]
    key_ref[...] = jnp.where(idx == 0.0, 0.0, idx)
    theta, cut = _select_params(key_ref, 1, lp, topk, nq, pos_bits)
    kb = key_ref[...]
    pos = lax.broadcasted_iota(I32, (lp, nq), 0)
    sel = ((kb > theta) | ((kb == theta) & (pos <= cut))) & (idx_ref[...] > -jnp.inf)
    sel_ref[...] = sel.astype(F32)


def _sample_bias_kernel(rb_ref, near_ref, new_ref, far_ref, *, ts):
    rows = ts * ATT_HEADS
    r = lax.broadcasted_iota(I32, (rows, PAGE_SIZE), 0)
    c = lax.broadcasted_iota(I32, (rows, PAGE_SIZE), 1)
    t = r // ATT_HEADS
    rhead = r % ATT_HEADS

    def gather(bucket):
        acc = jnp.zeros((rows, PAGE_SIZE), F32)
        for h in range(ATT_HEADS):
            for bk in range(NUM_BUCKETS):
                acc = jnp.where((bucket == bk) & (rhead == h), rb_ref[bk, h], acc)
        return acc

    near_ref[...] = gather(_t5_bucket(PAGE_SIZE + t - c))
    ok = (c <= t) & (c < ts)
    new_ref[...] = jnp.where(ok, gather(_t5_bucket(t - c)), -jnp.inf)
    far_ref[...] = gather(jnp.full((rows, PAGE_SIZE), NUM_BUCKETS - 1, I32))


def _sample_attn_kernel(pt_ref, q_ref, sel_ref, near_ref, new_ref, far_ref, *rest, n_pages, ts):
    k_refs = rest[:n_pages + 1]
    v_refs = rest[n_pages + 1:2 * n_pages + 2]
    o_ref = rest[2 * n_pages + 2]
    l_ref = rest[2 * n_pages + 3]
    rows = ts * ATT_HEADS
    rep = lambda a: jnp.concatenate([jnp.broadcast_to(a[t:t + 1], (ATT_HEADS, a.shape[1])) for t in range(ts)], axis=0)
    r = lax.broadcasted_iota(I32, (rows, ATT_W), 0)
    c = lax.broadcasted_iota(I32, (rows, ATT_W), 1)
    own = (r % ATT_HEADS) == (c // HEAD_DIM)
    q = rep(q_ref[0].astype(F32) * HEAD_DIM ** -0.5)
    q = jnp.where(own, q, 0.0).astype(BF16)
    sel = rep(sel_ref[0]) > 0.5

    for j in range(n_pages):
        kt = k_refs[j][0].reshape(ATT_W, PAGE_SIZE).astype(BF16)
        s = jnp.dot(q, kt, preferred_element_type=F32)
        bias = near_ref[...] if j == n_pages - 1 else far_ref[...]
        sl = slice(j * PAGE_SIZE, (j + 1) * PAGE_SIZE)
        l_ref[:, sl] = jnp.where(sel[:, sl], s + bias, -jnp.inf)
    kn = k_refs[n_pages][0].astype(BF16)
    s = lax.dot_general(q, kn, (((1,), (1,)), ((), ())), preferred_element_type=F32)
    sl = slice(n_pages * PAGE_SIZE, (n_pages + 1) * PAGE_SIZE)
    l_ref[:, sl] = jnp.where(sel[:, sl], s + new_ref[...], -jnp.inf)

    logits = l_ref[...]
    m = jnp.max(logits, axis=1, keepdims=True)
    pr = jnp.exp(logits - m)
    den = jnp.sum(pr, axis=1, keepdims=True)
    prb = pr.astype(BF16)
    acc = jnp.dot(prb[:, sl], v_refs[n_pages][0].astype(BF16), preferred_element_type=F32)
    for j in range(n_pages):
        vt = v_refs[j][0].reshape(ATT_W, PAGE_SIZE).astype(BF16)
        acc = acc + lax.dot_general(prb[:, j * PAGE_SIZE:(j + 1) * PAGE_SIZE], vt, (((1,), (1,)), ((), ())),
                                    preferred_element_type=F32)
    acc = jnp.where(own, acc / den, 0.0)
    o_ref[0] = jnp.sum(acc.reshape(ts, ATT_HEADS, ATT_W), axis=1)


def _attn_sample(qb, qi, wi, k_new, v_new, kd_new, cache_k, cache_v, cache_kidx, page_table, rel_bias, bsz, ts):
    n_pages = page_table.shape[1]
    past_len = n_pages * PAGE_SIZE
    topk = min(MAX_TOPK, (past_len + ts) // 4)
    lp = (n_pages + 1) * PAGE_SIZE
    rows = ts * ATT_HEADS
    pt = page_table.reshape(-1).astype(I32)

    ki_pool = jnp.swapaxes(cache_kidx, 1, 2)
    ki_new = jnp.zeros((bsz, PAGE_SIZE, IDX_DIM), F32).at[:, 0:ts].set(kd_new.reshape(bsz, ts, LANES)[..., 0:IDX_DIM])
    page_spec = lambda j, shp: pl.BlockSpec((1,) + shp, lambda b, ptr: (ptr[b * n_pages + j],) + (0,) * len(shp))
    per_b = lambda shp: pl.BlockSpec((1,) + shp, lambda b, ptr: (b,) + (0,) * len(shp))
    idx = pl.pallas_call(
        functools.partial(_sample_index_kernel, n_pages=n_pages, ts=ts),
        grid_spec=pltpu.PrefetchScalarGridSpec(
            num_scalar_prefetch=1, grid=(bsz,),
            in_specs=[per_b((ts, IDX_HEADS * IDX_DIM)), per_b((ts, LANES))]
                     + [page_spec(j, (IDX_DIM, PAGE_SIZE)) for j in range(n_pages)]
                     + [per_b((PAGE_SIZE, IDX_DIM))],
            out_specs=per_b((ts, lp))),
        out_shape=jax.ShapeDtypeStruct((bsz, ts, lp), F32),
        compiler_params=_cparams(("arbitrary",)),
        name="sample_index",
    )(pt, qi.reshape(bsz, ts, -1), wi.reshape(bsz, ts, LANES), *([ki_pool] * n_pages), ki_new)

    sel = pl.pallas_call(
        functools.partial(_sample_select_kernel, topk=topk, pos_bits=max(1, (lp - 1).bit_length())),
        out_shape=jax.ShapeDtypeStruct((lp, bsz * ts), F32),
        scratch_shapes=[pltpu.VMEM((lp, bsz * ts), F32)],
        compiler_params=pltpu.CompilerParams(vmem_limit_bytes=VMEM_LIMIT),
        name="sample_select",
    )(idx.reshape(bsz * ts, lp).T)
    sel = sel.T.reshape(bsz, ts, lp)

    near, newb, far = pl.pallas_call(
        functools.partial(_sample_bias_kernel, ts=ts),
        in_specs=[pl.BlockSpec(memory_space=pltpu.SMEM)],
        out_shape=[jax.ShapeDtypeStruct((rows, PAGE_SIZE), F32)] * 3,
        compiler_params=pltpu.CompilerParams(vmem_limit_bytes=VMEM_LIMIT),
        name="sample_bias",
    )(rel_bias)

    page4 = (ATT_HEADS, HEAD_DIM, PAGE_SIZE)
    k_pool = jnp.transpose(cache_k, (0, 2, 3, 1))
    v_pool = jnp.transpose(cache_v, (0, 2, 3, 1))
    pad_new = lambda a: jnp.zeros((bsz, PAGE_SIZE, ATT_W), F32).at[:, 0:ts].set(a.reshape(bsz, ts, ATT_W))
    const = lambda shp: pl.BlockSpec(shp, lambda b, ptr: (0,) * len(shp))
    out = pl.pallas_call(
        functools.partial(_sample_attn_kernel, n_pages=n_pages, ts=ts),
        grid_spec=pltpu.PrefetchScalarGridSpec(
            num_scalar_prefetch=1, grid=(bsz,),
            in_specs=[per_b((ts, ATT_W)), per_b((ts, lp)),
                      const((rows, PAGE_SIZE)), const((rows, PAGE_SIZE)), const((rows, PAGE_SIZE))]
                     + [page_spec(j, page4) for j in range(n_pages)] + [per_b((PAGE_SIZE, ATT_W))]
                     + [page_spec(j, page4) for j in range(n_pages)] + [per_b((PAGE_SIZE, ATT_W))],
            out_specs=per_b((ts, ATT_W)),
            scratch_shapes=[pltpu.VMEM((rows, lp), F32)]),
        out_shape=jax.ShapeDtypeStruct((bsz, ts, ATT_W), F32),
        compiler_params=_cparams(("arbitrary",)),
        name="sample_attn",
    )(pt, qb.reshape(bsz, ts, ATT_W), sel, near, newb, far,
      *([k_pool] * n_pages), pad_new(k_new), *([v_pool] * n_pages), pad_new(v_new))
    return out.reshape(bsz * ts, ATT_W)


def _layer_norm(x, g, b):
    mu = jnp.mean(x, axis=-1, keepdims=True)
    xc = x - mu
    var = jnp.mean(xc * xc, axis=-1, keepdims=True)
    return xc * lax.rsqrt(var + LN_EPS) * g + b


def _ffn_kernel(x_ref, rw_ref, att_ref, prev_ref, wo_ref, g1_ref, b1_ref, wup_ref, cw_ref, cb_ref, wdn_ref,
                g2_ref, b2_ref, y_ref, nc_ref, pad_ref, *, tm, shift, off, alpha):
    ti = pl.program_id(1)
    data = off + 2 * shift

    @pl.when(ti == 0)
    def _():
        pad_ref[off:data, :] = prev_ref[0]

    x = x_ref[0]
    mix = (jnp.dot(rw_ref[0].astype(BF16), wo_ref[0:RW_W, :], preferred_element_type=F32)
           + jnp.dot(att_ref[0].astype(BF16), wo_ref[RW_W:RW_W + ATT_W, :], preferred_element_type=F32))
    x1 = _layer_norm(alpha * x + mix, g1_ref[...], b1_ref[...])
    x1b = x1.astype(BF16)
    u_conv = jnp.dot(x1b, wup_ref[:, 0:D_FF], preferred_element_type=F32)
    u_lin = jnp.dot(x1b, wup_ref[:, D_FF:2 * D_FF], preferred_element_type=F32)
    pad_ref[data:data + tm, :] = u_conv
    conv = (pad_ref[off:off + tm, :] * cw_ref[0:1, :] + pad_ref[off + shift:off + shift + tm, :] * cw_ref[1:2, :]
            + u_conv * cw_ref[2:3, :] + cb_ref[...])
    h = conv * _sigmoid(conv) * u_lin
    down = jnp.dot(h.astype(BF16), wdn_ref[...], preferred_element_type=F32)
    y_ref[0] = _layer_norm(alpha * x1 + down, g2_ref[...], b2_ref[...])
    tail = pad_ref[off + tm:data + tm, :]
    pad_ref[off:data, :] = tail
    nc_ref[0] = tail


def _ffn(x, rw, att, prev, lw, alpha, shift):
    nb, rows, _ = x.shape
    tm = min(256, rows) if shift == 1 else rows
    off = 6 if shift == 1 else 0
    row = lambda a: a.reshape(1, -1)
    tile = lambda w: pl.BlockSpec((1, tm, w), lambda b, i: (b, i, 0))
    y, nc = pl.pallas_call(
        functools.partial(_ffn_kernel, tm=tm, shift=shift, off=off, alpha=alpha),
        grid=(nb, rows // tm),
        in_specs=[tile(D_MODEL), tile(RW_W), tile(ATT_W),
                  pl.BlockSpec((1, 2 * shift, D_FF), lambda b, i: (b, 0, 0)),
                  _const_spec((D_MODEL, D_MODEL)), _const_spec((1, D_MODEL)), _const_spec((1, D_MODEL)),
                  _const_spec((D_MODEL, 2 * D_FF)), _const_spec((CONV_W, D_FF)), _const_spec((1, D_FF)),
                  _const_spec((D_FF, D_MODEL)), _const_spec((1, D_MODEL)), _const_spec((1, D_MODEL))],
        out_specs=[tile(D_MODEL), pl.BlockSpec((1, 2 * shift, D_FF), lambda b, i: (b, 0, 0))],
        out_shape=[jax.ShapeDtypeStruct((nb, rows, D_MODEL), F32), jax.ShapeDtypeStruct((nb, 2 * shift, D_FF), F32)],
        scratch_shapes=[pltpu.VMEM((off + 2 * shift + tm, D_FF), F32)],
        compiler_params=_cparams(("arbitrary", "arbitrary")),
        name="ffn",
    )(x, rw, att, prev, lw["w_out"].astype(BF16), row(lw["ln1_g"]), row(lw["ln1_b"]),
      lw["ffn_w_up"].astype(BF16), lw["ffn_conv_w"], row(lw["ffn_conv_b"]), lw["ffn_w_down"].astype(BF16),
      row(lw["ln2_g"]), row(lw["ln2_b"]))
    return y, nc


def _layer(x, shift_prev, wkv_prev, conv_prev, past, rel_bias, lw, alpha):
    bsz, t, _ = x.shape
    w_in = lw["w_in"]
    w_ki = w_in[:, C_KI:C_KI + IDX_DIM]
    w_wi = w_in[:, C_KI + IDX_DIM:]
    wp = jnp.concatenate([w_in[:, 0:C_KI], w_ki, w_ki, w_wi,
                          jnp.zeros((D_MODEL, LANES - IDX_HEADS), F32)], axis=1).astype(BF16)
    x2d = x.reshape(bsz * t, D_MODEL)
    if past is None:
        tq = min(256, t)
        prw, qt, kb, kt, vt, vbt, qit, kdb, kit, wit = _proj_t(x2d, wp, bsz, t, tq)
        rw_out, new_wkv, new_shift = _rwkv(prw.reshape(bsz, t, SHIFT_COLS), shift_prev, wkv_prev, lw)
        att = _attn_prompt(qt, qit, wit, kb, vbt, kdb, rel_bias, bsz, t, tq)
        y, new_conv = _ffn(x, rw_out, att, conv_prev, lw, alpha, shift=1)
        heads = lambda a: jnp.transpose(a.reshape(bsz, ATT_HEADS, HEAD_DIM, t), (0, 3, 1, 2))
        return y, (heads(kt), heads(vt), jnp.swapaxes(kit, 1, 2), new_wkv, new_shift, new_conv)
    prw, qb, k, v, qi, kd, wi = _proj(x2d, wp)
    rw_out, new_wkv, new_shift = _rwkv(prw.reshape(bsz, t, SHIFT_COLS), shift_prev, wkv_prev, lw)
    att = _attn_sample(qb, qi, wi, k, v, kd, *past, rel_bias, bsz, t)
    tmaj = lambda a: jnp.swapaxes(a.reshape(bsz, t, -1), 0, 1).reshape(1, t * bsz, -1)
    prev = jnp.swapaxes(conv_prev, 0, 1).reshape(1, (CONV_W - 1) * bsz, D_FF)
    y, new_conv = _ffn(tmaj(x), tmaj(rw_out), tmaj(att), prev, lw, alpha, shift=bsz)
    y = jnp.swapaxes(y.reshape(t, bsz, D_MODEL), 0, 1)
    new_conv = jnp.swapaxes(new_conv.reshape(CONV_W - 1, bsz, D_FF), 0, 1)
    state = (k.reshape(bsz, t, ATT_HEADS, HEAD_DIM), v.reshape(bsz, t, ATT_HEADS, HEAD_DIM),
             kd.reshape(bsz, t, LANES)[..., 0:IDX_DIM], new_wkv, new_shift, new_conv)
    return y, state


_LAYER_WEIGHTS = ("w_in", "shift_mu", "rw_w0", "rw_w_up", "rw_a0", "rw_a_up", "rw_g_up", "rw_k_k", "rw_k_a",
                  "rw_r_k", "rw_gn_g", "rw_gn_b", "w_out", "ln1_g", "ln1_b", "ffn_w_up", "ffn_conv_w",
                  "ffn_conv_b", "ffn_w_down", "ln2_g", "ln2_b")


def kernel(x_prompt, x_sample, cache_k, cache_v, cache_kidx, page_table, state_wkv, state_shift, state_conv,
           rel_bias, w_in, shift_mu, rw_w0, rw_w_up, rw_a0, rw_a_up, rw_g_up, rw_k_k, rw_k_a, rw_r_k, rw_gn_g,
           rw_gn_b, w_out, ln1_g, ln1_b, ffn_w_up, ffn_conv_w, ffn_conv_b, ffn_w_down, ln2_g, ln2_b):
    stacked = dict(zip(_LAYER_WEIGHTS, (w_in, shift_mu, rw_w0, rw_w_up, rw_a0, rw_a_up, rw_g_up, rw_k_k, rw_k_a,
                                        rw_r_k, rw_gn_g, rw_gn_b, w_out, ln1_g, ln1_b, ffn_w_up, ffn_conv_w,
                                        ffn_conv_b, ffn_w_down, ln2_g, ln2_b)))
    depth = w_in.shape[0]
    alpha = (2 * depth) ** 0.25
    bp = x_prompt.shape[0]
    yp, ys = x_prompt, x_sample
    outs_p, outs_s = [], []
    for l in range(depth):
        lw = {name: w[l] for name, w in stacked.items()}
        yp, st_p = _layer(yp, jnp.zeros((bp, 1, SHIFT_COLS), F32),
                          jnp.zeros((bp, RW_HEADS, HEAD_DIM, HEAD_DIM), F32),
                          jnp.zeros((bp, CONV_W - 1, D_FF), F32), None, rel_bias, lw, alpha)
        ys, st_s = _layer(ys, state_shift[l], state_wkv[l], state_conv[l],
                          (cache_k[l], cache_v[l], cache_kidx[l], page_table), rel_bias, lw, alpha)
        outs_p.append(st_p)
        outs_s.append(st_s)
    st = lambda outs, i: jnp.stack([o[i] for o in outs])
    return (yp, ys) + tuple(st(outs_p, i) for i in range(6)) + tuple(st(outs_s, i) for i in range(6))
```

```python
import functools
import math

import jax
import jax.numpy as jnp
from jax import lax
from jax.experimental import pallas as pl
from jax.experimental.pallas import tpu as pltpu

F32 = jnp.float32
BF16 = jnp.bfloat16
I32 = jnp.int32
HIGHEST = lax.Precision.HIGHEST

D_MODEL = 1024
HEAD_DIM = 64
RW_W = 512
ATT_W = 512
RW_HEADS = 8
ATT_HEADS = 8
IDX_HEADS = 4
IDX_DIM = 64
MAX_TOPK = 256
W_LORA = 64
A_LORA = 64
G_LORA = 128
GN_EPS = 64e-5
LN_EPS = 1e-5
D_FF = 2816
CONV_W = 3
NUM_BUCKETS = 32
MAX_DISTANCE = 128
PAGE_SIZE = 128
SHIFT_COLS = 3 * RW_W + W_LORA + A_LORA + G_LORA
LANES = 128
SUBLANES = 8
C_Q = SHIFT_COLS
C_K = C_Q + ATT_W
C_V = C_K + ATT_W
C_QI = C_V + ATT_W
C_KI = C_QI + IDX_HEADS * IDX_DIM
C_WI = C_KI + LANES
C_END = C_WI + LANES
VMEM_LIMIT = 56 * 1024 * 1024
INT_MIN = -(2 ** 31)
ORD_NEG_INF = INT_MIN + 0x7FFFFF
NEG_BIG = -1e30


def _cparams(sem):
    return pltpu.CompilerParams(dimension_semantics=sem, vmem_limit_bytes=VMEM_LIMIT)


def _const_spec(shape):
    zeros = (0,) * len(shape)
    return pl.BlockSpec(shape, lambda *_: zeros, pipeline_mode=pl.Buffered(1))


def _proj_kernel(x_ref, w_ref, prw_ref, qb_ref, k_ref, v_ref, qi_ref, kd_ref, wi_ref):
    x = x_ref[...].astype(BF16)

    def mm(lo, hi):
        return jnp.dot(x, w_ref[:, lo:hi], preferred_element_type=F32)

    prw_ref[...] = mm(0, C_Q)
    qb_ref[...] = mm(C_Q, C_K).astype(BF16)
    k_ref[...] = mm(C_K, C_V)
    v_ref[...] = mm(C_V, C_QI)
    qi_ref[...] = mm(C_QI, C_KI)
    kd_ref[...] = mm(C_KI, C_WI)
    wi_ref[...] = mm(C_WI, C_END)


def _proj(x2d, wp):
    m = x2d.shape[0]
    tm = min(256, m)
    widths = [(C_Q, F32), (ATT_W, BF16), (ATT_W, F32), (ATT_W, F32), (IDX_HEADS * IDX_DIM, F32), (LANES, F32),
              (LANES, F32)]
    return pl.pallas_call(
        _proj_kernel,
        grid=(m // tm,),
        in_specs=[pl.BlockSpec((tm, D_MODEL), lambda i: (i, 0)), _const_spec((D_MODEL, C_END))],
        out_specs=[pl.BlockSpec((tm, w), lambda i: (i, 0)) for w, _ in widths],
        out_shape=[jax.ShapeDtypeStruct((m, w), dt) for w, dt in widths],
        compiler_params=_cparams(("arbitrary",)),
        name="proj",
    )(x2d, wp)


def _proj_t_kernel(x_ref, w_ref, wt_ref, prw_ref, qt_ref, kb_ref, kt_ref, vt_ref, vbt_ref, qit_ref, kdb_ref, kit_ref,
                   wit_ref):
    x = x_ref[...].astype(BF16)

    def mm(lo, hi):
        return jnp.dot(x, w_ref[:, lo:hi], preferred_element_type=F32)

    def mt(lo, hi):
        return lax.dot_general(wt_ref[lo:hi, :], x, (((1,), (1,)), ((), ())), preferred_element_type=F32)

    prw_ref[...] = mm(0, C_Q)
    kb_ref[...] = mm(C_K, C_V).astype(BF16)
    kdb_ref[...] = mm(C_KI, C_WI).astype(BF16)
    qt_ref[0] = (mt(C_Q, C_K) * HEAD_DIM ** -0.5).astype(BF16)
    kt_ref[0] = mt(C_K, C_V)
    vt = mt(C_V, C_QI)
    vt_ref[0] = vt
    vbt_ref[0, 0] = vt.astype(BF16)
    qit_ref[0] = mt(C_QI, C_KI).astype(BF16)
    kit_ref[0] = mt(C_KI, C_KI + IDX_DIM)
    wit_ref[0] = mt(C_WI, C_WI + SUBLANES)


def _proj_t(x2d, wp, bsz, t, tm):
    m = x2d.shape[0]
    nblk = t // tm
    rows = lambda w: pl.BlockSpec((tm, w), lambda b, i: (b * nblk + i, 0))
    cols = lambda r: pl.BlockSpec((1, r, tm), lambda b, i: (b, 0, i))
    tshape = lambda r, dt: jax.ShapeDtypeStruct((bsz, r, t), dt)
    return pl.pallas_call(
        _proj_t_kernel,
        grid=(bsz, nblk),
        in_specs=[rows(D_MODEL), _const_spec((D_MODEL, C_END)), _const_spec((C_END, D_MODEL))],
        out_specs=[rows(C_Q), cols(ATT_W), rows(ATT_W), cols(ATT_W), cols(ATT_W),
                   pl.BlockSpec((1, 1, ATT_W, tm), lambda b, i: (b, i, 0, 0)),
                   cols(IDX_HEADS * IDX_DIM), rows(LANES), cols(IDX_DIM), cols(SUBLANES)],
        out_shape=[jax.ShapeDtypeStruct((m, C_Q), F32), tshape(ATT_W, BF16), jax.ShapeDtypeStruct((m, ATT_W), BF16),
                   tshape(ATT_W, F32), tshape(ATT_W, F32), jax.ShapeDtypeStruct((bsz, nblk, ATT_W, tm), BF16),
                   tshape(IDX_HEADS * IDX_DIM, BF16), jax.ShapeDtypeStruct((m, LANES), BF16),
                   tshape(IDX_DIM, F32), tshape(SUBLANES, F32)],
        compiler_params=_cparams(("arbitrary", "arbitrary")),
        name="proj_t",
    )(x2d, wp, wp.T)


def _softplus(z):
    return jnp.maximum(z, 0.0) + jnp.log1p(jnp.exp(-jnp.abs(z)))


def _sigmoid(z):
    return 1.0 / (1.0 + jnp.exp(-z))


def _seg_sum(x, ones_bd):
    parts = [jnp.dot(x[:, g * LANES:(g + 1) * LANES], ones_bd, precision=HIGHEST, preferred_element_type=F32)
             for g in range(x.shape[1] // LANES)]
    return jnp.concatenate(parts, axis=1)


def _rwkv_kernel(p_ref, sp_ref, s0_ref, mu_ref, w0_ref, lw_ref, a0_ref, gup_ref, kk_ref, ka_ref, rk_ref,
                 gng_ref, gnb_ref, ones2_ref, diag_ref,
                 out_ref, sfin_ref, shift_ref,
                 pad_ref, carry_ref, s_ref, kkv_ref, dec_ref, b_ref, k2_ref, r_ref, v_ref, g_ref, bon_ref, y_ref,
                 *, nb, tc_len):
    tc = pl.program_id(1)
    last = pl.num_programs(1) - 1
    npair = RW_HEADS // 2

    @pl.when(tc == 0)
    def _():
        carry_ref[...] = sp_ref[...]
        for b in range(nb):
            for hp in range(npair):
                s_ref[b, hp] = jnp.concatenate([s0_ref[b, 2 * hp], s0_ref[b, 2 * hp + 1]], axis=1)

    rows = lax.broadcasted_iota(I32, (LANES, LANES), 0)
    cols = lax.broadcasted_iota(I32, (LANES, LANES), 1)
    ones_bd = (rows // HEAD_DIM == cols // HEAD_DIM).astype(F32)
    lane = lax.broadcasted_iota(I32, (tc_len, LANES), 1)

    for b in range(nb):
        x = p_ref[b]
        pad_ref[b, 7:8, :] = carry_ref[b]
        pad_ref[b, 8:8 + tc_len, :] = x
        xm1 = pad_ref[b, 7:7 + tc_len, :]
        ps = x + (xm1 - x) * mu_ref[...]
        carry_ref[b] = x[tc_len - 1:tc_len, :]
        r = ps[:, 0:RW_W]
        k = ps[:, RW_W:2 * RW_W]
        v = ps[:, 2 * RW_W:3 * RW_W]
        o = 3 * RW_W
        z = ps[:, o:o + LANES]
        z = jnp.where(lane < W_LORA, jnp.tanh(z), z)
        lo = jnp.dot(z, lw_ref[...], precision=HIGHEST, preferred_element_type=F32)
        w_log = -_softplus(-(w0_ref[...] + lo[:, 0:RW_W])) - 0.5
        dec = jnp.exp(-jnp.exp(w_log))
        a = _sigmoid(a0_ref[...] + lo[:, RW_W:2 * RW_W])
        gd = ps[:, o + LANES:o + 2 * LANES]
        g = jnp.dot(_sigmoid(gd), gup_ref[...], precision=HIGHEST, preferred_element_type=F32)
        kkx = k * kk_ref[...]
        ss = _seg_sum(kkx * kkx, ones_bd)
        kkn = kkx * lax.rsqrt(jnp.maximum(ss, 1e-24))
        k2 = k * (1.0 + (a - 1.0) * ka_ref[...])
        bonus = _seg_sum(r * k2 * rk_ref[...], ones_bd) * v
        kkv_ref[b, 0:tc_len] = kkn
        dec_ref[b, 0:tc_len] = dec
        b_ref[b, 0:tc_len] = kkn * a
        k2_ref[b, 0:tc_len] = k2
        r_ref[b, 0:tc_len] = r
        v_ref[b, 0:tc_len] = v
        g_ref[b, 0:tc_len] = g
        bon_ref[b, 0:tc_len] = bonus

    diag = diag_ref[...]
    diag_b = diag.astype(BF16)
    drow = lax.broadcasted_iota(I32, (HEAD_DIM, LANES), 0)
    dlane = lax.broadcasted_iota(I32, (HEAD_DIM, LANES), 1)
    diag2_b = ((drow + 1) % HEAD_DIM == dlane % HEAD_DIM).astype(BF16)
    seg_start = lax.broadcasted_iota(I32, (SUBLANES, LANES), 1) % HEAD_DIM == 0

    sub = min(SUBLANES, tc_len)
    riota = lax.broadcasted_iota(I32, (SUBLANES, LANES), 0)

    tiles = [(b, hp) for b in range(nb) for hp in range(npair)]
    nt = len(tiles)

    def bsum(xs):
        lhs = jnp.concatenate([jnp.concatenate([xs[2 * m], xs[2 * m + 1]], axis=1) for m in range(nt // 2)], axis=0)
        red = jnp.dot(lhs, ones2_ref[...], preferred_element_type=F32)
        return [red[(k // 2) * HEAD_DIM:(k // 2 + 1) * HEAD_DIM, (k % 2) * LANES:(k % 2 + 1) * LANES]
                for k in range(nt)]

    def group(gi, carry):
        t0 = pl.multiple_of(gi * SUBLANES, SUBLANES)
        rows8 = [[ref[b, pl.ds(t0, SUBLANES), hp * LANES:(hp + 1) * LANES]
                  for ref in (kkv_ref, dec_ref, b_ref, k2_ref, r_ref, v_ref)] for b, hp in tiles]
        vhi8 = [rows8[k][5].astype(BF16).astype(F32) for k in range(nt)]
        vmid8 = [rows8[k][5] - vhi8[k] for k in range(nt)]
        vmid8 = [jnp.where(seg_start, pltpu.roll(x, LANES - HEAD_DIM + 1, 1), pltpu.roll(x, 1, 1)) for x in vmid8]
        s = [s_ref[b, hp] for b, hp in tiles]
        y8 = [jnp.zeros((SUBLANES, LANES), F32) for _ in tiles]
        tile_of = lambda row: jnp.broadcast_to(row, (HEAD_DIM, LANES)).astype(BF16)
        for i in range(sub):
            kkr, wr, br, kr, rr = [[rows8[k][q][i:i + 1, :] for k in range(nt)] for q in range(5)]
            sa = bsum([(s[k] * kkr[k]).astype(BF16) for k in range(nt)])
            vb = bsum([tile_of(vhi8[k][i:i + 1, :]) * diag_b + tile_of(vmid8[k][i:i + 1, :]) * diag2_b
                       for k in range(nt)])
            for k in range(nt):
                s[k] = s[k] * wr[k] - sa[k] * br[k] + vb[k] * kr[k]
            yb = bsum([(s[k] * rr[k]).astype(BF16) for k in range(nt)])
            for k in range(nt):
                yrow = jnp.sum(yb[k] * diag, axis=0, keepdims=True)
                y8[k] = jnp.where(riota == i, yrow, y8[k])
        for k, (b, hp) in enumerate(tiles):
            s_ref[b, hp] = s[k]
            y_ref[b, pl.ds(t0, SUBLANES), hp * LANES:(hp + 1) * LANES] = y8[k]
        return carry

    lax.fori_loop(0, max(1, tc_len // SUBLANES), group, 0)

    for b in range(nb):
        y = y_ref[b, 0:tc_len]
        mu = _seg_sum(y, ones_bd) * (1.0 / HEAD_DIM)
        yc = y - mu
        var = _seg_sum(yc * yc, ones_bd) * (1.0 / HEAD_DIM)
        yn = yc * lax.rsqrt(var + GN_EPS) * gng_ref[...] + gnb_ref[...]
        out_ref[b] = (yn + bon_ref[b, 0:tc_len]) * g_ref[b, 0:tc_len]

    @pl.when(tc == last)
    def _():
        shift_ref[...] = carry_ref[...]
        for b in range(nb):
            for hp in range(npair):
                s = s_ref[b, hp]
                sfin_ref[b, 2 * hp] = s[:, 0:HEAD_DIM]
                sfin_ref[b, 2 * hp + 1] = s[:, HEAD_DIM:2 * HEAD_DIM]


def _rwkv(p, shift_prev, wkv_prev, lw):
    bsz, t, _ = p.shape
    nb = 4
    tc_len = min(128, t)
    row = lambda a: a.reshape(1, -1)
    w_up, a_up = lw["rw_w_up"], lw["rw_a_up"]
    lora = jnp.zeros((LANES, 2 * RW_W), F32)
    lora = lora.at[0:W_LORA, 0:RW_W].set(w_up).at[W_LORA:LANES, RW_W:2 * RW_W].set(a_up)
    rr = jnp.arange(2 * LANES)[:, None]
    cc = jnp.arange(2 * LANES)[None, :]
    ones2 = (rr // HEAD_DIM == cc // HEAD_DIM).astype(BF16)
    diag = (jnp.arange(HEAD_DIM)[:, None] == cc[:, 0:LANES] % HEAD_DIM).astype(F32)
    blk3 = lambda w: pl.BlockSpec((nb, tc_len, w), lambda i, j: (i, j, 0))
    vec = lambda w: _const_spec((1, w))
    assert tc_len < SUBLANES or tc_len % SUBLANES == 0
    scr = lambda w: pltpu.VMEM((nb, max(SUBLANES, tc_len), w), F32)
    out, sfin, shift = pl.pallas_call(
        functools.partial(_rwkv_kernel, nb=nb, tc_len=tc_len),
        grid=(bsz // nb, t // tc_len),
        in_specs=[blk3(SHIFT_COLS),
                  pl.BlockSpec((nb, 1, SHIFT_COLS), lambda i, j: (i, 0, 0)),
                  pl.BlockSpec((nb, RW_HEADS, HEAD_DIM, HEAD_DIM), lambda i, j: (i, 0, 0, 0)),
                  vec(SHIFT_COLS), vec(RW_W), _const_spec((LANES, 2 * RW_W)), vec(RW_W),
                  _const_spec((G_LORA, RW_W)), vec(RW_W), vec(RW_W), vec(RW_W), vec(RW_W), vec(RW_W),
                  _const_spec((2 * LANES, 2 * LANES)), _const_spec((HEAD_DIM, LANES))],
        out_specs=[blk3(RW_W),
                   pl.BlockSpec((nb, RW_HEADS, HEAD_DIM, HEAD_DIM), lambda i, j: (i, 0, 0, 0)),
                   pl.BlockSpec((nb, 1, SHIFT_COLS), lambda i, j: (i, 0, 0))],
        out_shape=[jax.ShapeDtypeStruct((bsz, t, RW_W), F32),
                   jax.ShapeDtypeStruct((bsz, RW_HEADS, HEAD_DIM, HEAD_DIM), F32),
                   jax.ShapeDtypeStruct((bsz, 1, SHIFT_COLS), F32)],
        scratch_shapes=[pltpu.VMEM((nb, tc_len + 8, SHIFT_COLS), F32),
                        pltpu.VMEM((nb, 1, SHIFT_COLS), F32),
                        pltpu.VMEM((nb, RW_HEADS // 2, HEAD_DIM, LANES), F32)] + [scr(RW_W)] * 9,
        compiler_params=_cparams(("arbitrary", "arbitrary")),
        name="rwkv",
    )(p, shift_prev, wkv_prev, row(lw["shift_mu"]), row(lw["rw_w0"]), lora, row(lw["rw_a0"]), lw["rw_g_up"],
      row(lw["rw_k_k"]), row(lw["rw_k_a"]), row(lw["rw_r_k"]), row(lw["rw_gn_g"]), row(lw["rw_gn_b"]), ones2, diag)
    return out, sfin, shift


def _ordinal_to_score(o):
    o = jnp.maximum(o, ORD_NEG_INF)
    bits = jnp.where(o >= 0, o, o ^ jnp.int32(0x7FFFFFFF))
    return lax.bitcast_convert_type(bits, F32)


def _select_params(score_ref, nblk, blk, topk, nq, pos_bits):
    kf = jnp.float32(topk)

    def count(pred):
        def one(j, acc):
            off = pl.multiple_of(j * blk, blk)
            kb = score_ref[pl.ds(off, blk), :]
            pos = off + lax.broadcasted_iota(I32, (blk, nq), 0)
            hit = pred(kb, pos).astype(F32)
            return acc + jnp.sum(hit.reshape(blk // SUBLANES, SUBLANES, nq), axis=0)

        acc = lax.fori_loop(0, nblk // 2, lambda jj, acc: one(2 * jj + 1, one(2 * jj, acc)),
                            jnp.zeros((SUBLANES, nq), F32))
        acc = lax.cond(nblk % 2 == 1, lambda acc: one(nblk - 1, acc), lambda acc: acc, acc)
        return jnp.sum(acc, axis=0, keepdims=True)

    zero = jnp.zeros((1, nq), I32)
    c0 = count(lambda kb, pos: kb >= 0.0)
    ordinal = jnp.where(c0 >= kf, zero, jnp.full((1, nq), INT_MIN, I32))
    cnt = jnp.where(c0 >= kf, c0, jnp.asarray(nblk * blk, F32))

    def more(st):
        bit, _, cnt = st
        return (bit >= 0) & (jnp.max(jnp.abs(cnt - kf)) > 0.0)

    def bit_body(st):
        bit, ordinal, cnt = st
        cand = ordinal | jnp.left_shift(jnp.int32(1), bit)
        cand_score = _ordinal_to_score(cand)
        c = count(lambda kb, pos: kb >= cand_score)
        ok = c >= kf
        return bit - 1, jnp.where(ok, cand, ordinal), jnp.where(ok, c, cnt)

    _, ordinal, cnt = lax.while_loop(more, bit_body, (jnp.int32(30), ordinal, cnt))
    theta = _ordinal_to_score(ordinal)
    big = jnp.full((1, nq), 2 ** 30, I32)

    def with_ties():
        n_gt = count(lambda kb, pos: kb > theta)
        n_eq = count(lambda kb, pos: kb == theta)
        need = kf - n_gt

        def tie_cut():
            def pbody(i, cut):
                cand = cut | jnp.left_shift(jnp.int32(1), jnp.int32(pos_bits - 1) - i)
                c = count(lambda kb, pos: (kb == theta) & (pos < cand))
                return jnp.where(c < need, cand, cut)
            return lax.fori_loop(0, pos_bits, pbody, jnp.zeros((1, nq), I32))

        return lax.cond(jnp.max(n_eq - need) > 0.0, tie_cut, lambda: big)

    cut = lax.cond(jnp.max(jnp.abs(cnt - kf)) > 0.0, with_ties, lambda: big)
    return theta, cut


def _t5_bucket(dist):
    n = jnp.maximum(dist, 0)
    max_exact = NUM_BUCKETS // 2
    nf = jnp.maximum(n, 1).astype(F32)
    large = max_exact + jnp.floor(jnp.log(nf / max_exact) / math.log(MAX_DISTANCE / max_exact)
                                  * (NUM_BUCKETS - max_exact)).astype(I32)
    large = jnp.minimum(large, NUM_BUCKETS - 1)
    return jnp.where(n < max_exact, n, large)


def _near_bias_kernel(rb_ref, o_ref, *, tq):
    c = lax.broadcasted_iota(I32, (2 * tq, 2 * tq), 0)
    l = lax.broadcasted_iota(I32, (2 * tq, 2 * tq), 1)
    first = l < tq
    bucket = _t5_bucket(jnp.where(first, l, l - tq) - c + tq)
    for p in range(ATT_HEADS // 2):
        acc = jnp.zeros((2 * tq, 2 * tq), F32)
        for bk in range(NUM_BUCKETS):
            val = jnp.where(first, rb_ref[bk, 2 * p], rb_ref[bk, 2 * p + 1])
            acc = jnp.where(bucket == bk, val, acc)
        o_ref[p] = acc


def _near_bias(rel_bias, tq):
    return pl.pallas_call(
        functools.partial(_near_bias_kernel, tq=tq),
        in_specs=[pl.BlockSpec(memory_space=pltpu.SMEM)],
        out_specs=pl.BlockSpec(memory_space=pltpu.VMEM),
        out_shape=jax.ShapeDtypeStruct((ATT_HEADS // 2, 2 * tq, 2 * tq), F32),
        compiler_params=pltpu.CompilerParams(vmem_limit_bytes=VMEM_LIMIT),
        name="near_bias",
    )(rel_bias)


def _attn_prompt_kernel(rb_ref, qt_ref, qit_ref, wit_ref, kb_ref, vbt_ref, kdb_ref, near_ref, o_ref,
                        key_ref, qbd_ref, qibd_ref, m_ref, l_ref, acc_ref, *, tq, topk, pos_bits):
    i = pl.program_id(1)
    npair = ATT_HEADS // 2
    upper = lax.broadcasted_iota(I32, (LANES, tq), 0) < HEAD_DIM

    def stack_pair(x):
        zero = jnp.zeros_like(x)
        return jnp.concatenate([jnp.where(upper, x, zero), jnp.where(upper, zero, x)], axis=1)

    for p in range(npair):
        qbd_ref[p] = stack_pair(qt_ref[0, p * LANES:(p + 1) * LANES, :])
    for p in range(IDX_HEADS // 2):
        qibd_ref[p] = stack_pair(qit_ref[0, p * LANES:(p + 1) * LANES, :])
    wrow = wit_ref[0] * (IDX_HEADS ** -0.5 * IDX_DIM ** -0.5)

    kpos = lax.broadcasted_iota(I32, (tq, tq), 0)
    qpos = i * tq + lax.broadcasted_iota(I32, (tq, tq), 1)

    def idx_body(j, carry):
        off = pl.multiple_of(j * tq, tq)
        kd = kdb_ref[0, pl.ds(off, tq), :]
        idx = jnp.zeros((tq, tq), F32)
        for p in range(IDX_HEADS // 2):
            sc = jnp.maximum(jnp.dot(kd, qibd_ref[p], preferred_element_type=F32), 0.0)
            idx = idx + sc[:, 0:tq] * wrow[2 * p:2 * p + 1, :] + sc[:, tq:2 * tq] * wrow[2 * p + 1:2 * p + 2, :]
        idx = jnp.where(idx == 0.0, 0.0, idx)
        key_ref[pl.ds(off, tq), :] = jnp.where(off + kpos <= qpos, idx, -jnp.inf)
        return carry

    lax.fori_loop(0, i + 1, idx_body, 0)

    theta, cut = _select_params(key_ref, i + 1, tq, topk, tq, pos_bits)

    m_ref[...] = jnp.full(m_ref.shape, NEG_BIG, F32)
    l_ref[...] = jnp.zeros(l_ref.shape, F32)
    acc_ref[...] = jnp.zeros(acc_ref.shape, F32)
    first = lax.broadcasted_iota(I32, (1, 2 * tq), 1) < tq
    ones_rows = jnp.ones((2 * SUBLANES, tq), BF16)

    def attend(j, near_rows):
        off = pl.multiple_of(j * tq, tq)
        kb = key_ref[pl.ds(off, tq), :]
        pos = off + kpos
        sel = ((kb > theta) | ((kb == theta) & (pos <= cut))) & (pos <= qpos)
        mb = jnp.where(sel, 0.0, -jnp.inf)
        mb = jnp.concatenate([mb, mb], axis=1)
        ss = [jnp.dot(kb_ref[0, pl.ds(off, tq), p * LANES:(p + 1) * LANES], qbd_ref[p], preferred_element_type=F32)
              for p in range(npair)]
        prs, alphas = [], []
        for p in range(npair):
            if near_rows is None:
                logits = ss[p] + mb
                const = jnp.where(first, rb_ref[NUM_BUCKETS - 1, 2 * p], rb_ref[NUM_BUCKETS - 1, 2 * p + 1])
            else:
                logits = ss[p] + (near_ref[p, near_rows:near_rows + tq, :] + mb)
                const = jnp.zeros((1, 2 * tq), F32)
            m_old = m_ref[p]
            m_new = jnp.maximum(m_old, jnp.max(logits, axis=0, keepdims=True) + const)
            alphas.append(jnp.exp(m_old - m_new))
            prs.append(jnp.exp(logits - (m_new[0:1, :] - const)).astype(BF16))
            m_ref[p] = m_new
        for p in range(npair):
            vt = jnp.concatenate([vbt_ref[0, j, p * LANES:(p + 1) * LANES, :], ones_rows], axis=0)
            pv = jnp.dot(vt, prs[p], preferred_element_type=F32)
            l_ref[p] = alphas[p] * l_ref[p] + pv[LANES:LANES + SUBLANES, :]
            acc_ref[p] = alphas[p][0:1, :] * acc_ref[p] + pv[0:LANES, :]

    def far_body(j, carry):
        attend(j, None)
        return carry

    lax.fori_loop(0, jnp.maximum(i - 1, 0), far_body, 0)

    @pl.when(i >= 1)
    def _():
        attend(i - 1, 0)

    attend(i, tq)

    for p in range(npair):
        o = acc_ref[p] / l_ref[p][0:1, :]
        ot = jnp.concatenate([o[0:HEAD_DIM, 0:tq], o[HEAD_DIM:LANES, tq:2 * tq]], axis=0)
        o_ref[0, :, p * LANES:(p + 1) * LANES] = ot.T


def _attn_prompt(qt, qit, wit, kb, vbt, kdb, rel_bias, bsz, t, tq):
    topk = min(MAX_TOPK, t // 4)
    pos_bits = max(1, (t - 1).bit_length())
    near = _near_bias(rel_bias, tq)
    npair = ATT_HEADS // 2
    nblk = t // tq
    qtile_t = lambda rows: pl.BlockSpec((1, rows, tq), lambda b, i: (b, 0, i))
    full = lambda w: pl.BlockSpec((1, t, w), lambda b, i: (b, 0, 0))
    return pl.pallas_call(
        functools.partial(_attn_prompt_kernel, tq=tq, topk=topk, pos_bits=pos_bits),
        grid=(bsz, nblk),
        in_specs=[pl.BlockSpec(memory_space=pltpu.SMEM),
                  qtile_t(ATT_W), qtile_t(IDX_HEADS * IDX_DIM), qtile_t(SUBLANES),
                  full(ATT_W), pl.BlockSpec((1, nblk, ATT_W, tq), lambda b, i: (b, 0, 0, 0)), full(LANES),
                  _const_spec((npair, 2 * tq, 2 * tq))],
        out_specs=pl.BlockSpec((1, tq, ATT_W), lambda b, i: (b, i, 0)),
        out_shape=jax.ShapeDtypeStruct((bsz, t, ATT_W), F32),
        scratch_shapes=[pltpu.VMEM((t, tq), F32),
                        pltpu.VMEM((npair, LANES, 2 * tq), BF16),
                        pltpu.VMEM((IDX_HEADS // 2, LANES, 2 * tq), BF16),
                        pltpu.VMEM((npair, SUBLANES, 2 * tq), F32),
                        pltpu.VMEM((npair, SUBLANES, 2 * tq), F32),
                        pltpu.VMEM((npair, LANES, 2 * tq), F32)],
        compiler_params=_cparams(("arbitrary", "arbitrary")),
        name="attn_prompt",
    )(rel_bias, qt, qit, wit, kb.reshape(bsz, t, ATT_W), vbt, kdb.reshape(bsz, t, LANES), near)


def _sample_index_kernel(pt_ref, qi_ref, wi_ref, *rest, n_pages, ts):
    page_refs = rest[:n_pages + 1]
    o_ref = rest[n_pages + 1]
    qi = qi_ref[0]
    q16 = jnp.concatenate([qi[:, h * IDX_DIM:(h + 1) * IDX_DIM] for h in range(IDX_HEADS)], axis=0).astype(BF16)
    wi = wi_ref[0]
    scale = IDX_HEADS ** -0.5 * IDX_DIM ** -0.5
    w16 = jnp.concatenate([wi[:, h:h + 1] for h in range(IDX_HEADS)], axis=0) * scale
    trow = lax.broadcasted_iota(I32, (ts, PAGE_SIZE), 0)
    ncol = lax.broadcasted_iota(I32, (ts, PAGE_SIZE), 1)
    for j in range(n_pages + 1):
        kp = page_refs[j][0].astype(BF16)
        if j < n_pages:
            sc = jnp.dot(q16, kp, preferred_element_type=F32)
        else:
            sc = lax.dot_general(q16, kp, (((1,), (1,)), ((), ())), preferred_element_type=F32)
        rw = jnp.maximum(sc, 0.0) * w16
        idx = rw[0:ts]
        for h in range(1, IDX_HEADS):
            idx = idx + rw[h * ts:(h + 1) * ts]
        if j == n_pages:
            idx = jnp.where(ncol <= trow, idx, -jnp.inf)
        o_ref[0, :, j * PAGE_SIZE:(j + 1) * PAGE_SIZE] = idx


def _sample_select_kernel(idx_ref, sel_ref, key_ref, *, topk, pos_bits):
    lp, nq = idx_ref.shape
    idx = idx_ref[...]
    key_ref[...] = jnp.where(idx == 0.0, 0.0, idx)
    theta, cut = _select_params(key_ref, 1, lp, topk, nq, pos_bits)
    kb = key_ref[...]
    pos = lax.broadcasted_iota(I32, (lp, nq), 0)
    sel = ((kb > theta) | ((kb == theta) & (pos <= cut))) & (idx_ref[...] > -jnp.inf)
    sel_ref[...] = sel.astype(F32)


def _sample_bias_kernel(rb_ref, near_ref, new_ref, far_ref, *, ts):
    rows = ts * ATT_HEADS
    r = lax.broadcasted_iota(I32, (rows, PAGE_SIZE), 0)
    c = lax.broadcasted_iota(I32, (rows, PAGE_SIZE), 1)
    t = r // ATT_HEADS
    rhead = r % ATT_HEADS

    def gather(bucket):
        acc = jnp.zeros((rows, PAGE_SIZE), F32)
        for h in range(ATT_HEADS):
            for bk in range(NUM_BUCKETS):
                acc = jnp.where((bucket == bk) & (rhead == h), rb_ref[bk, h], acc)
        return acc

    near_ref[...] = gather(_t5_bucket(PAGE_SIZE + t - c))
    ok = (c <= t) & (c < ts)
    new_ref[...] = jnp.where(ok, gather(_t5_bucket(t - c)), -jnp.inf)
    far_ref[...] = gather(jnp.full((rows, PAGE_SIZE), NUM_BUCKETS - 1, I32))


def _sample_attn_kernel(pt_ref, q_ref, sel_ref, near_ref, new_ref, far_ref, *rest, n_pages, ts):
    k_refs = rest[:n_pages + 1]
    v_refs = rest[n_pages + 1:2 * n_pages + 2]
    o_ref = rest[2 * n_pages + 2]
    l_ref = rest[2 * n_pages + 3]
    rows = ts * ATT_HEADS
    rep = lambda a: jnp.concatenate([jnp.broadcast_to(a[t:t + 1], (ATT_HEADS, a.shape[1])) for t in range(ts)], axis=0)
    r = lax.broadcasted_iota(I32, (rows, ATT_W), 0)
    c = lax.broadcasted_iota(I32, (rows, ATT_W), 1)
    own = (r % ATT_HEADS) == (c // HEAD_DIM)
    q = rep(q_ref[0].astype(F32) * HEAD_DIM ** -0.5)
    q = jnp.where(own, q, 0.0).astype(BF16)
    sel = rep(sel_ref[0]) > 0.5

    for j in range(n_pages):
        kt = k_refs[j][0].reshape(ATT_W, PAGE_SIZE).astype(BF16)
        s = jnp.dot(q, kt, preferred_element_type=F32)
        bias = near_ref[...] if j == n_pages - 1 else far_ref[...]
        sl = slice(j * PAGE_SIZE, (j + 1) * PAGE_SIZE)
        l_ref[:, sl] = jnp.where(sel[:, sl], s + bias, -jnp.inf)
    kn = k_refs[n_pages][0].astype(BF16)
    s = lax.dot_general(q, kn, (((1,), (1,)), ((), ())), preferred_element_type=F32)
    sl = slice(n_pages * PAGE_SIZE, (n_pages + 1) * PAGE_SIZE)
    l_ref[:, sl] = jnp.where(sel[:, sl], s + new_ref[...], -jnp.inf)

    logits = l_ref[...]
    m = jnp.max(logits, axis=1, keepdims=True)
    pr = jnp.exp(logits - m)
    den = jnp.sum(pr, axis=1, keepdims=True)
    prb = pr.astype(BF16)
    acc = jnp.dot(prb[:, sl], v_refs[n_pages][0].astype(BF16), preferred_element_type=F32)
    for j in range(n_pages):
        vt = v_refs[j][0].reshape(ATT_W, PAGE_SIZE).astype(BF16)
        acc = acc + lax.dot_general(prb[:, j * PAGE_SIZE:(j + 1) * PAGE_SIZE], vt, (((1,), (1,)), ((), ())),
                                    preferred_element_type=F32)
    acc = jnp.where(own, acc / den, 0.0)
    o_ref[0] = jnp.sum(acc.reshape(ts, ATT_HEADS, ATT_W), axis=1)


def _attn_sample(qb, qi, wi, k_new, v_new, kd_new, cache_k, cache_v, cache_kidx, page_table, rel_bias, bsz, ts):
    n_pages = page_table.shape[1]
    past_len = n_pages * PAGE_SIZE
    topk = min(MAX_TOPK, (past_len + ts) // 4)
    lp = (n_pages + 1) * PAGE_SIZE
    rows = ts * ATT_HEADS
    pt = page_table.reshape(-1).astype(I32)

    ki_pool = jnp.swapaxes(cache_kidx, 1, 2)
    ki_new = jnp.zeros((bsz, PAGE_SIZE, IDX_DIM), F32).at[:, 0:ts].set(kd_new.reshape(bsz, ts, LANES)[..., 0:IDX_DIM])
    page_spec = lambda j, shp: pl.BlockSpec((1,) + shp, lambda b, ptr: (ptr[b * n_pages + j],) + (0,) * len(shp))
    per_b = lambda shp: pl.BlockSpec((1,) + shp, lambda b, ptr: (b,) + (0,) * len(shp))
    idx = pl.pallas_call(
        functools.partial(_sample_index_kernel, n_pages=n_pages, ts=ts),
        grid_spec=pltpu.PrefetchScalarGridSpec(
            num_scalar_prefetch=1, grid=(bsz,),
            in_specs=[per_b((ts, IDX_HEADS * IDX_DIM)), per_b((ts, LANES))]
                     + [page_spec(j, (IDX_DIM, PAGE_SIZE)) for j in range(n_pages)]
                     + [per_b((PAGE_SIZE, IDX_DIM))],
            out_specs=per_b((ts, lp))),
        out_shape=jax.ShapeDtypeStruct((bsz, ts, lp), F32),
        compiler_params=_cparams(("arbitrary",)),
        name="sample_index",
    )(pt, qi.reshape(bsz, ts, -1), wi.reshape(bsz, ts, LANES), *([ki_pool] * n_pages), ki_new)

    sel = pl.pallas_call(
        functools.partial(_sample_select_kernel, topk=topk, pos_bits=max(1, (lp - 1).bit_length())),
        out_shape=jax.ShapeDtypeStruct((lp, bsz * ts), F32),
        scratch_shapes=[pltpu.VMEM((lp, bsz * ts), F32)],
        compiler_params=pltpu.CompilerParams(vmem_limit_bytes=VMEM_LIMIT),
        name="sample_select",
    )(idx.reshape(bsz * ts, lp).T)
    sel = sel.T.reshape(bsz, ts, lp)

    near, newb, far = pl.pallas_call(
        functools.partial(_sample_bias_kernel, ts=ts),
        in_specs=[pl.BlockSpec(memory_space=pltpu.SMEM)],
        out_shape=[jax.ShapeDtypeStruct((rows, PAGE_SIZE), F32)] * 3,
        compiler_params=pltpu.CompilerParams(vmem_limit_bytes=VMEM_LIMIT),
        name="sample_bias",
    )(rel_bias)

    page4 = (ATT_HEADS, HEAD_DIM, PAGE_SIZE)
    k_pool = jnp.transpose(cache_k, (0, 2, 3, 1))
    v_pool = jnp.transpose(cache_v, (0, 2, 3, 1))
    pad_new = lambda a: jnp.zeros((bsz, PAGE_SIZE, ATT_W), F32).at[:, 0:ts].set(a.reshape(bsz, ts, ATT_W))
    const = lambda shp: pl.BlockSpec(shp, lambda b, ptr: (0,) * len(shp))
    out = pl.pallas_call(
        functools.partial(_sample_attn_kernel, n_pages=n_pages, ts=ts),
        grid_spec=pltpu.PrefetchScalarGridSpec(
            num_scalar_prefetch=1, grid=(bsz,),
            in_specs=[per_b((ts, ATT_W)), per_b((ts, lp)),
                      const((rows, PAGE_SIZE)), const((rows, PAGE_SIZE)), const((rows, PAGE_SIZE))]
                     + [page_spec(j, page4) for j in range(n_pages)] + [per_b((PAGE_SIZE, ATT_W))]
                     + [page_spec(j, page4) for j in range(n_pages)] + [per_b((PAGE_SIZE, ATT_W))],
            out_specs=per_b((ts, ATT_W)),
            scratch_shapes=[pltpu.VMEM((rows, lp), F32)]),
        out_shape=jax.ShapeDtypeStruct((bsz, ts, ATT_W), F32),
        compiler_params=_cparams(("arbitrary",)),
        name="sample_attn",
    )(pt, qb.reshape(bsz, ts, ATT_W), sel, near, newb, far,
      *([k_pool] * n_pages), pad_new(k_new), *([v_pool] * n_pages), pad_new(v_new))
    return out.reshape(bsz * ts, ATT_W)


def _layer_norm(x, g, b):
    mu = jnp.mean(x, axis=-1, keepdims=True)
    xc = x - mu
    var = jnp.mean(xc * xc, axis=-1, keepdims=True)
    return xc * lax.rsqrt(var + LN_EPS) * g + b


def _ffn_kernel(x_ref, rw_ref, att_ref, prev_ref, wo_ref, g1_ref, b1_ref, wup_ref, cw_ref, cb_ref, wdn_ref,
                g2_ref, b2_ref, y_ref, nc_ref, pad_ref, *, tm, shift, off, alpha):
    ti = pl.program_id(1)
    data = off + 2 * shift

    @pl.when(ti == 0)
    def _():
        pad_ref[off:data, :] = prev_ref[0]

    x = x_ref[0]
    mix = (jnp.dot(rw_ref[0].astype(BF16), wo_ref[0:RW_W, :], preferred_element_type=F32)
           + jnp.dot(att_ref[0].astype(BF16), wo_ref[RW_W:RW_W + ATT_W, :], preferred_element_type=F32))
    x1 = _layer_norm(alpha * x + mix, g1_ref[...], b1_ref[...])
    x1b = x1.astype(BF16)
    u_conv = jnp.dot(x1b, wup_ref[:, 0:D_FF], preferred_element_type=F32)
    u_lin = jnp.dot(x1b, wup_ref[:, D_FF:2 * D_FF], preferred_element_type=F32)
    pad_ref[data:data + tm, :] = u_conv
    conv = (pad_ref[off:off + tm, :] * cw_ref[0:1, :] + pad_ref[off + shift:off + shift + tm, :] * cw_ref[1:2, :]
            + u_conv * cw_ref[2:3, :] + cb_ref[...])
    h = conv * _sigmoid(conv) * u_lin
    down = jnp.dot(h.astype(BF16), wdn_ref[...], preferred_element_type=F32)
    y_ref[0] = _layer_norm(alpha * x1 + down, g2_ref[...], b2_ref[...])
    tail = pad_ref[off + tm:data + tm, :]
    pad_ref[off:data, :] = tail
    nc_ref[0] = tail


def _ffn(x, rw, att, prev, lw, alpha, shift):
    nb, rows, _ = x.shape
    tm = min(256, rows) if shift == 1 else rows
    off = 6 if shift == 1 else 0
    row = lambda a: a.reshape(1, -1)
    tile = lambda w: pl.BlockSpec((1, tm, w), lambda b, i: (b, i, 0))
    y, nc = pl.pallas_call(
        functools.partial(_ffn_kernel, tm=tm, shift=shift, off=off, alpha=alpha),
        grid=(nb, rows // tm),
        in_specs=[tile(D_MODEL), tile(RW_W), tile(ATT_W),
                  pl.BlockSpec((1, 2 * shift, D_FF), lambda b, i: (b, 0, 0)),
                  _const_spec((D_MODEL, D_MODEL)), _const_spec((1, D_MODEL)), _const_spec((1, D_MODEL)),
                  _const_spec((D_MODEL, 2 * D_FF)), _const_spec((CONV_W, D_FF)), _const_spec((1, D_FF)),
                  _const_spec((D_FF, D_MODEL)), _const_spec((1, D_MODEL)), _const_spec((1, D_MODEL))],
        out_specs=[tile(D_MODEL), pl.BlockSpec((1, 2 * shift, D_FF), lambda b, i: (b, 0, 0))],
        out_shape=[jax.ShapeDtypeStruct((nb, rows, D_MODEL), F32), jax.ShapeDtypeStruct((nb, 2 * shift, D_FF), F32)],
        scratch_shapes=[pltpu.VMEM((off + 2 * shift + tm, D_FF), F32)],
        compiler_params=_cparams(("arbitrary", "arbitrary")),
        name="ffn",
    )(x, rw, att, prev, lw["w_out"].astype(BF16), row(lw["ln1_g"]), row(lw["ln1_b"]),
      lw["ffn_w_up"].astype(BF16), lw["ffn_conv_w"], row(lw["ffn_conv_b"]), lw["ffn_w_down"].astype(BF16),
      row(lw["ln2_g"]), row(lw["ln2_b"]))
    return y, nc


def _layer(x, shift_prev, wkv_prev, conv_prev, past, rel_bias, lw, alpha):
    bsz, t, _ = x.shape
    w_in = lw["w_in"]
    w_ki = w_in[:, C_KI:C_KI + IDX_DIM]
    w_wi = w_in[:, C_KI + IDX_DIM:]
    wp = jnp.concatenate([w_in[:, 0:C_KI], w_ki, w_ki, w_wi,
                          jnp.zeros((D_MODEL, LANES - IDX_HEADS), F32)], axis=1).astype(BF16)
    x2d = x.reshape(bsz * t, D_MODEL)
    if past is None:
        tq = min(256, t)
        prw, qt, kb, kt, vt, vbt, qit, kdb, kit, wit = _proj_t(x2d, wp, bsz, t, tq)
        rw_out, new_wkv, new_shift = _rwkv(prw.reshape(bsz, t, SHIFT_COLS), shift_prev, wkv_prev, lw)
        att = _attn_prompt(qt, qit, wit, kb, vbt, kdb, rel_bias, bsz, t, tq)
        y, new_conv = _ffn(x, rw_out, att, conv_prev, lw, alpha, shift=1)
        heads = lambda a: jnp.transpose(a.reshape(bsz, ATT_HEADS, HEAD_DIM, t), (0, 3, 1, 2))
        return y, (heads(kt), heads(vt), jnp.swapaxes(kit, 1, 2), new_wkv, new_shift, new_conv)
    prw, qb, k, v, qi, kd, wi = _proj(x2d, wp)
    rw_out, new_wkv, new_shift = _rwkv(prw.reshape(bsz, t, SHIFT_COLS), shift_prev, wkv_prev, lw)
    att = _attn_sample(qb, qi, wi, k, v, kd, *past, rel_bias, bsz, t)
    tmaj = lambda a: jnp.swapaxes(a.reshape(bsz, t, -1), 0, 1).reshape(1, t * bsz, -1)
    prev = jnp.swapaxes(conv_prev, 0, 1).reshape(1, (CONV_W - 1) * bsz, D_FF)
    y, new_conv = _ffn(tmaj(x), tmaj(rw_out), tmaj(att), prev, lw, alpha, shift=bsz)
    y = jnp.swapaxes(y.reshape(t, bsz, D_MODEL), 0, 1)
    new_conv = jnp.swapaxes(new_conv.reshape(CONV_W - 1, bsz, D_FF), 0, 1)
    state = (k.reshape(bsz, t, ATT_HEADS, HEAD_DIM), v.reshape(bsz, t, ATT_HEADS, HEAD_DIM),
             kd.reshape(bsz, t, LANES)[..., 0:IDX_DIM], new_wkv, new_shift, new_conv)
    return y, state


_LAYER_WEIGHTS = ("w_in", "shift_mu", "rw_w0", "rw_w_up", "rw_a0", "rw_a_up", "rw_g_up", "rw_k_k", "rw_k_a",
                  "rw_r_k", "rw_gn_g", "rw_gn_b", "w_out", "ln1_g", "ln1_b", "ffn_w_up", "ffn_conv_w",
                  "ffn_conv_b", "ffn_w_down", "ln2_g", "ln2_b")


def kernel(x_prompt, x_sample, cache_k, cache_v, cache_kidx, page_table, state_wkv, state_shift, state_conv,
           rel_bias, w_in, shift_mu, rw_w0, rw_w_up, rw_a0, rw_a_up, rw_g_up, rw_k_k, rw_k_a, rw_r_k, rw_gn_g,
           rw_gn_b, w_out, ln1_g, ln1_b, ffn_w_up, ffn_conv_w, ffn_conv_b, ffn_w_down, ln2_g, ln2_b):
    stacked = dict(zip(_LAYER_WEIGHTS, (w_in, shift_mu, rw_w0, rw_w_up, rw_a0, rw_a_up, rw_g_up, rw_k_k, rw_k_a,
                                        rw_r_k, rw_gn_g, rw_gn_b, w_out, ln1_g, ln1_b, ffn_w_up, ffn_conv_w,
                                        ffn_conv_b, ffn_w_down, ln2_g, ln2_b)))
    depth = w_in.shape[0]
    alpha = (2 * depth) ** 0.25
    bp = x_prompt.shape[0]
    yp, ys = x_prompt, x_sample
    outs_p, outs_s = [], []
    for l in range(depth):
        lw = {name: w[l] for name, w in stacked.items()}
        yp, st_p = _layer(yp, jnp.zeros((bp, 1, SHIFT_COLS), F32),
                          jnp.zeros((bp, RW_HEADS, HEAD_DIM, HEAD_DIM), F32),
                          jnp.zeros((bp, CONV_W - 1, D_FF), F32), None, rel_bias, lw, alpha)
        ys, st_s = _layer(ys, state_shift[l], state_wkv[l], state_conv[l],
                          (cache_k[l], cache_v[l], cache_kidx[l], page_table), rel_bias, lw, alpha)
        outs_p.append(st_p)
        outs_s.append(st_s)
    st = lambda outs, i: jnp.stack([o[i] for o in outs])
    return (yp, ys) + tuple(st(outs_p, i) for i in range(6)) + tuple(st(outs_s, i) for i in range(6))
```

```python
import functools
import math

import jax
import jax.numpy as jnp
from jax import lax
from jax.experimental import pallas as pl
from jax.experimental.pallas import tpu as pltpu

F32 = jnp.float32
BF16 = jnp.bfloat16
I32 = jnp.int32
HIGHEST = lax.Precision.HIGHEST

D_MODEL = 1024
HEAD_DIM = 64
RW_W = 512
ATT_W = 512
RW_HEADS = 8
ATT_HEADS = 8
IDX_HEADS = 4
IDX_DIM = 64
MAX_TOPK = 256
W_LORA = 64
A_LORA = 64
G_LORA = 128
GN_EPS = 64e-5
LN_EPS = 1e-5
D_FF = 2816
CONV_W = 3
NUM_BUCKETS = 32
MAX_DISTANCE = 128
PAGE_SIZE = 128
SHIFT_COLS = 3 * RW_W + W_LORA + A_LORA + G_LORA
LANES = 128
SUBLANES = 8
C_Q = SHIFT_COLS
C_K = C_Q + ATT_W
C_V = C_K + ATT_W
C_QI = C_V + ATT_W
C_KI = C_QI + IDX_HEADS * IDX_DIM
C_WI = C_KI + LANES
C_END = C_WI + LANES
VMEM_LIMIT = 56 * 1024 * 1024
INT_MIN = -(2 ** 31)
ORD_NEG_INF = INT_MIN + 0x7FFFFF
NEG_BIG = -1e30


def _cparams(sem):
    return pltpu.CompilerParams(dimension_semantics=sem, vmem_limit_bytes=VMEM_LIMIT)


def _const_spec(shape):
    zeros = (0,) * len(shape)
    return pl.BlockSpec(shape, lambda *_: zeros, pipeline_mode=pl.Buffered(1))


def _proj_kernel(x_ref, w_ref, prw_ref, qb_ref, k_ref, v_ref, qi_ref, kd_ref, wi_ref):
    x = x_ref[...].astype(BF16)

    def mm(lo, hi):
        return jnp.dot(x, w_ref[:, lo:hi], preferred_element_type=F32)

    prw_ref[...] = mm(0, C_Q)
    qb_ref[...] = mm(C_Q, C_K).astype(BF16)
    k_ref[...] = mm(C_K, C_V)
    v_ref[...] = mm(C_V, C_QI)
    qi_ref[...] = mm(C_QI, C_KI)
    kd_ref[...] = mm(C_KI, C_WI)
    wi_ref[...] = mm(C_WI, C_END)


def _proj(x2d, wp):
    m = x2d.shape[0]
    tm = min(256, m)
    widths = [(C_Q, F32), (ATT_W, BF16), (ATT_W, F32), (ATT_W, F32), (IDX_HEADS * IDX_DIM, F32), (LANES, F32),
              (LANES, F32)]
    return pl.pallas_call(
        _proj_kernel,
        grid=(m // tm,),
        in_specs=[pl.BlockSpec((tm, D_MODEL), lambda i: (i, 0)), _const_spec((D_MODEL, C_END))],
        out_specs=[pl.BlockSpec((tm, w), lambda i: (i, 0)) for w, _ in widths],
        out_shape=[jax.ShapeDtypeStruct((m, w), dt) for w, dt in widths],
        compiler_params=_cparams(("arbitrary",)),
        name="proj",
    )(x2d, wp)


def _proj_t_kernel(x_ref, w_ref, wt_ref, prw_ref, qt_ref, kb_ref, kt_ref, vt_ref, vbt_ref, qit_ref, kdb_ref, kit_ref,
                   wit_ref):
    x = x_ref[...].astype(BF16)

    def mm(lo, hi):
        return jnp.dot(x, w_ref[:, lo:hi], preferred_element_type=F32)

    def mt(lo, hi):
        return lax.dot_general(wt_ref[lo:hi, :], x, (((1,), (1,)), ((), ())), preferred_element_type=F32)

    prw_ref[...] = mm(0, C_Q)
    kb_ref[...] = mm(C_K, C_V).astype(BF16)
    kdb_ref[...] = mm(C_KI, C_WI).astype(BF16)
    qt_ref[0] = (mt(C_Q, C_K) * HEAD_DIM ** -0.5).astype(BF16)
    kt_ref[0] = mt(C_K, C_V)
    vt = mt(C_V, C_QI)
    vt_ref[0] = vt
    vbt_ref[0, 0] = vt.astype(BF16)
    qit_ref[0] = mt(C_QI, C_KI).astype(BF16)
    kit_ref[0] = mt(C_KI, C_KI + IDX_DIM)
    wit_ref[0] = mt(C_WI, C_WI + SUBLANES)


def _proj_t(x2d, wp, bsz, t, tm):
    m = x2d.shape[0]
    nblk = t // tm
    rows = lambda w: pl.BlockSpec((tm, w), lambda b, i: (b * nblk + i, 0))
    cols = lambda r: pl.BlockSpec((1, r, tm), lambda b, i: (b, 0, i))
    tshape = lambda r, dt: jax.ShapeDtypeStruct((bsz, r, t), dt)
    return pl.pallas_call(
        _proj_t_kernel,
        grid=(bsz, nblk),
        in_specs=[rows(D_MODEL), _const_spec((D_MODEL, C_END)), _const_spec((C_END, D_MODEL))],
        out_specs=[rows(C_Q), cols(ATT_W), rows(ATT_W), cols(ATT_W), cols(ATT_W),
                   pl.BlockSpec((1, 1, ATT_W, tm), lambda b, i: (b, i, 0, 0)),
                   cols(IDX_HEADS * IDX_DIM), rows(LANES), cols(IDX_DIM), cols(SUBLANES)],
        out_shape=[jax.ShapeDtypeStruct((m, C_Q), F32), tshape(ATT_W, BF16), jax.ShapeDtypeStruct((m, ATT_W), BF16),
                   tshape(ATT_W, F32), tshape(ATT_W, F32), jax.ShapeDtypeStruct((bsz, nblk, ATT_W, tm), BF16),
                   tshape(IDX_HEADS * IDX_DIM, BF16), jax.ShapeDtypeStruct((m, LANES), BF16),
                   tshape(IDX_DIM, F32), tshape(SUBLANES, F32)],
        compiler_params=_cparams(("arbitrary", "arbitrary")),
        name="proj_t",
    )(x2d, wp, wp.T)


def _softplus(z):
    return jnp.maximum(z, 0.0) + jnp.log1p(jnp.exp(-jnp.abs(z)))


def _sigmoid(z):
    return 1.0 / (1.0 + jnp.exp(-z))


def _seg_sum(x, ones_bd):
    parts = [jnp.dot(x[:, g * LANES:(g + 1) * LANES], ones_bd, precision=HIGHEST, preferred_element_type=F32)
             for g in range(x.shape[1] // LANES)]
    return jnp.concatenate(parts, axis=1)


def _rwkv_kernel(p_ref, sp_ref, s0_ref, mu_ref, w0_ref, lw_ref, a0_ref, gup_ref, kk_ref, ka_ref, rk_ref,
                 gng_ref, gnb_ref, ones2_ref, diag_ref,
                 out_ref, sfin_ref, shift_ref,
                 pad_ref, carry_ref, s_ref, kkv_ref, dec_ref, b_ref, k2_ref, r_ref, v_ref, g_ref, bon_ref, y_ref,
                 *, nb, tc_len):
    tc = pl.program_id(1)
    last = pl.num_programs(1) - 1
    npair = RW_HEADS // 2

    @pl.when(tc == 0)
    def _():
        carry_ref[...] = sp_ref[...]
        for b in range(nb):
            for hp in range(npair):
                s_ref[b, hp] = jnp.concatenate([s0_ref[b, 2 * hp], s0_ref[b, 2 * hp + 1]], axis=1)

    rows = lax.broadcasted_iota(I32, (LANES, LANES), 0)
    cols = lax.broadcasted_iota(I32, (LANES, LANES), 1)
    ones_bd = (rows // HEAD_DIM == cols // HEAD_DIM).astype(F32)
    lane = lax.broadcasted_iota(I32, (tc_len, LANES), 1)

    for b in range(nb):
        x = p_ref[b]
        pad_ref[b, 7:8, :] = carry_ref[b]
        pad_ref[b, 8:8 + tc_len, :] = x
        xm1 = pad_ref[b, 7:7 + tc_len, :]
        ps = x + (xm1 - x) * mu_ref[...]
        carry_ref[b] = x[tc_len - 1:tc_len, :]
        r = ps[:, 0:RW_W]
        k = ps[:, RW_W:2 * RW_W]
        v = ps[:, 2 * RW_W:3 * RW_W]
        o = 3 * RW_W
        z = ps[:, o:o + LANES]
        z = jnp.where(lane < W_LORA, jnp.tanh(z), z)
        lo = jnp.dot(z, lw_ref[...], precision=HIGHEST, preferred_element_type=F32)
        w_log = -_softplus(-(w0_ref[...] + lo[:, 0:RW_W])) - 0.5
        dec = jnp.exp(-jnp.exp(w_log))
        a = _sigmoid(a0_ref[...] + lo[:, RW_W:2 * RW_W])
        gd = ps[:, o + LANES:o + 2 * LANES]
        g = jnp.dot(_sigmoid(gd), gup_ref[...], precision=HIGHEST, preferred_element_type=F32)
        kkx = k * kk_ref[...]
        ss = _seg_sum(kkx * kkx, ones_bd)
        kkn = kkx * lax.rsqrt(jnp.maximum(ss, 1e-24))
        k2 = k * (1.0 + (a - 1.0) * ka_ref[...])
        bonus = _seg_sum(r * k2 * rk_ref[...], ones_bd) * v
        kkv_ref[b, 0:tc_len] = kkn
        dec_ref[b, 0:tc_len] = dec
        b_ref[b, 0:tc_len] = kkn * a
        k2_ref[b, 0:tc_len] = k2
        r_ref[b, 0:tc_len] = r
        v_ref[b, 0:tc_len] = v
        g_ref[b, 0:tc_len] = g
        bon_ref[b, 0:tc_len] = bonus

    diag = diag_ref[...]
    diag_b = diag.astype(BF16)
    drow = lax.broadcasted_iota(I32, (HEAD_DIM, LANES), 0)
    dlane = lax.broadcasted_iota(I32, (HEAD_DIM, LANES), 1)
    diag2_b = ((drow + 1) % HEAD_DIM == dlane % HEAD_DIM).astype(BF16)
    seg_start = lax.broadcasted_iota(I32, (SUBLANES, LANES), 1) % HEAD_DIM == 0

    sub = min(SUBLANES, tc_len)
    riota = lax.broadcasted_iota(I32, (SUBLANES, LANES), 0)

    tiles = [(b, hp) for b in range(nb) for hp in range(npair)]
    nt = len(tiles)

    def bsum(xs):
        lhs = jnp.concatenate([jnp.concatenate([xs[2 * m], xs[2 * m + 1]], axis=1) for m in range(nt // 2)], axis=0)
        red = jnp.dot(lhs, ones2_ref[...], preferred_element_type=F32)
        return [red[(k // 2) * HEAD_DIM:(k // 2 + 1) * HEAD_DIM, (k % 2) * LANES:(k % 2 + 1) * LANES]
                for k in range(nt)]

    def group(gi, carry):
        t0 = pl.multiple_of(gi * SUBLANES, SUBLANES)
        rows8 = [[ref[b, pl.ds(t0, SUBLANES), hp * LANES:(hp + 1) * LANES]
                  for ref in (kkv_ref, dec_ref, b_ref, k2_ref, r_ref, v_ref)] for b, hp in tiles]
        vhi8 = [rows8[k][5].astype(BF16).astype(F32) for k in range(nt)]
        vmid8 = [rows8[k][5] - vhi8[k] for k in range(nt)]
        vmid8 = [jnp.where(seg_start, pltpu.roll(x, LANES - HEAD_DIM + 1, 1), pltpu.roll(x, 1, 1)) for x in vmid8]
        s = [s_ref[b, hp] for b, hp in tiles]
        y8 = [jnp.zeros((SUBLANES, LANES), F32) for _ in tiles]
        tile_of = lambda row: jnp.broadcast_to(row, (HEAD_DIM, LANES)).astype(BF16)
        for i in range(sub):
            kkr, wr, br, kr, rr = [[rows8[k][q][i:i + 1, :] for k in range(nt)] for q in range(5)]
            sa = bsum([(s[k] * kkr[k]).astype(BF16) for k in range(nt)])
            vb = bsum([tile_of(vhi8[k][i:i + 1, :]) * diag_b + tile_of(vmid8[k][i:i + 1, :]) * diag2_b
                       for k in range(nt)])
            for k in range(nt):
                s[k] = s[k] * wr[k] - sa[k] * br[k] + vb[k] * kr[k]
            yb = bsum([(s[k] * rr[k]).astype(BF16) for k in range(nt)])
            for k in range(nt):
                yrow = jnp.sum(yb[k] * diag, axis=0, keepdims=True)
                y8[k] = jnp.where(riota == i, yrow, y8[k])
        for k, (b, hp) in enumerate(tiles):
            s_ref[b, hp] = s[k]
            y_ref[b, pl.ds(t0, SUBLANES), hp * LANES:(hp + 1) * LANES] = y8[k]
        return carry

    lax.fori_loop(0, max(1, tc_len // SUBLANES), group, 0)

    for b in range(nb):
        y = y_ref[b, 0:tc_len]
        mu = _seg_sum(y, ones_bd) * (1.0 / HEAD_DIM)
        yc = y - mu
        var = _seg_sum(yc * yc, ones_bd) * (1.0 / HEAD_DIM)
        yn = yc * lax.rsqrt(var + GN_EPS) * gng_ref[...] + gnb_ref[...]
        out_ref[b] = (yn + bon_ref[b, 0:tc_len]) * g_ref[b, 0:tc_len]

    @pl.when(tc == last)
    def _():
        shift_ref[...] = carry_ref[...]
        for b in range(nb):
            for hp in range(npair):
                s = s_ref[b, hp]
                sfin_ref[b, 2 * hp] = s[:, 0:HEAD_DIM]
                sfin_ref[b, 2 * hp + 1] = s[:, HEAD_DIM:2 * HEAD_DIM]


def _rwkv(p, shift_prev, wkv_prev, lw):
    bsz, t, _ = p.shape
    nb = 4
    tc_len = min(128, t)
    row = lambda a: a.reshape(1, -1)
    w_up, a_up = lw["rw_w_up"], lw["rw_a_up"]
    lora = jnp.zeros((LANES, 2 * RW_W), F32)
    lora = lora.at[0:W_LORA, 0:RW_W].set(w_up).at[W_LORA:LANES, RW_W:2 * RW_W].set(a_up)
    rr = jnp.arange(2 * LANES)[:, None]
    cc = jnp.arange(2 * LANES)[None, :]
    ones2 = (rr // HEAD_DIM == cc // HEAD_DIM).astype(BF16)
    diag = (jnp.arange(HEAD_DIM)[:, None] == cc[:, 0:LANES] % HEAD_DIM).astype(F32)
    blk3 = lambda w: pl.BlockSpec((nb, tc_len, w), lambda i, j: (i, j, 0))
    vec = lambda w: _const_spec((1, w))
    assert tc_len < SUBLANES or tc_len % SUBLANES == 0
    scr = lambda w: pltpu.VMEM((nb, max(SUBLANES, tc_len), w), F32)
    out, sfin, shift = pl.pallas_call(
        functools.partial(_rwkv_kernel, nb=nb, tc_len=tc_len),
        grid=(bsz // nb, t // tc_len),
        in_specs=[blk3(SHIFT_COLS),
                  pl.BlockSpec((nb, 1, SHIFT_COLS), lambda i, j: (i, 0, 0)),
                  pl.BlockSpec((nb, RW_HEADS, HEAD_DIM, HEAD_DIM), lambda i, j: (i, 0, 0, 0)),
                  vec(SHIFT_COLS), vec(RW_W), _const_spec((LANES, 2 * RW_W)), vec(RW_W),
                  _const_spec((G_LORA, RW_W)), vec(RW_W), vec(RW_W), vec(RW_W), vec(RW_W), vec(RW_W),
                  _const_spec((2 * LANES, 2 * LANES)), _const_spec((HEAD_DIM, LANES))],
        out_specs=[blk3(RW_W),
                   pl.BlockSpec((nb, RW_HEADS, HEAD_DIM, HEAD_DIM), lambda i, j: (i, 0, 0, 0)),
                   pl.BlockSpec((nb, 1, SHIFT_COLS), lambda i, j: (i, 0, 0))],
        out_shape=[jax.ShapeDtypeStruct((bsz, t, RW_W), F32),
                   jax.ShapeDtypeStruct((bsz, RW_HEADS, HEAD_DIM, HEAD_DIM), F32),
                   jax.ShapeDtypeStruct((bsz, 1, SHIFT_COLS), F32)],
        scratch_shapes=[pltpu.VMEM((nb, tc_len + 8, SHIFT_COLS), F32),
                        pltpu.VMEM((nb, 1, SHIFT_COLS), F32),
                        pltpu.VMEM((nb, RW_HEADS // 2, HEAD_DIM, LANES), F32)] + [scr(RW_W)] * 9,
        compiler_params=_cparams(("arbitrary", "arbitrary")),
        name="rwkv",
    )(p, shift_prev, wkv_prev, row(lw["shift_mu"]), row(lw["rw_w0"]), lora, row(lw["rw_a0"]), lw["rw_g_up"],
      row(lw["rw_k_k"]), row(lw["rw_k_a"]), row(lw["rw_r_k"]), row(lw["rw_gn_g"]), row(lw["rw_gn_b"]), ones2, diag)
    return out, sfin, shift


def _ordinal_to_score(o):
    o = jnp.maximum(o, ORD_NEG_INF)
    bits = jnp.where(o >= 0, o, o ^ jnp.int32(0x7FFFFFFF))
    return lax.bitcast_convert_type(bits, F32)


def _select_params(score_ref, nblk, blk, topk, nq, pos_bits):
    kf = jnp.float32(topk)

    def count(pred):
        def one(j, acc):
            off = pl.multiple_of(j * blk, blk)
            kb = score_ref[pl.ds(off, blk), :]
            pos = off + lax.broadcasted_iota(I32, (blk, nq), 0)
            hit = pred(kb, pos).astype(F32)
            return acc + jnp.sum(hit.reshape(blk // SUBLANES, SUBLANES, nq), axis=0)

        acc = lax.fori_loop(0, nblk // 2, lambda jj, acc: one(2 * jj + 1, one(2 * jj, acc)),
                            jnp.zeros((SUBLANES, nq), F32))
        acc = lax.cond(nblk % 2 == 1, lambda acc: one(nblk - 1, acc), lambda acc: acc, acc)
        return jnp.sum(acc, axis=0, keepdims=True)

    zero = jnp.zeros((1, nq), I32)
    c0 = count(lambda kb, pos: kb >= 0.0)
    ordinal = jnp.where(c0 >= kf, zero, jnp.full((1, nq), INT_MIN, I32))
    def bit_body(i, ordinal):
        cand = ordinal | jnp.left_shift(jnp.int32(1), jnp.int32(30) - i)
        cand_score = _ordinal_to_score(cand)
        c = count(lambda kb, pos: kb >= cand_score)
        return jnp.where(c >= kf, cand, ordinal)

    ordinal = lax.fori_loop(0, 31, bit_body, ordinal)
    theta = _ordinal_to_score(ordinal)
    big = jnp.full((1, nq), 2 ** 30, I32)

    def with_ties():
        n_gt = count(lambda kb, pos: kb > theta)
        n_eq = count(lambda kb, pos: kb == theta)
        need = kf - n_gt

        def tie_cut():
            def pbody(i, cut):
                cand = cut | jnp.left_shift(jnp.int32(1), jnp.int32(pos_bits - 1) - i)
                c = count(lambda kb, pos: (kb == theta) & (pos < cand))
                return jnp.where(c < need, cand, cut)
            return lax.fori_loop(0, pos_bits, pbody, jnp.zeros((1, nq), I32))

        return lax.cond(jnp.max(n_eq - need) > 0.0, tie_cut, lambda: big)

    return theta, with_ties()


def _t5_bucket(dist):
    n = jnp.maximum(dist, 0)
    max_exact = NUM_BUCKETS // 2
    nf = jnp.maximum(n, 1).astype(F32)
    large = max_exact + jnp.floor(jnp.log(nf / max_exact) / math.log(MAX_DISTANCE / max_exact)
                                  * (NUM_BUCKETS - max_exact)).astype(I32)
    large = jnp.minimum(large, NUM_BUCKETS - 1)
    return jnp.where(n < max_exact, n, large)


def _near_bias_kernel(rb_ref, o_ref, *, tq):
    c = lax.broadcasted_iota(I32, (2 * tq, 2 * tq), 0)
    l = lax.broadcasted_iota(I32, (2 * tq, 2 * tq), 1)
    first = l < tq
    bucket = _t5_bucket(jnp.where(first, l, l - tq) - c + tq)
    for p in range(ATT_HEADS // 2):
        acc = jnp.zeros((2 * tq, 2 * tq), F32)
        for bk in range(NUM_BUCKETS):
            val = jnp.where(first, rb_ref[bk, 2 * p], rb_ref[bk, 2 * p + 1])
            acc = jnp.where(bucket == bk, val, acc)
        o_ref[p] = acc


def _near_bias(rel_bias, tq):
    return pl.pallas_call(
        functools.partial(_near_bias_kernel, tq=tq),
        in_specs=[pl.BlockSpec(memory_space=pltpu.SMEM)],
        out_specs=pl.BlockSpec(memory_space=pltpu.VMEM),
        out_shape=jax.ShapeDtypeStruct((ATT_HEADS // 2, 2 * tq, 2 * tq), F32),
        compiler_params=pltpu.CompilerParams(vmem_limit_bytes=VMEM_LIMIT),
        name="near_bias",
    )(rel_bias)


def _attn_prompt_kernel(rb_ref, qt_ref, qit_ref, wit_ref, kb_ref, vbt_ref, kdb_ref, near_ref, o_ref,
                        key_ref, qbd_ref, qibd_ref, m_ref, l_ref, acc_ref, *, tq, topk, pos_bits):
    i = pl.program_id(1)
    npair = ATT_HEADS // 2
    upper = lax.broadcasted_iota(I32, (LANES, tq), 0) < HEAD_DIM

    def stack_pair(x):
        zero = jnp.zeros_like(x)
        return jnp.concatenate([jnp.where(upper, x, zero), jnp.where(upper, zero, x)], axis=1)

    for p in range(npair):
        qbd_ref[p] = stack_pair(qt_ref[0, p * LANES:(p + 1) * LANES, :])
    for p in range(IDX_HEADS // 2):
        qibd_ref[p] = stack_pair(qit_ref[0, p * LANES:(p + 1) * LANES, :])
    wrow = wit_ref[0] * (IDX_HEADS ** -0.5 * IDX_DIM ** -0.5)

    kpos = lax.broadcasted_iota(I32, (tq, tq), 0)
    qpos = i * tq + lax.broadcasted_iota(I32, (tq, tq), 1)

    def idx_body(j, carry):
        off = pl.multiple_of(j * tq, tq)
        kd = kdb_ref[0, pl.ds(off, tq), :]
        idx = jnp.zeros((tq, tq), F32)
        for p in range(IDX_HEADS // 2):
            sc = jnp.maximum(jnp.dot(kd, qibd_ref[p], preferred_element_type=F32), 0.0)
            idx = idx + sc[:, 0:tq] * wrow[2 * p:2 * p + 1, :] + sc[:, tq:2 * tq] * wrow[2 * p + 1:2 * p + 2, :]
        idx = jnp.where(idx == 0.0, 0.0, idx)
        key_ref[pl.ds(off, tq), :] = jnp.where(off + kpos <= qpos, idx, -jnp.inf)
        return carry

    lax.fori_loop(0, i + 1, idx_body, 0)

    theta, cut = _select_params(key_ref, i + 1, tq, topk, tq, pos_bits)

    m_ref[...] = jnp.full(m_ref.shape, NEG_BIG, F32)
    l_ref[...] = jnp.zeros(l_ref.shape, F32)
    acc_ref[...] = jnp.zeros(acc_ref.shape, F32)
    first = lax.broadcasted_iota(I32, (1, 2 * tq), 1) < tq
    ones_rows = jnp.ones((2 * SUBLANES, tq), BF16)

    def attend(j, near_rows):
        off = pl.multiple_of(j * tq, tq)
        kb = key_ref[pl.ds(off, tq), :]
        pos = off + kpos
        sel = ((kb > theta) | ((kb == theta) & (pos <= cut))) & (pos <= qpos)
        mb = jnp.where(sel, 0.0, -jnp.inf)
        mb = jnp.concatenate([mb, mb], axis=1)
        ss = [jnp.dot(kb_ref[0, pl.ds(off, tq), p * LANES:(p + 1) * LANES], qbd_ref[p], preferred_element_type=F32)
              for p in range(npair)]
        prs, alphas = [], []
        for p in range(npair):
            if near_rows is None:
                logits = ss[p] + mb
                const = jnp.where(first, rb_ref[NUM_BUCKETS - 1, 2 * p], rb_ref[NUM_BUCKETS - 1, 2 * p + 1])
            else:
                logits = ss[p] + (near_ref[p, near_rows:near_rows + tq, :] + mb)
                const = jnp.zeros((1, 2 * tq), F32)
            m_old = m_ref[p]
            m_new = jnp.maximum(m_old, jnp.max(logits, axis=0, keepdims=True) + const)
            alphas.append(jnp.exp(m_old - m_new))
            prs.append(jnp.exp(logits - (m_new[0:1, :] - const)).astype(BF16))
            m_ref[p] = m_new
        for p in range(npair):
            vt = jnp.concatenate([vbt_ref[0, j, p * LANES:(p + 1) * LANES, :], ones_rows], axis=0)
            pv = jnp.dot(vt, prs[p], preferred_element_type=F32)
            l_ref[p] = alphas[p] * l_ref[p] + pv[LANES:LANES + SUBLANES, :]
            acc_ref[p] = alphas[p][0:1, :] * acc_ref[p] + pv[0:LANES, :]

    def far_body(j, carry):
        attend(j, None)
        return carry

    lax.fori_loop(0, jnp.maximum(i - 1, 0), far_body, 0)

    @pl.when(i >= 1)
    def _():
        attend(i - 1, 0)

    attend(i, tq)

    for p in range(npair):
        o = acc_ref[p] / l_ref[p][0:1, :]
        ot = jnp.concatenate([o[0:HEAD_DIM, 0:tq], o[HEAD_DIM:LANES, tq:2 * tq]], axis=0)
        o_ref[0, :, p * LANES:(p + 1) * LANES] = ot.T


def _attn_prompt(qt, qit, wit, kb, vbt, kdb, rel_bias, bsz, t, tq):
    topk = min(MAX_TOPK, t // 4)
    pos_bits = max(1, (t - 1).bit_length())
    near = _near_bias(rel_bias, tq)
    npair = ATT_HEADS // 2
    nblk = t // tq
    qtile_t = lambda rows: pl.BlockSpec((1, rows, tq), lambda b, i: (b, 0, i))
    full = lambda w: pl.BlockSpec((1, t, w), lambda b, i: (b, 0, 0))
    return pl.pallas_call(
        functools.partial(_attn_prompt_kernel, tq=tq, topk=topk, pos_bits=pos_bits),
        grid=(bsz, nblk),
        in_specs=[pl.BlockSpec(memory_space=pltpu.SMEM),
                  qtile_t(ATT_W), qtile_t(IDX_HEADS * IDX_DIM), qtile_t(SUBLANES),
                  full(ATT_W), pl.BlockSpec((1, nblk, ATT_W, tq), lambda b, i: (b, 0, 0, 0)), full(LANES),
                  _const_spec((npair, 2 * tq, 2 * tq))],
        out_specs=pl.BlockSpec((1, tq, ATT_W), lambda b, i: (b, i, 0)),
        out_shape=jax.ShapeDtypeStruct((bsz, t, ATT_W), F32),
        scratch_shapes=[pltpu.VMEM((t, tq), F32),
                        pltpu.VMEM((npair, LANES, 2 * tq), BF16),
                        pltpu.VMEM((IDX_HEADS // 2, LANES, 2 * tq), BF16),
                        pltpu.VMEM((npair, SUBLANES, 2 * tq), F32),
                        pltpu.VMEM((npair, SUBLANES, 2 * tq), F32),
                        pltpu.VMEM((npair, LANES, 2 * tq), F32)],
        compiler_params=_cparams(("arbitrary", "arbitrary")),
        name="attn_prompt",
    )(rel_bias, qt, qit, wit, kb.reshape(bsz, t, ATT_W), vbt, kdb.reshape(bsz, t, LANES), near)


def _sample_index_kernel(pt_ref, qi_ref, wi_ref, *rest, n_pages, ts):
    page_refs = rest[:n_pages + 1]
    o_ref = rest[n_pages + 1]
    qi = qi_ref[0]
    q16 = jnp.concatenate([qi[:, h * IDX_DIM:(h + 1) * IDX_DIM] for h in range(IDX_HEADS)], axis=0).astype(BF16)
    wi = wi_ref[0]
    scale = IDX_HEADS ** -0.5 * IDX_DIM ** -0.5
    w16 = jnp.concatenate([wi[:, h:h + 1] for h in range(IDX_HEADS)], axis=0) * scale
    trow = lax.broadcasted_iota(I32, (ts, PAGE_SIZE), 0)
    ncol = lax.broadcasted_iota(I32, (ts, PAGE_SIZE), 1)
    for j in range(n_pages + 1):
        kp = page_refs[j][0].astype(BF16)
        if j < n_pages:
            sc = jnp.dot(q16, kp, preferred_element_type=F32)
        else:
            sc = lax.dot_general(q16, kp, (((1,), (1,)), ((), ())), preferred_element_type=F32)
        rw = jnp.maximum(sc, 0.0) * w16
        idx = rw[0:ts]
        for h in range(1, IDX_HEADS):
            idx = idx + rw[h * ts:(h + 1) * ts]
        if j == n_pages:
            idx = jnp.where(ncol <= trow, idx, -jnp.inf)
        o_ref[0, :, j * PAGE_SIZE:(j + 1) * PAGE_SIZE] = idx


def _sample_select_kernel(idx_ref, sel_ref, key_ref, *, topk, pos_bits):
    lp, nq = idx_ref.shape
    idx = idx_ref[...]
    key_ref[...] = jnp.where(idx == 0.0, 0.0, idx)
    theta, cut = _select_params(key_ref, 1, lp, topk, nq, pos_bits)
    kb = key_ref[...]
    pos = lax.broadcasted_iota(I32, (lp, nq), 0)
    sel = ((kb > theta) | ((kb == theta) & (pos <= cut))) & (idx_ref[...] > -jnp.inf)
    sel_ref[...] = sel.astype(F32)


def _sample_bias_kernel(rb_ref, near_ref, new_ref, far_ref, *, ts):
    rows = ts * ATT_HEADS
    r = lax.broadcasted_iota(I32, (rows, PAGE_SIZE), 0)
    c = lax.broadcasted_iota(I32, (rows, PAGE_SIZE), 1)
    t = r // ATT_HEADS
    rhead = r % ATT_HEADS

    def gather(bucket):
        acc = jnp.zeros((rows, PAGE_SIZE), F32)
        for h in range(ATT_HEADS):
            for bk in range(NUM_BUCKETS):
                acc = jnp.where((bucket == bk) & (rhead == h), rb_ref[bk, h], acc)
        return acc

    near_ref[...] = gather(_t5_bucket(PAGE_SIZE + t - c))
    ok = (c <= t) & (c < ts)
    new_ref[...] = jnp.where(ok, gather(_t5_bucket(t - c)), -jnp.inf)
    far_ref[...] = gather(jnp.full((rows, PAGE_SIZE), NUM_BUCKETS - 1, I32))


def _sample_attn_kernel(pt_ref, q_ref, sel_ref, near_ref, new_ref, far_ref, *rest, n_pages, ts):
    k_refs = rest[:n_pages + 1]
    v_refs = rest[n_pages + 1:2 * n_pages + 2]
    o_ref = rest[2 * n_pages + 2]
    l_ref = rest[2 * n_pages + 3]
    rows = ts * ATT_HEADS
    rep = lambda a: jnp.concatenate([jnp.broadcast_to(a[t:t + 1], (ATT_HEADS, a.shape[1])) for t in range(ts)], axis=0)
    r = lax.broadcasted_iota(I32, (rows, ATT_W), 0)
    c = lax.broadcasted_iota(I32, (rows, ATT_W), 1)
    own = (r % ATT_HEADS) == (c // HEAD_DIM)
    q = rep(q_ref[0].astype(F32) * HEAD_DIM ** -0.5)
    q = jnp.where(own, q, 0.0).astype(BF16)
    sel = rep(sel_ref[0]) > 0.5

    for j in range(n_pages):
        kt = k_refs[j][0].reshape(ATT_W, PAGE_SIZE).astype(BF16)
        s = jnp.dot(q, kt, preferred_element_type=F32)
        bias = near_ref[...] if j == n_pages - 1 else far_ref[...]
        sl = slice(j * PAGE_SIZE, (j + 1) * PAGE_SIZE)
        l_ref[:, sl] = jnp.where(sel[:, sl], s + bias, -jnp.inf)
    kn = k_refs[n_pages][0].astype(BF16)
    s = lax.dot_general(q, kn, (((1,), (1,)), ((), ())), preferred_element_type=F32)
    sl = slice(n_pages * PAGE_SIZE, (n_pages + 1) * PAGE_SIZE)
    l_ref[:, sl] = jnp.where(sel[:, sl], s + new_ref[...], -jnp.inf)

    logits = l_ref[...]
    m = jnp.max(logits, axis=1, keepdims=True)
    pr = jnp.exp(logits - m)
    den = jnp.sum(pr, axis=1, keepdims=True)
    prb = pr.astype(BF16)
    acc = jnp.dot(prb[:, sl], v_refs[n_pages][0].astype(BF16), preferred_element_type=F32)
    for j in range(n_pages):
        vt = v_refs[j][0].reshape(ATT_W, PAGE_SIZE).astype(BF16)
        acc = acc + lax.dot_general(prb[:, j * PAGE_SIZE:(j + 1) * PAGE_SIZE], vt, (((1,), (1,)), ((), ())),
                                    preferred_element_type=F32)
    acc = jnp.where(own, acc / den, 0.0)
    o_ref[0] = jnp.sum(acc.reshape(ts, ATT_HEADS, ATT_W), axis=1)


def _attn_sample(qb, qi, wi, k_new, v_new, kd_new, cache_k, cache_v, cache_kidx, page_table, rel_bias, bsz, ts):
    n_pages = page_table.shape[1]
    past_len = n_pages * PAGE_SIZE
    topk = min(MAX_TOPK, (past_len + ts) // 4)
    lp = (n_pages + 1) * PAGE_SIZE
    rows = ts * ATT_HEADS
    pt = page_table.reshape(-1).astype(I32)

    ki_pool = jnp.swapaxes(cache_kidx, 1, 2)
    ki_new = jnp.zeros((bsz, PAGE_SIZE, IDX_DIM), F32).at[:, 0:ts].set(kd_new.reshape(bsz, ts, LANES)[..., 0:IDX_DIM])
    page_spec = lambda j, shp: pl.BlockSpec((1,) + shp, lambda b, ptr: (ptr[b * n_pages + j],) + (0,) * len(shp))
    per_b = lambda shp: pl.BlockSpec((1,) + shp, lambda b, ptr: (b,) + (0,) * len(shp))
    idx = pl.pallas_call(
        functools.partial(_sample_index_kernel, n_pages=n_pages, ts=ts),
        grid_spec=pltpu.PrefetchScalarGridSpec(
            num_scalar_prefetch=1, grid=(bsz,),
            in_specs=[per_b((ts, IDX_HEADS * IDX_DIM)), per_b((ts, LANES))]
                     + [page_spec(j, (IDX_DIM, PAGE_SIZE)) for j in range(n_pages)]
                     + [per_b((PAGE_SIZE, IDX_DIM))],
            out_specs=per_b((ts, lp))),
        out_shape=jax.ShapeDtypeStruct((bsz, ts, lp), F32),
        compiler_params=_cparams(("arbitrary",)),
        name="sample_index",
    )(pt, qi.reshape(bsz, ts, -1), wi.reshape(bsz, ts, LANES), *([ki_pool] * n_pages), ki_new)

    sel = pl.pallas_call(
        functools.partial(_sample_select_kernel, topk=topk, pos_bits=max(1, (lp - 1).bit_length())),
        out_shape=jax.ShapeDtypeStruct((lp, bsz * ts), F32),
        scratch_shapes=[pltpu.VMEM((lp, bsz * ts), F32)],
        compiler_params=pltpu.CompilerParams(vmem_limit_bytes=VMEM_LIMIT),
        name="sample_select",
    )(idx.reshape(bsz * ts, lp).T)
    sel = sel.T.reshape(bsz, ts, lp)

    near, newb, far = pl.pallas_call(
        functools.partial(_sample_bias_kernel, ts=ts),
        in_specs=[pl.BlockSpec(memory_space=pltpu.SMEM)],
        out_shape=[jax.ShapeDtypeStruct((rows, PAGE_SIZE), F32)] * 3,
        compiler_params=pltpu.CompilerParams(vmem_limit_bytes=VMEM_LIMIT),
        name="sample_bias",
    )(rel_bias)

    page4 = (ATT_HEADS, HEAD_DIM, PAGE_SIZE)
    k_pool = jnp.transpose(cache_k, (0, 2, 3, 1))
    v_pool = jnp.transpose(cache_v, (0, 2, 3, 1))
    pad_new = lambda a: jnp.zeros((bsz, PAGE_SIZE, ATT_W), F32).at[:, 0:ts].set(a.reshape(bsz, ts, ATT_W))
    const = lambda shp: pl.BlockSpec(shp, lambda b, ptr: (0,) * len(shp))
    out = pl.pallas_call(
        functools.partial(_sample_attn_kernel, n_pages=n_pages, ts=ts),
        grid_spec=pltpu.PrefetchScalarGridSpec(
            num_scalar_prefetch=1, grid=(bsz,),
            in_specs=[per_b((ts, ATT_W)), per_b((ts, lp)),
                      const((rows, PAGE_SIZE)), const((rows, PAGE_SIZE)), const((rows, PAGE_SIZE))]
                     + [page_spec(j, page4) for j in range(n_pages)] + [per_b((PAGE_SIZE, ATT_W))]
                     + [page_spec(j, page4) for j in range(n_pages)] + [per_b((PAGE_SIZE, ATT_W))],
            out_specs=per_b((ts, ATT_W)),
            scratch_shapes=[pltpu.VMEM((rows, lp), F32)]),
        out_shape=jax.ShapeDtypeStruct((bsz, ts, ATT_W), F32),
        compiler_params=_cparams(("arbitrary",)),
        name="sample_attn",
    )(pt, qb.reshape(bsz, ts, ATT_W), sel, near, newb, far,
      *([k_pool] * n_pages), pad_new(k_new), *([v_pool] * n_pages), pad_new(v_new))
    return out.reshape(bsz * ts, ATT_W)


def _layer_norm(x, g, b):
    mu = jnp.mean(x, axis=-1, keepdims=True)
    xc = x - mu
    var = jnp.mean(xc * xc, axis=-1, keepdims=True)
    return xc * lax.rsqrt(var + LN_EPS) * g + b


def _ffn_kernel(x_ref, rw_ref, att_ref, prev_ref, wo_ref, g1_ref, b1_ref, wup_ref, cw_ref, cb_ref, wdn_ref,
                g2_ref, b2_ref, y_ref, nc_ref, pad_ref, *, tm, shift, off, alpha):
    ti = pl.program_id(1)
    data = off + 2 * shift

    @pl.when(ti == 0)
    def _():
        pad_ref[off:data, :] = prev_ref[0]

    x = x_ref[0]
    mix = (jnp.dot(rw_ref[0].astype(BF16), wo_ref[0:RW_W, :], preferred_element_type=F32)
           + jnp.dot(att_ref[0].astype(BF16), wo_ref[RW_W:RW_W + ATT_W, :], preferred_element_type=F32))
    x1 = _layer_norm(alpha * x + mix, g1_ref[...], b1_ref[...])
    x1b = x1.astype(BF16)
    u_conv = jnp.dot(x1b, wup_ref[:, 0:D_FF], preferred_element_type=F32)
    u_lin = jnp.dot(x1b, wup_ref[:, D_FF:2 * D_FF], preferred_element_type=F32)
    pad_ref[data:data + tm, :] = u_conv
    conv = (pad_ref[off:off + tm, :] * cw_ref[0:1, :] + pad_ref[off + shift:off + shift + tm, :] * cw_ref[1:2, :]
            + u_conv * cw_ref[2:3, :] + cb_ref[...])
    h = conv * _sigmoid(conv) * u_lin
    down = jnp.dot(h.astype(BF16), wdn_ref[...], preferred_element_type=F32)
    y_ref[0] = _layer_norm(alpha * x1 + down, g2_ref[...], b2_ref[...])
    tail = pad_ref[off + tm:data + tm, :]
    pad_ref[off:data, :] = tail
    nc_ref[0] = tail


def _ffn(x, rw, att, prev, lw, alpha, shift):
    nb, rows, _ = x.shape
    tm = min(256, rows) if shift == 1 else rows
    off = 6 if shift == 1 else 0
    row = lambda a: a.reshape(1, -1)
    tile = lambda w: pl.BlockSpec((1, tm, w), lambda b, i: (b, i, 0))
    y, nc = pl.pallas_call(
        functools.partial(_ffn_kernel, tm=tm, shift=shift, off=off, alpha=alpha),
        grid=(nb, rows // tm),
        in_specs=[tile(D_MODEL), tile(RW_W), tile(ATT_W),
                  pl.BlockSpec((1, 2 * shift, D_FF), lambda b, i: (b, 0, 0)),
                  _const_spec((D_MODEL, D_MODEL)), _const_spec((1, D_MODEL)), _const_spec((1, D_MODEL)),
                  _const_spec((D_MODEL, 2 * D_FF)), _const_spec((CONV_W, D_FF)), _const_spec((1, D_FF)),
                  _const_spec((D_FF, D_MODEL)), _const_spec((1, D_MODEL)), _const_spec((1, D_MODEL))],
        out_specs=[tile(D_MODEL), pl.BlockSpec((1, 2 * shift, D_FF), lambda b, i: (b, 0, 0))],
        out_shape=[jax.ShapeDtypeStruct((nb, rows, D_MODEL), F32), jax.ShapeDtypeStruct((nb, 2 * shift, D_FF), F32)],
        scratch_shapes=[pltpu.VMEM((off + 2 * shift + tm, D_FF), F32)],
        compiler_params=_cparams(("arbitrary", "arbitrary")),
        name="ffn",
    )(x, rw, att, prev, lw["w_out"].astype(BF16), row(lw["ln1_g"]), row(lw["ln1_b"]),
      lw["ffn_w_up"].astype(BF16), lw["ffn_conv_w"], row(lw["ffn_conv_b"]), lw["ffn_w_down"].astype(BF16),
      row(lw["ln2_g"]), row(lw["ln2_b"]))
    return y, nc


def _layer(x, shift_prev, wkv_prev, conv_prev, past, rel_bias, lw, alpha):
    bsz, t, _ = x.shape
    w_in = lw["w_in"]
    w_ki = w_in[:, C_KI:C_KI + IDX_DIM]
    w_wi = w_in[:, C_KI + IDX_DIM:]
    wp = jnp.concatenate([w_in[:, 0:C_KI], w_ki, w_ki, w_wi,
                          jnp.zeros((D_MODEL, LANES - IDX_HEADS), F32)], axis=1).astype(BF16)
    x2d = x.reshape(bsz * t, D_MODEL)
    if past is None:
        tq = min(256, t)
        prw, qt, kb, kt, vt, vbt, qit, kdb, kit, wit = _proj_t(x2d, wp, bsz, t, tq)
        rw_out, new_wkv, new_shift = _rwkv(prw.reshape(bsz, t, SHIFT_COLS), shift_prev, wkv_prev, lw)
        att = _attn_prompt(qt, qit, wit, kb, vbt, kdb, rel_bias, bsz, t, tq)
        y, new_conv = _ffn(x, rw_out, att, conv_prev, lw, alpha, shift=1)
        heads = lambda a: jnp.transpose(a.reshape(bsz, ATT_HEADS, HEAD_DIM, t), (0, 3, 1, 2))
        return y, (heads(kt), heads(vt), jnp.swapaxes(kit, 1, 2), new_wkv, new_shift, new_conv)
    prw, qb, k, v, qi, kd, wi = _proj(x2d, wp)
    rw_out, new_wkv, new_shift = _rwkv(prw.reshape(bsz, t, SHIFT_COLS), shift_prev, wkv_prev, lw)
    att = _attn_sample(qb, qi, wi, k, v, kd, *past, rel_bias, bsz, t)
    tmaj = lambda a: jnp.swapaxes(a.reshape(bsz, t, -1), 0, 1).reshape(1, t * bsz, -1)
    prev = jnp.swapaxes(conv_prev, 0, 1).reshape(1, (CONV_W - 1) * bsz, D_FF)
    y, new_conv = _ffn(tmaj(x), tmaj(rw_out), tmaj(att), prev, lw, alpha, shift=bsz)
    y = jnp.swapaxes(y.reshape(t, bsz, D_MODEL), 0, 1)
    new_conv = jnp.swapaxes(new_conv.reshape(CONV_W - 1, bsz, D_FF), 0, 1)
    state = (k.reshape(bsz, t, ATT_HEADS, HEAD_DIM), v.reshape(bsz, t, ATT_HEADS, HEAD_DIM),
             kd.reshape(bsz, t, LANES)[..., 0:IDX_DIM], new_wkv, new_shift, new_conv)
    return y, state


_LAYER_WEIGHTS = ("w_in", "shift_mu", "rw_w0", "rw_w_up", "rw_a0", "rw_a_up", "rw_g_up", "rw_k_k", "rw_k_a",
                  "rw_r_k", "rw_gn_g", "rw_gn_b", "w_out", "ln1_g", "ln1_b", "ffn_w_up", "ffn_conv_w",
                  "ffn_conv_b", "ffn_w_down", "ln2_g", "ln2_b")


def kernel(x_prompt, x_sample, cache_k, cache_v, cache_kidx, page_table, state_wkv, state_shift, state_conv,
           rel_bias, w_in, shift_mu, rw_w0, rw_w_up, rw_a0, rw_a_up, rw_g_up, rw_k_k, rw_k_a, rw_r_k, rw_gn_g,
           rw_gn_b, w_out, ln1_g, ln1_b, ffn_w_up, ffn_conv_w, ffn_conv_b, ffn_w_down, ln2_g, ln2_b):
    stacked = dict(zip(_LAYER_WEIGHTS, (w_in, shift_mu, rw_w0, rw_w_up, rw_a0, rw_a_up, rw_g_up, rw_k_k, rw_k_a,
                                        rw_r_k, rw_gn_g, rw_gn_b, w_out, ln1_g, ln1_b, ffn_w_up, ffn_conv_w,
                                        ffn_conv_b, ffn_w_down, ln2_g, ln2_b)))
    depth = w_in.shape[0]
    alpha = (2 * depth) ** 0.25
    bp = x_prompt.shape[0]
    yp, ys = x_prompt, x_sample
    outs_p, outs_s = [], []
    for l in range(depth):
        lw = {name: w[l] for name, w in stacked.items()}
        yp, st_p = _layer(yp, jnp.zeros((bp, 1, SHIFT_COLS), F32),
                          jnp.zeros((bp, RW_HEADS, HEAD_DIM, HEAD_DIM), F32),
                          jnp.zeros((bp, CONV_W - 1, D_FF), F32), None, rel_bias, lw, alpha)
        ys, st_s = _layer(ys, state_shift[l], state_wkv[l], state_conv[l],
                          (cache_k[l], cache_v[l], cache_kidx[l], page_table), rel_bias, lw, alpha)
        outs_p.append(st_p)
        outs_s.append(st_s)
    st = lambda outs, i: jnp.stack([o[i] for o in outs])
    return (yp, ys) + tuple(st(outs_p, i) for i in range(6)) + tuple(st(outs_s, i) for i in range(6))
```

```python
import functools
import math

import jax
import jax.numpy as jnp
from jax import lax
from jax.experimental import pallas as pl
from jax.experimental.pallas import tpu as pltpu

F32 = jnp.float32
BF16 = jnp.bfloat16
I32 = jnp.int32

D_MODEL = 1024
HEAD_DIM = 64
RW_W = 512
ATT_W = 512
RW_HEADS = 8
ATT_HEADS = 8
IDX_HEADS = 4
IDX_DIM = 64
MAX_TOPK = 256
W_LORA = 64
A_LORA = 64
G_LORA = 128
GN_EPS = 64e-5
LN_EPS = 1e-5
D_FF = 2816
CONV_W = 3
NUM_BUCKETS = 32
MAX_DISTANCE = 128
PAGE_SIZE = 128
SHIFT_COLS = 3 * RW_W + W_LORA + A_LORA + G_LORA
LANES = 128
SUBLANES = 8
C_Q = SHIFT_COLS
C_K = C_Q + ATT_W
C_V = C_K + ATT_W
C_QI = C_V + ATT_W
C_KI = C_QI + IDX_HEADS * IDX_DIM
C_WI = C_KI + LANES
C_END = C_WI + LANES
VMEM_LIMIT = 56 * 1024 * 1024
INT_MIN = -(2 ** 31)
ORD_NEG_INF = INT_MIN + 0x7FFFFF
NEG_BIG = -1e30


def _cparams(sem):
    return pltpu.CompilerParams(dimension_semantics=sem, vmem_limit_bytes=VMEM_LIMIT)


def _const_spec(shape):
    zeros = (0,) * len(shape)
    return pl.BlockSpec(shape, lambda *_: zeros, pipeline_mode=pl.Buffered(1))


def _proj_kernel(x_ref, w_ref, prw_ref, qb_ref, k_ref, v_ref, qi_ref, kd_ref, wi_ref):
    x = x_ref[...].astype(BF16)

    def mm(lo, hi):
        return jnp.dot(x, w_ref[:, lo:hi], preferred_element_type=F32)

    prw_ref[...] = mm(0, C_Q)
    qb_ref[...] = mm(C_Q, C_K).astype(BF16)
    k_ref[...] = mm(C_K, C_V)
    v_ref[...] = mm(C_V, C_QI)
    qi_ref[...] = mm(C_QI, C_KI)
    kd_ref[...] = mm(C_KI, C_WI)
    wi_ref[...] = mm(C_WI, C_END)


def _proj(x2d, wp):
    m = x2d.shape[0]
    tm = min(256, m)
    widths = [(C_Q, F32), (ATT_W, BF16), (ATT_W, F32), (ATT_W, F32), (IDX_HEADS * IDX_DIM, F32), (LANES, F32),
              (LANES, F32)]
    return pl.pallas_call(
        _proj_kernel,
        grid=(m // tm,),
        in_specs=[pl.BlockSpec((tm, D_MODEL), lambda i: (i, 0)), _const_spec((D_MODEL, C_END))],
        out_specs=[pl.BlockSpec((tm, w), lambda i: (i, 0)) for w, _ in widths],
        out_shape=[jax.ShapeDtypeStruct((m, w), dt) for w, dt in widths],
        compiler_params=_cparams(("arbitrary",)),
        name="proj",
    )(x2d, wp)


def _proj_t_kernel(x_ref, w_ref, wt_ref, prw_ref, qt_ref, kb_ref, kt_ref, vt_ref, vbt_ref, qit_ref, kdb_ref, kit_ref,
                   wit_ref):
    x = x_ref[...].astype(BF16)

    def mm(lo, hi):
        return jnp.dot(x, w_ref[:, lo:hi], preferred_element_type=F32)

    def mt(lo, hi):
        return lax.dot_general(wt_ref[lo:hi, :], x, (((1,), (1,)), ((), ())), preferred_element_type=F32)

    prw_ref[...] = mm(0, C_Q)
    kb_ref[...] = mm(C_K, C_V).astype(BF16)
    kdb_ref[...] = mm(C_KI, C_WI).astype(BF16)
    qt_ref[0] = (mt(C_Q, C_K) * HEAD_DIM ** -0.5).astype(BF16)
    kt_ref[0] = mt(C_K, C_V)
    vt = mt(C_V, C_QI)
    vt_ref[0] = vt
    vbt_ref[0, 0] = vt.astype(BF16)
    qit_ref[0] = mt(C_QI, C_KI).astype(BF16)
    kit_ref[0] = mt(C_KI, C_KI + IDX_DIM)
    wit_ref[0] = mt(C_WI, C_WI + SUBLANES)


def _proj_t(x2d, wp, bsz, t, tm):
    m = x2d.shape[0]
    nblk = t // tm
    rows = lambda w: pl.BlockSpec((tm, w), lambda b, i: (b * nblk + i, 0))
    cols = lambda r: pl.BlockSpec((1, r, tm), lambda b, i: (b, 0, i))
    tshape = lambda r, dt: jax.ShapeDtypeStruct((bsz, r, t), dt)
    return pl.pallas_call(
        _proj_t_kernel,
        grid=(bsz, nblk),
        in_specs=[rows(D_MODEL), _const_spec((D_MODEL, C_END)), _const_spec((C_END, D_MODEL))],
        out_specs=[rows(C_Q), cols(ATT_W), rows(ATT_W), cols(ATT_W), cols(ATT_W),
                   pl.BlockSpec((1, 1, ATT_W, tm), lambda b, i: (b, i, 0, 0)),
                   cols(IDX_HEADS * IDX_DIM), rows(LANES), cols(IDX_DIM), cols(SUBLANES)],
        out_shape=[jax.ShapeDtypeStruct((m, C_Q), F32), tshape(ATT_W, BF16), jax.ShapeDtypeStruct((m, ATT_W), BF16),
                   tshape(ATT_W, F32), tshape(ATT_W, F32), jax.ShapeDtypeStruct((bsz, nblk, ATT_W, tm), BF16),
                   tshape(IDX_HEADS * IDX_DIM, BF16), jax.ShapeDtypeStruct((m, LANES), BF16),
                   tshape(IDX_DIM, F32), tshape(SUBLANES, F32)],
        compiler_params=_cparams(("arbitrary", "arbitrary")),
        name="proj_t",
    )(x2d, wp, wp.T)


def _softplus(z):
    return jnp.maximum(z, 0.0) + jnp.log1p(jnp.exp(-jnp.abs(z)))


def _sigmoid(z):
    return 1.0 / (1.0 + jnp.exp(-z))


def _seg_sum(x, ones_bd):
    hi = x.astype(BF16)
    mid = (x - hi.astype(F32)).astype(BF16)
    parts = [jnp.dot(jnp.concatenate([hi[:, g * LANES:(g + 1) * LANES], mid[:, g * LANES:(g + 1) * LANES]], axis=1),
                     ones_bd, preferred_element_type=F32) for g in range(x.shape[1] // LANES)]
    return jnp.concatenate(parts, axis=1)


def _rwkv_kernel(p_ref, sp_ref, s0_ref, mu_ref, w0_ref, lw_ref, a0_ref, gup_ref, kk_ref, ka_ref, rk_ref,
                 gng_ref, gnb_ref, ones2_ref, diag_ref,
                 out_ref, sfin_ref, shift_ref,
                 pad_ref, carry_ref, s_ref, kkv_ref, dec_ref, b_ref, k2_ref, r_ref, v_ref, g_ref, bon_ref, y_ref,
                 *, nb, tc_len):
    tc = pl.program_id(1)
    last = pl.num_programs(1) - 1
    npair = RW_HEADS // 2

    @pl.when(tc == 0)
    def _():
        carry_ref[...] = sp_ref[...]
        for b in range(nb):
            for hp in range(npair):
                s_ref[b, hp] = jnp.concatenate([s0_ref[b, 2 * hp], s0_ref[b, 2 * hp + 1]], axis=1)

    rows = lax.broadcasted_iota(I32, (2 * LANES, LANES), 0) % LANES
    cols = lax.broadcasted_iota(I32, (2 * LANES, LANES), 1)
    ones_bd = (rows // HEAD_DIM == cols // HEAD_DIM).astype(BF16)
    lane = lax.broadcasted_iota(I32, (tc_len, LANES), 1)

    for b in range(nb):
        x = p_ref[b]
        pad_ref[b, 7:8, :] = carry_ref[b]
        pad_ref[b, 8:8 + tc_len, :] = x
        xm1 = pad_ref[b, 7:7 + tc_len, :]
        ps = x + (xm1 - x) * mu_ref[...]
        carry_ref[b] = x[tc_len - 1:tc_len, :]
        r = ps[:, 0:RW_W]
        k = ps[:, RW_W:2 * RW_W]
        v = ps[:, 2 * RW_W:3 * RW_W]
        o = 3 * RW_W
        z = ps[:, o:o + LANES]
        z = jnp.where(lane < W_LORA, jnp.tanh(z), z)
        lo = jnp.dot(z.astype(BF16), lw_ref[...], preferred_element_type=F32)
        w_log = -_softplus(-(w0_ref[...] + lo[:, 0:RW_W])) - 0.5
        dec = jnp.exp(-jnp.exp(w_log))
        a = _sigmoid(a0_ref[...] + lo[:, RW_W:2 * RW_W])
        gd = ps[:, o + LANES:o + 2 * LANES]
        g = jnp.dot(_sigmoid(gd).astype(BF16), gup_ref[...], preferred_element_type=F32)
        kkx = k * kk_ref[...]
        ss = _seg_sum(kkx * kkx, ones_bd)
        kkn = kkx * lax.rsqrt(jnp.maximum(ss, 1e-24))
        k2 = k * (1.0 + (a - 1.0) * ka_ref[...])
        bonus = _seg_sum(r * k2 * rk_ref[...], ones_bd) * v
        kkv_ref[b, 0:tc_len] = kkn
        dec_ref[b, 0:tc_len] = dec
        b_ref[b, 0:tc_len] = kkn * a
        k2_ref[b, 0:tc_len] = k2
        r_ref[b, 0:tc_len] = r
        v_ref[b, 0:tc_len] = v
        g_ref[b, 0:tc_len] = g
        bon_ref[b, 0:tc_len] = bonus

    diag = diag_ref[...]
    diag_b = diag.astype(BF16)
    drow = lax.broadcasted_iota(I32, (HEAD_DIM, LANES), 0)
    dlane = lax.broadcasted_iota(I32, (HEAD_DIM, LANES), 1)
    diag2_b = ((drow + 1) % HEAD_DIM == dlane % HEAD_DIM).astype(BF16)
    seg_start = lax.broadcasted_iota(I32, (SUBLANES, LANES), 1) % HEAD_DIM == 0

    sub = min(SUBLANES, tc_len)
    riota = lax.broadcasted_iota(I32, (SUBLANES, LANES), 0)

    tiles = [(b, hp) for b in range(nb) for hp in range(npair)]
    nt = len(tiles)

    def bsum(xs):
        lhs = jnp.concatenate([jnp.concatenate([xs[2 * m], xs[2 * m + 1]], axis=1) for m in range(nt // 2)], axis=0)
        red = jnp.dot(lhs, ones2_ref[...], preferred_element_type=F32)
        return [red[(k // 2) * HEAD_DIM:(k // 2 + 1) * HEAD_DIM, (k % 2) * LANES:(k % 2 + 1) * LANES]
                for k in range(nt)]

    def group(gi, carry):
        t0 = pl.multiple_of(gi * SUBLANES, SUBLANES)
        rows8 = [[ref[b, pl.ds(t0, SUBLANES), hp * LANES:(hp + 1) * LANES]
                  for ref in (kkv_ref, dec_ref, b_ref, k2_ref, r_ref, v_ref)] for b, hp in tiles]
        vhi8 = [rows8[k][5].astype(BF16).astype(F32) for k in range(nt)]
        vmid8 = [rows8[k][5] - vhi8[k] for k in range(nt)]
        vmid8 = [jnp.where(seg_start, pltpu.roll(x, LANES - HEAD_DIM + 1, 1), pltpu.roll(x, 1, 1)) for x in vmid8]
        s = [s_ref[b, hp] for b, hp in tiles]
        y8 = [jnp.zeros((SUBLANES, LANES), F32) for _ in tiles]
        tile_of = lambda row: jnp.broadcast_to(row, (HEAD_DIM, LANES)).astype(BF16)
        for i in range(sub):
            kkr, wr, br, kr, rr = [[rows8[k][q][i:i + 1, :] for k in range(nt)] for q in range(5)]
            sa = bsum([(s[k] * kkr[k]).astype(BF16) for k in range(nt)])
            vb = bsum([tile_of(vhi8[k][i:i + 1, :]) * diag_b + tile_of(vmid8[k][i:i + 1, :]) * diag2_b
                       for k in range(nt)])
            for k in range(nt):
                s[k] = s[k] * wr[k] - sa[k] * br[k] + vb[k] * kr[k]
            yb = bsum([(s[k] * rr[k]).astype(BF16) for k in range(nt)])
            for k in range(nt):
                yrow = jnp.sum(yb[k] * diag, axis=0, keepdims=True)
                y8[k] = jnp.where(riota == i, yrow, y8[k])
        for k, (b, hp) in enumerate(tiles):
            s_ref[b, hp] = s[k]
            y_ref[b, pl.ds(t0, SUBLANES), hp * LANES:(hp + 1) * LANES] = y8[k]
        return carry

    lax.fori_loop(0, max(1, tc_len // SUBLANES), group, 0)

    for b in range(nb):
        y = y_ref[b, 0:tc_len]
        mu = _seg_sum(y, ones_bd) * (1.0 / HEAD_DIM)
        yc = y - mu
        var = _seg_sum(yc * yc, ones_bd) * (1.0 / HEAD_DIM)
        yn = yc * lax.rsqrt(var + GN_EPS) * gng_ref[...] + gnb_ref[...]
        out_ref[b] = (yn + bon_ref[b, 0:tc_len]) * g_ref[b, 0:tc_len]

    @pl.when(tc == last)
    def _():
        shift_ref[...] = carry_ref[...]
        for b in range(nb):
            for hp in range(npair):
                s = s_ref[b, hp]
                sfin_ref[b, 2 * hp] = s[:, 0:HEAD_DIM]
                sfin_ref[b, 2 * hp + 1] = s[:, HEAD_DIM:2 * HEAD_DIM]


def _rwkv(p, shift_prev, wkv_prev, lw):
    bsz, t, _ = p.shape
    nb = 4
    tc_len = min(128, t)
    row = lambda a: a.reshape(1, -1)
    w_up, a_up = lw["rw_w_up"], lw["rw_a_up"]
    lora = jnp.zeros((LANES, 2 * RW_W), F32)
    lora = lora.at[0:W_LORA, 0:RW_W].set(w_up).at[W_LORA:LANES, RW_W:2 * RW_W].set(a_up)
    rr = jnp.arange(2 * LANES)[:, None]
    cc = jnp.arange(2 * LANES)[None, :]
    ones2 = (rr // HEAD_DIM == cc // HEAD_DIM).astype(BF16)
    diag = (jnp.arange(HEAD_DIM)[:, None] == cc[:, 0:LANES] % HEAD_DIM).astype(F32)
    blk3 = lambda w: pl.BlockSpec((nb, tc_len, w), lambda i, j: (i, j, 0))
    vec = lambda w: _const_spec((1, w))
    assert tc_len < SUBLANES or tc_len % SUBLANES == 0
    scr = lambda w: pltpu.VMEM((nb, max(SUBLANES, tc_len), w), F32)
    out, sfin, shift = pl.pallas_call(
        functools.partial(_rwkv_kernel, nb=nb, tc_len=tc_len),
        grid=(bsz // nb, t // tc_len),
        in_specs=[blk3(SHIFT_COLS),
                  pl.BlockSpec((nb, 1, SHIFT_COLS), lambda i, j: (i, 0, 0)),
                  pl.BlockSpec((nb, RW_HEADS, HEAD_DIM, HEAD_DIM), lambda i, j: (i, 0, 0, 0)),
                  vec(SHIFT_COLS), vec(RW_W), _const_spec((LANES, 2 * RW_W)), vec(RW_W),
                  _const_spec((G_LORA, RW_W)), vec(RW_W), vec(RW_W), vec(RW_W), vec(RW_W), vec(RW_W),
                  _const_spec((2 * LANES, 2 * LANES)), _const_spec((HEAD_DIM, LANES))],
        out_specs=[blk3(RW_W),
                   pl.BlockSpec((nb, RW_HEADS, HEAD_DIM, HEAD_DIM), lambda i, j: (i, 0, 0, 0)),
                   pl.BlockSpec((nb, 1, SHIFT_COLS), lambda i, j: (i, 0, 0))],
        out_shape=[jax.ShapeDtypeStruct((bsz, t, RW_W), F32),
                   jax.ShapeDtypeStruct((bsz, RW_HEADS, HEAD_DIM, HEAD_DIM), F32),
                   jax.ShapeDtypeStruct((bsz, 1, SHIFT_COLS), F32)],
        scratch_shapes=[pltpu.VMEM((nb, tc_len + 8, SHIFT_COLS), F32),
                        pltpu.VMEM((nb, 1, SHIFT_COLS), F32),
                        pltpu.VMEM((nb, RW_HEADS // 2, HEAD_DIM, LANES), F32)] + [scr(RW_W)] * 9,
        compiler_params=_cparams(("arbitrary", "arbitrary")),
        name="rwkv",
    )(p, shift_prev, wkv_prev, row(lw["shift_mu"]), row(lw["rw_w0"]), lora.astype(BF16), row(lw["rw_a0"]), lw["rw_g_up"].astype(BF16),
      row(lw["rw_k_k"]), row(lw["rw_k_a"]), row(lw["rw_r_k"]), row(lw["rw_gn_g"]), row(lw["rw_gn_b"]), ones2, diag)
    return out, sfin, shift


def _ordinal_to_score(o):
    o = jnp.maximum(o, ORD_NEG_INF)
    bits = jnp.where(o >= 0, o, o ^ jnp.int32(0x7FFFFFFF))
    return lax.bitcast_convert_type(bits, F32)


def _select_params(score_ref, nblk, blk, topk, nq, pos_bits):
    kf = jnp.float32(topk)

    def count(pred):
        def one(j, acc):
            off = pl.multiple_of(j * blk, blk)
            kb = score_ref[pl.ds(off, blk), :]
            pos = off + lax.broadcasted_iota(I32, (blk, nq), 0)
            hit = pred(kb, pos).astype(F32)
            return acc + jnp.sum(hit.reshape(blk // SUBLANES, SUBLANES, nq), axis=0)

        acc = lax.fori_loop(0, nblk // 2, lambda jj, acc: one(2 * jj + 1, one(2 * jj, acc)),
                            jnp.zeros((SUBLANES, nq), F32))
        acc = lax.cond(nblk % 2 == 1, lambda acc: one(nblk - 1, acc), lambda acc: acc, acc)
        return jnp.sum(acc, axis=0, keepdims=True)

    zero = jnp.zeros((1, nq), I32)
    c0 = count(lambda kb, pos: kb >= 0.0)
    ordinal = jnp.where(c0 >= kf, zero, jnp.full((1, nq), INT_MIN, I32))
    def bit_body(i, ordinal):
        cand = ordinal | jnp.left_shift(jnp.int32(1), jnp.int32(30) - i)
        cand_score = _ordinal_to_score(cand)
        c = count(lambda kb, pos: kb >= cand_score)
        return jnp.where(c >= kf, cand, ordinal)

    ordinal = lax.fori_loop(0, 31, bit_body, ordinal)
    theta = _ordinal_to_score(ordinal)
    big = jnp.full((1, nq), 2 ** 30, I32)

    def with_ties():
        n_gt = count(lambda kb, pos: kb > theta)
        n_eq = count(lambda kb, pos: kb == theta)
        need = kf - n_gt

        def tie_cut():
            def pbody(i, cut):
                cand = cut | jnp.left_shift(jnp.int32(1), jnp.int32(pos_bits - 1) - i)
                c = count(lambda kb, pos: (kb == theta) & (pos < cand))
                return jnp.where(c < need, cand, cut)
            return lax.fori_loop(0, pos_bits, pbody, jnp.zeros((1, nq), I32))

        return lax.cond(jnp.max(n_eq - need) > 0.0, tie_cut, lambda: big)

    return theta, with_ties()


def _t5_bucket(dist):
    n = jnp.maximum(dist, 0)
    max_exact = NUM_BUCKETS // 2
    nf = jnp.maximum(n, 1).astype(F32)
    large = max_exact + jnp.floor(jnp.log(nf / max_exact) / math.log(MAX_DISTANCE / max_exact)
                                  * (NUM_BUCKETS - max_exact)).astype(I32)
    large = jnp.minimum(large, NUM_BUCKETS - 1)
    return jnp.where(n < max_exact, n, large)


def _near_bias_kernel(rb_ref, o_ref, *, tq):
    c = lax.broadcasted_iota(I32, (2 * tq, 2 * tq), 0)
    l = lax.broadcasted_iota(I32, (2 * tq, 2 * tq), 1)
    first = l < tq
    bucket = _t5_bucket(jnp.where(first, l, l - tq) - c + tq)
    for p in range(ATT_HEADS // 2):
        acc = jnp.zeros((2 * tq, 2 * tq), F32)
        for bk in range(NUM_BUCKETS):
            val = jnp.where(first, rb_ref[bk, 2 * p], rb_ref[bk, 2 * p + 1])
            acc = jnp.where(bucket == bk, val, acc)
        o_ref[p] = acc


def _near_bias(rel_bias, tq):
    return pl.pallas_call(
        functools.partial(_near_bias_kernel, tq=tq),
        in_specs=[pl.BlockSpec(memory_space=pltpu.SMEM)],
        out_specs=pl.BlockSpec(memory_space=pltpu.VMEM),
        out_shape=jax.ShapeDtypeStruct((ATT_HEADS // 2, 2 * tq, 2 * tq), F32),
        compiler_params=pltpu.CompilerParams(vmem_limit_bytes=VMEM_LIMIT),
        name="near_bias",
    )(rel_bias)


def _attn_prompt_kernel(rb_ref, qt_ref, qit_ref, wit_ref, kb_ref, vbt_ref, kdb_ref, near_ref, o_ref,
                        key_ref, qbd_ref, qibd_ref, m_ref, l_ref, acc_ref, *, tq, topk, pos_bits):
    i = pl.program_id(1)
    npair = ATT_HEADS // 2
    upper = lax.broadcasted_iota(I32, (LANES, tq), 0) < HEAD_DIM

    def stack_pair(x):
        zero = jnp.zeros_like(x)
        return jnp.concatenate([jnp.where(upper, x, zero), jnp.where(upper, zero, x)], axis=1)

    for p in range(npair):
        qbd_ref[p] = stack_pair(qt_ref[0, p * LANES:(p + 1) * LANES, :])
    for p in range(IDX_HEADS // 2):
        qibd_ref[p] = stack_pair(qit_ref[0, p * LANES:(p + 1) * LANES, :])
    wrow = wit_ref[0] * (IDX_HEADS ** -0.5 * IDX_DIM ** -0.5)

    kpos = lax.broadcasted_iota(I32, (tq, tq), 0)
    qpos = i * tq + lax.broadcasted_iota(I32, (tq, tq), 1)

    def idx_body(j, carry):
        off = pl.multiple_of(j * tq, tq)
        kd = kdb_ref[0, pl.ds(off, tq), :]
        idx = jnp.zeros((tq, tq), F32)
        for p in range(IDX_HEADS // 2):
            sc = jnp.maximum(jnp.dot(kd, qibd_ref[p], preferred_element_type=F32), 0.0)
            idx = idx + sc[:, 0:tq] * wrow[2 * p:2 * p + 1, :] + sc[:, tq:2 * tq] * wrow[2 * p + 1:2 * p + 2, :]
        idx = jnp.where(idx == 0.0, 0.0, idx)
        key_ref[pl.ds(off, tq), :] = jnp.where(off + kpos <= qpos, idx, -jnp.inf)
        return carry

    lax.fori_loop(0, i + 1, idx_body, 0)

    theta, cut = _select_params(key_ref, i + 1, tq, topk, tq, pos_bits)

    m_ref[...] = jnp.full(m_ref.shape, NEG_BIG, F32)
    l_ref[...] = jnp.zeros(l_ref.shape, F32)
    acc_ref[...] = jnp.zeros(acc_ref.shape, F32)
    first = lax.broadcasted_iota(I32, (1, 2 * tq), 1) < tq
    ones_rows = jnp.ones((2 * SUBLANES, tq), BF16)

    def attend(j, near_rows):
        off = pl.multiple_of(j * tq, tq)
        kb = key_ref[pl.ds(off, tq), :]
        pos = off + kpos
        sel = ((kb > theta) | ((kb == theta) & (pos <= cut))) & (pos <= qpos)
        mb = jnp.where(sel, 0.0, -jnp.inf)
        mb = jnp.concatenate([mb, mb], axis=1)
        ss = [jnp.dot(kb_ref[0, pl.ds(off, tq), p * LANES:(p + 1) * LANES], qbd_ref[p], preferred_element_type=F32)
              for p in range(npair)]
        prs, alphas = [], []
        for p in range(npair):
            if near_rows is None:
                logits = ss[p] + mb
                const = jnp.where(first, rb_ref[NUM_BUCKETS - 1, 2 * p], rb_ref[NUM_BUCKETS - 1, 2 * p + 1])
            else:
                logits = ss[p] + (near_ref[p, near_rows:near_rows + tq, :] + mb)
                const = jnp.zeros((1, 2 * tq), F32)
            m_old = m_ref[p]
            m_new = jnp.maximum(m_old, jnp.max(logits, axis=0, keepdims=True) + const)
            alphas.append(jnp.exp(m_old - m_new))
            prs.append(jnp.exp(logits - (m_new[0:1, :] - const)).astype(BF16))
            m_ref[p] = m_new
        for p in range(npair):
            vt = jnp.concatenate([vbt_ref[0, j, p * LANES:(p + 1) * LANES, :], ones_rows], axis=0)
            pv = jnp.dot(vt, prs[p], preferred_element_type=F32)
            l_ref[p] = alphas[p] * l_ref[p] + pv[LANES:LANES + SUBLANES, :]
            acc_ref[p] = alphas[p][0:1, :] * acc_ref[p] + pv[0:LANES, :]

    def far_body(j, carry):
        attend(j, None)
        return carry

    lax.fori_loop(0, jnp.maximum(i - 1, 0), far_body, 0)

    @pl.when(i >= 1)
    def _():
        attend(i - 1, 0)

    attend(i, tq)

    for p in range(npair):
        o = acc_ref[p] / l_ref[p][0:1, :]
        ot = jnp.concatenate([o[0:HEAD_DIM, 0:tq], o[HEAD_DIM:LANES, tq:2 * tq]], axis=0)
        o_ref[0, :, p * LANES:(p + 1) * LANES] = ot.T


def _attn_prompt(qt, qit, wit, kb, vbt, kdb, rel_bias, bsz, t, tq):
    topk = min(MAX_TOPK, t // 4)
    pos_bits = max(1, (t - 1).bit_length())
    near = _near_bias(rel_bias, tq)
    npair = ATT_HEADS // 2
    nblk = t // tq
    qtile_t = lambda rows: pl.BlockSpec((1, rows, tq), lambda b, i: (b, 0, i))
    full = lambda w: pl.BlockSpec((1, t, w), lambda b, i: (b, 0, 0))
    return pl.pallas_call(
        functools.partial(_attn_prompt_kernel, tq=tq, topk=topk, pos_bits=pos_bits),
        grid=(bsz, nblk),
        in_specs=[pl.BlockSpec(memory_space=pltpu.SMEM),
                  qtile_t(ATT_W), qtile_t(IDX_HEADS * IDX_DIM), qtile_t(SUBLANES),
                  full(ATT_W), pl.BlockSpec((1, nblk, ATT_W, tq), lambda b, i: (b, 0, 0, 0)), full(LANES),
                  _const_spec((npair, 2 * tq, 2 * tq))],
        out_specs=pl.BlockSpec((1, tq, ATT_W), lambda b, i: (b, i, 0)),
        out_shape=jax.ShapeDtypeStruct((bsz, t, ATT_W), F32),
        scratch_shapes=[pltpu.VMEM((t, tq), F32),
                        pltpu.VMEM((npair, LANES, 2 * tq), BF16),
                        pltpu.VMEM((IDX_HEADS // 2, LANES, 2 * tq), BF16),
                        pltpu.VMEM((npair, SUBLANES, 2 * tq), F32),
                        pltpu.VMEM((npair, SUBLANES, 2 * tq), F32),
                        pltpu.VMEM((npair, LANES, 2 * tq), F32)],
        compiler_params=_cparams(("arbitrary", "arbitrary")),
        name="attn_prompt",
    )(rel_bias, qt, qit, wit, kb.reshape(bsz, t, ATT_W), vbt, kdb.reshape(bsz, t, LANES), near)


def _sample_index_kernel(pt_ref, qi_ref, wi_ref, *rest, n_pages, ts):
    page_refs = rest[:n_pages + 1]
    o_ref = rest[n_pages + 1]
    qi = qi_ref[0]
    q16 = jnp.concatenate([qi[:, h * IDX_DIM:(h + 1) * IDX_DIM] for h in range(IDX_HEADS)], axis=0).astype(BF16)
    wi = wi_ref[0]
    scale = IDX_HEADS ** -0.5 * IDX_DIM ** -0.5
    w16 = jnp.concatenate([wi[:, h:h + 1] for h in range(IDX_HEADS)], axis=0) * scale
    trow = lax.broadcasted_iota(I32, (ts, PAGE_SIZE), 0)
    ncol = lax.broadcasted_iota(I32, (ts, PAGE_SIZE), 1)
    for j in range(n_pages + 1):
        kp = page_refs[j][0].astype(BF16)
        if j < n_pages:
            sc = jnp.dot(q16, kp, preferred_element_type=F32)
        else:
            sc = lax.dot_general(q16, kp, (((1,), (1,)), ((), ())), preferred_element_type=F32)
        rw = jnp.maximum(sc, 0.0) * w16
        idx = rw[0:ts]
        for h in range(1, IDX_HEADS):
            idx = idx + rw[h * ts:(h + 1) * ts]
        if j == n_pages:
            idx = jnp.where(ncol <= trow, idx, -jnp.inf)
        o_ref[0, :, j * PAGE_SIZE:(j + 1) * PAGE_SIZE] = idx


def _sample_select_kernel(idx_ref, sel_ref, key_ref, *, topk, pos_bits):
    lp, nq = idx_ref.shape
    idx = idx_ref[...]
    key_ref[...] = jnp.where(idx == 0.0, 0.0, idx)
    theta, cut = _select_params(key_ref, 1, lp, topk, nq, pos_bits)
    kb = key_ref[...]
    pos = lax.broadcasted_iota(I32, (lp, nq), 0)
    sel = ((kb > theta) | ((kb == theta) & (pos <= cut))) & (idx_ref[...] > -jnp.inf)
    sel_ref[...] = sel.astype(F32)


def _sample_bias_kernel(rb_ref, near_ref, new_ref, far_ref, *, ts):
    rows = ts * ATT_HEADS
    r = lax.broadcasted_iota(I32, (rows, PAGE_SIZE), 0)
    c = lax.broadcasted_iota(I32, (rows, PAGE_SIZE), 1)
    t = r // ATT_HEADS
    rhead = r % ATT_HEADS

    def gather(bucket):
        acc = jnp.zeros((rows, PAGE_SIZE), F32)
        for h in range(ATT_HEADS):
            for bk in range(NUM_BUCKETS):
                acc = jnp.where((bucket == bk) & (rhead == h), rb_ref[bk, h], acc)
        return acc

    near_ref[...] = gather(_t5_bucket(PAGE_SIZE + t - c))
    ok = (c <= t) & (c < ts)
    new_ref[...] = jnp.where(ok, gather(_t5_bucket(t - c)), -jnp.inf)
    far_ref[...] = gather(jnp.full((rows, PAGE_SIZE), NUM_BUCKETS - 1, I32))


def _sample_attn_kernel(pt_ref, q_ref, sel_ref, near_ref, new_ref, far_ref, *rest, n_pages, ts):
    k_refs = rest[:n_pages + 1]
    v_refs = rest[n_pages + 1:2 * n_pages + 2]
    o_ref = rest[2 * n_pages + 2]
    l_ref = rest[2 * n_pages + 3]
    rows = ts * ATT_HEADS
    rep = lambda a: jnp.concatenate([jnp.broadcast_to(a[t:t + 1], (ATT_HEADS, a.shape[1])) for t in range(ts)], axis=0)
    r = lax.broadcasted_iota(I32, (rows, ATT_W), 0)
    c = lax.broadcasted_iota(I32, (rows, ATT_W), 1)
    own = (r % ATT_HEADS) == (c // HEAD_DIM)
    q = rep(q_ref[0].astype(F32) * HEAD_DIM ** -0.5)
    q = jnp.where(own, q, 0.0).astype(BF16)
    sel = rep(sel_ref[0]) > 0.5

    for j in range(n_pages):
        kt = k_refs[j][0].reshape(ATT_W, PAGE_SIZE).astype(BF16)
        s = jnp.dot(q, kt, preferred_element_type=F32)
        bias = near_ref[...] if j == n_pages - 1 else far_ref[...]
        sl = slice(j * PAGE_SIZE, (j + 1) * PAGE_SIZE)
        l_ref[:, sl] = jnp.where(sel[:, sl], s + bias, -jnp.inf)
    kn = k_refs[n_pages][0].astype(BF16)
    s = lax.dot_general(q, kn, (((1,), (1,)), ((), ())), preferred_element_type=F32)
    sl = slice(n_pages * PAGE_SIZE, (n_pages + 1) * PAGE_SIZE)
    l_ref[:, sl] = jnp.where(sel[:, sl], s + new_ref[...], -jnp.inf)

    logits = l_ref[...]
    m = jnp.max(logits, axis=1, keepdims=True)
    pr = jnp.exp(logits - m)
    den = jnp.sum(pr, axis=1, keepdims=True)
    prb = pr.astype(BF16)
    acc = jnp.dot(prb[:, sl], v_refs[n_pages][0].astype(BF16), preferred_element_type=F32)
    for j in range(n_pages):
        vt = v_refs[j][0].reshape(ATT_W, PAGE_SIZE).astype(BF16)
        acc = acc + lax.dot_general(prb[:, j * PAGE_SIZE:(j + 1) * PAGE_SIZE], vt, (((1,), (1,)), ((), ())),
                                    preferred_element_type=F32)
    acc = jnp.where(own, acc / den, 0.0)
    o_ref[0] = jnp.sum(acc.reshape(ts, ATT_HEADS, ATT_W), axis=1)


def _attn_sample(qb, qi, wi, k_new, v_new, kd_new, cache_k, cache_v, cache_kidx, page_table, rel_bias, bsz, ts):
    n_pages = page_table.shape[1]
    past_len = n_pages * PAGE_SIZE
    topk = min(MAX_TOPK, (past_len + ts) // 4)
    lp = (n_pages + 1) * PAGE_SIZE
    rows = ts * ATT_HEADS
    pt = page_table.reshape(-1).astype(I32)

    ki_pool = jnp.swapaxes(cache_kidx, 1, 2)
    ki_new = jnp.zeros((bsz, PAGE_SIZE, IDX_DIM), F32).at[:, 0:ts].set(kd_new.reshape(bsz, ts, LANES)[..., 0:IDX_DIM])
    page_spec = lambda j, shp: pl.BlockSpec((1,) + shp, lambda b, ptr: (ptr[b * n_pages + j],) + (0,) * len(shp))
    per_b = lambda shp: pl.BlockSpec((1,) + shp, lambda b, ptr: (b,) + (0,) * len(shp))
    idx = pl.pallas_call(
        functools.partial(_sample_index_kernel, n_pages=n_pages, ts=ts),
        grid_spec=pltpu.PrefetchScalarGridSpec(
            num_scalar_prefetch=1, grid=(bsz,),
            in_specs=[per_b((ts, IDX_HEADS * IDX_DIM)), per_b((ts, LANES))]
                     + [page_spec(j, (IDX_DIM, PAGE_SIZE)) for j in range(n_pages)]
                     + [per_b((PAGE_SIZE, IDX_DIM))],
            out_specs=per_b((ts, lp))),
        out_shape=jax.ShapeDtypeStruct((bsz, ts, lp), F32),
        compiler_params=_cparams(("arbitrary",)),
        name="sample_index",
    )(pt, qi.reshape(bsz, ts, -1), wi.reshape(bsz, ts, LANES), *([ki_pool] * n_pages), ki_new)

    sel = pl.pallas_call(
        functools.partial(_sample_select_kernel, topk=topk, pos_bits=max(1, (lp - 1).bit_length())),
        out_shape=jax.ShapeDtypeStruct((lp, bsz * ts), F32),
        scratch_shapes=[pltpu.VMEM((lp, bsz * ts), F32)],
        compiler_params=pltpu.CompilerParams(vmem_limit_bytes=VMEM_LIMIT),
        name="sample_select",
    )(idx.reshape(bsz * ts, lp).T)
    sel = sel.T.reshape(bsz, ts, lp)

    near, newb, far = pl.pallas_call(
        functools.partial(_sample_bias_kernel, ts=ts),
        in_specs=[pl.BlockSpec(memory_space=pltpu.SMEM)],
        out_shape=[jax.ShapeDtypeStruct((rows, PAGE_SIZE), F32)] * 3,
        compiler_params=pltpu.CompilerParams(vmem_limit_bytes=VMEM_LIMIT),
        name="sample_bias",
    )(rel_bias)

    page4 = (ATT_HEADS, HEAD_DIM, PAGE_SIZE)
    k_pool = jnp.transpose(cache_k, (0, 2, 3, 1))
    v_pool = jnp.transpose(cache_v, (0, 2, 3, 1))
    pad_new = lambda a: jnp.zeros((bsz, PAGE_SIZE, ATT_W), F32).at[:, 0:ts].set(a.reshape(bsz, ts, ATT_W))
    const = lambda shp: pl.BlockSpec(shp, lambda b, ptr: (0,) * len(shp))
    out = pl.pallas_call(
        functools.partial(_sample_attn_kernel, n_pages=n_pages, ts=ts),
        grid_spec=pltpu.PrefetchScalarGridSpec(
            num_scalar_prefetch=1, grid=(bsz,),
            in_specs=[per_b((ts, ATT_W)), per_b((ts, lp)),
                      const((rows, PAGE_SIZE)), const((rows, PAGE_SIZE)), const((rows, PAGE_SIZE))]
                     + [page_spec(j, page4) for j in range(n_pages)] + [per_b((PAGE_SIZE, ATT_W))]
                     + [page_spec(j, page4) for j in range(n_pages)] + [per_b((PAGE_SIZE, ATT_W))],
            out_specs=per_b((ts, ATT_W)),
            scratch_shapes=[pltpu.VMEM((rows, lp), F32)]),
        out_shape=jax.ShapeDtypeStruct((bsz, ts, ATT_W), F32),
        compiler_params=_cparams(("arbitrary",)),
        name="sample_attn",
    )(pt, qb.reshape(bsz, ts, ATT_W), sel, near, newb, far,
      *([k_pool] * n_pages), pad_new(k_new), *([v_pool] * n_pages), pad_new(v_new))
    return out.reshape(bsz * ts, ATT_W)


def _layer_norm(x, g, b):
    mu = jnp.mean(x, axis=-1, keepdims=True)
    xc = x - mu
    var = jnp.mean(xc * xc, axis=-1, keepdims=True)
    return xc * lax.rsqrt(var + LN_EPS) * g + b


def _ffn_kernel(x_ref, rw_ref, att_ref, prev_ref, wo_ref, g1_ref, b1_ref, wup_ref, cw_ref, cb_ref, wdn_ref,
                g2_ref, b2_ref, y_ref, nc_ref, pad_ref, *, tm, shift, off, alpha):
    ti = pl.program_id(1)
    data = off + 2 * shift

    @pl.when(ti == 0)
    def _():
        pad_ref[off:data, :] = prev_ref[0]

    x = x_ref[0]
    mix = (jnp.dot(rw_ref[0].astype(BF16), wo_ref[0:RW_W, :], preferred_element_type=F32)
           + jnp.dot(att_ref[0].astype(BF16), wo_ref[RW_W:RW_W + ATT_W, :], preferred_element_type=F32))
    x1 = _layer_norm(alpha * x + mix, g1_ref[...], b1_ref[...])
    x1b = x1.astype(BF16)
    u_conv = jnp.dot(x1b, wup_ref[:, 0:D_FF], preferred_element_type=F32)
    u_lin = jnp.dot(x1b, wup_ref[:, D_FF:2 * D_FF], preferred_element_type=F32)
    pad_ref[data:data + tm, :] = u_conv
    conv = (pad_ref[off:off + tm, :] * cw_ref[0:1, :] + pad_ref[off + shift:off + shift + tm, :] * cw_ref[1:2, :]
            + u_conv * cw_ref[2:3, :] + cb_ref[...])
    h = conv * _sigmoid(conv) * u_lin
    down = jnp.dot(h.astype(BF16), wdn_ref[...], preferred_element_type=F32)
    y_ref[0] = _layer_norm(alpha * x1 + down, g2_ref[...], b2_ref[...])
    tail = pad_ref[off + tm:data + tm, :]
    pad_ref[off:data, :] = tail
    nc_ref[0] = tail


def _ffn(x, rw, att, prev, lw, alpha, shift):
    nb, rows, _ = x.shape
    tm = min(256, rows) if shift == 1 else rows
    off = 6 if shift == 1 else 0
    row = lambda a: a.reshape(1, -1)
    tile = lambda w: pl.BlockSpec((1, tm, w), lambda b, i: (b, i, 0))
    y, nc = pl.pallas_call(
        functools.partial(_ffn_kernel, tm=tm, shift=shift, off=off, alpha=alpha),
        grid=(nb, rows // tm),
        in_specs=[tile(D_MODEL), tile(RW_W), tile(ATT_W),
                  pl.BlockSpec((1, 2 * shift, D_FF), lambda b, i: (b, 0, 0)),
                  _const_spec((D_MODEL, D_MODEL)), _const_spec((1, D_MODEL)), _const_spec((1, D_MODEL)),
                  _const_spec((D_MODEL, 2 * D_FF)), _const_spec((CONV_W, D_FF)), _const_spec((1, D_FF)),
                  _const_spec((D_FF, D_MODEL)), _const_spec((1, D_MODEL)), _const_spec((1, D_MODEL))],
        out_specs=[tile(D_MODEL), pl.BlockSpec((1, 2 * shift, D_FF), lambda b, i: (b, 0, 0))],
        out_shape=[jax.ShapeDtypeStruct((nb, rows, D_MODEL), F32), jax.ShapeDtypeStruct((nb, 2 * shift, D_FF), F32)],
        scratch_shapes=[pltpu.VMEM((off + 2 * shift + tm, D_FF), F32)],
        compiler_params=_cparams(("arbitrary", "arbitrary")),
        name="ffn",
    )(x, rw, att, prev, lw["w_out"].astype(BF16), row(lw["ln1_g"]), row(lw["ln1_b"]),
      lw["ffn_w_up"].astype(BF16), lw["ffn_conv_w"], row(lw["ffn_conv_b"]), lw["ffn_w_down"].astype(BF16),
      row(lw["ln2_g"]), row(lw["ln2_b"]))
    return y, nc


def _layer(x, shift_prev, wkv_prev, conv_prev, past, rel_bias, lw, alpha):
    bsz, t, _ = x.shape
    w_in = lw["w_in"]
    w_ki = w_in[:, C_KI:C_KI + IDX_DIM]
    w_wi = w_in[:, C_KI + IDX_DIM:]
    wp = jnp.concatenate([w_in[:, 0:C_KI], w_ki, w_ki, w_wi,
                          jnp.zeros((D_MODEL, LANES - IDX_HEADS), F32)], axis=1).astype(BF16)
    x2d = x.reshape(bsz * t, D_MODEL)
    if past is None:
        tq = min(256, t)
        prw, qt, kb, kt, vt, vbt, qit, kdb, kit, wit = _proj_t(x2d, wp, bsz, t, tq)
        rw_out, new_wkv, new_shift = _rwkv(prw.reshape(bsz, t, SHIFT_COLS), shift_prev, wkv_prev, lw)
        att = _attn_prompt(qt, qit, wit, kb, vbt, kdb, rel_bias, bsz, t, tq)
        y, new_conv = _ffn(x, rw_out, att, conv_prev, lw, alpha, shift=1)
        heads = lambda a: jnp.transpose(a.reshape(bsz, ATT_HEADS, HEAD_DIM, t), (0, 3, 1, 2))
        return y, (heads(kt), heads(vt), jnp.swapaxes(kit, 1, 2), new_wkv, new_shift, new_conv)
    prw, qb, k, v, qi, kd, wi = _proj(x2d, wp)
    rw_out, new_wkv, new_shift = _rwkv(prw.reshape(bsz, t, SHIFT_COLS), shift_prev, wkv_prev, lw)
    att = _attn_sample(qb, qi, wi, k, v, kd, *past, rel_bias, bsz, t)
    tmaj = lambda a: jnp.swapaxes(a.reshape(bsz, t, -1), 0, 1).reshape(1, t * bsz, -1)
    prev = jnp.swapaxes(conv_prev, 0, 1).reshape(1, (CONV_W - 1) * bsz, D_FF)
    y, new_conv = _ffn(tmaj(x), tmaj(rw_out), tmaj(att), prev, lw, alpha, shift=bsz)
    y = jnp.swapaxes(y.reshape(t, bsz, D_MODEL), 0, 1)
    new_conv = jnp.swapaxes(new_conv.reshape(CONV_W - 1, bsz, D_FF), 0, 1)
    state = (k.reshape(bsz, t, ATT_HEADS, HEAD_DIM), v.reshape(bsz, t, ATT_HEADS, HEAD_DIM),
             kd.reshape(bsz, t, LANES)[..., 0:IDX_DIM], new_wkv, new_shift, new_conv)
    return y, state


_LAYER_WEIGHTS = ("w_in", "shift_mu", "rw_w0", "rw_w_up", "rw_a0", "rw_a_up", "rw_g_up", "rw_k_k", "rw_k_a",
                  "rw_r_k", "rw_gn_g", "rw_gn_b", "w_out", "ln1_g", "ln1_b", "ffn_w_up", "ffn_conv_w",
                  "ffn_conv_b", "ffn_w_down", "ln2_g", "ln2_b")


def kernel(x_prompt, x_sample, cache_k, cache_v, cache_kidx, page_table, state_wkv, state_shift, state_conv,
           rel_bias, w_in, shift_mu, rw_w0, rw_w_up, rw_a0, rw_a_up, rw_g_up, rw_k_k, rw_k_a, rw_r_k, rw_gn_g,
           rw_gn_b, w_out, ln1_g, ln1_b, ffn_w_up, ffn_conv_w, ffn_conv_b, ffn_w_down, ln2_g, ln2_b):
    stacked = dict(zip(_LAYER_WEIGHTS, (w_in, shift_mu, rw_w0, rw_w_up, rw_a0, rw_a_up, rw_g_up, rw_k_k, rw_k_a,
                                        rw_r_k, rw_gn_g, rw_gn_b, w_out, ln1_g, ln1_b, ffn_w_up, ffn_conv_w,
                                        ffn_conv_b, ffn_w_down, ln2_g, ln2_b)))
    depth = w_in.shape[0]
    alpha = (2 * depth) ** 0.25
    bp = x_prompt.shape[0]
    yp, ys = x_prompt, x_sample
    outs_p, outs_s = [], []
    for l in range(depth):
        lw = {name: w[l] for name, w in stacked.items()}
        yp, st_p = _layer(yp, jnp.zeros((bp, 1, SHIFT_COLS), F32),
                          jnp.zeros((bp, RW_HEADS, HEAD_DIM, HEAD_DIM), F32),
                          jnp.zeros((bp, CONV_W - 1, D_FF), F32), None, rel_bias, lw, alpha)
        ys, st_s = _layer(ys, state_shift[l], state_wkv[l], state_conv[l],
                          (cache_k[l], cache_v[l], cache_kidx[l], page_table), rel_bias, lw, alpha)
        outs_p.append(st_p)
        outs_s.append(st_s)
    st = lambda outs, i: jnp.stack([o[i] for o in outs])
    return (yp, ys) + tuple(st(outs_p, i) for i in range(6)) + tuple(st(outs_s, i) for i in range(6))
```

```python
import functools
import math

import jax
import jax.numpy as jnp
from jax import lax
from jax.experimental import pallas as pl
from jax.experimental.pallas import tpu as pltpu

F32 = jnp.float32
BF16 = jnp.bfloat16
I32 = jnp.int32

D_MODEL = 1024
HEAD_DIM = 64
RW_W = 512
ATT_W = 512
RW_HEADS = 8
ATT_HEADS = 8
IDX_HEADS = 4
IDX_DIM = 64
MAX_TOPK = 256
W_LORA = 64
A_LORA = 64
G_LORA = 128
GN_EPS = 64e-5
LN_EPS = 1e-5
D_FF = 2816
CONV_W = 3
NUM_BUCKETS = 32
MAX_DISTANCE = 128
PAGE_SIZE = 128
SHIFT_COLS = 3 * RW_W + W_LORA + A_LORA + G_LORA
LANES = 128
SUBLANES = 8
C_Q = SHIFT_COLS
C_K = C_Q + ATT_W
C_V = C_K + ATT_W
C_QI = C_V + ATT_W
C_KI = C_QI + IDX_HEADS * IDX_DIM
C_WI = C_KI + LANES
C_END = C_WI + LANES
VMEM_LIMIT = 56 * 1024 * 1024
INT_MIN = -(2 ** 31)
ORD_NEG_INF = INT_MIN + 0x7FFFFF
NEG_BIG = -1e30


def _cparams(sem):
    return pltpu.CompilerParams(dimension_semantics=sem, vmem_limit_bytes=VMEM_LIMIT)


def _const_spec(shape):
    zeros = (0,) * len(shape)
    return pl.BlockSpec(shape, lambda *_: zeros, pipeline_mode=pl.Buffered(1))


def _proj_kernel(x_ref, w_ref, prw_ref, qb_ref, k_ref, v_ref, qi_ref, kd_ref, wi_ref):
    x = x_ref[...].astype(BF16)

    def mm(lo, hi):
        return jnp.dot(x, w_ref[:, lo:hi], preferred_element_type=F32)

    prw_ref[...] = mm(0, C_Q)
    qb_ref[...] = mm(C_Q, C_K).astype(BF16)
    k_ref[...] = mm(C_K, C_V)
    v_ref[...] = mm(C_V, C_QI)
    qi_ref[...] = mm(C_QI, C_KI)
    kd_ref[...] = mm(C_KI, C_WI)
    wi_ref[...] = mm(C_WI, C_END)


def _proj(x2d, wp):
    m = x2d.shape[0]
    tm = min(256, m)
    widths = [(C_Q, F32), (ATT_W, BF16), (ATT_W, F32), (ATT_W, F32), (IDX_HEADS * IDX_DIM, F32), (LANES, F32),
              (LANES, F32)]
    return pl.pallas_call(
        _proj_kernel,
        grid=(m // tm,),
        in_specs=[pl.BlockSpec((tm, D_MODEL), lambda i: (i, 0)), _const_spec((D_MODEL, C_END))],
        out_specs=[pl.BlockSpec((tm, w), lambda i: (i, 0)) for w, _ in widths],
        out_shape=[jax.ShapeDtypeStruct((m, w), dt) for w, dt in widths],
        compiler_params=_cparams(("arbitrary",)),
        name="proj",
    )(x2d, wp)


def _proj_t_kernel(x_ref, w_ref, wt_ref, prw_ref, qt_ref, kb_ref, kt_ref, vt_ref, vbt_ref, qit_ref, kdb_ref, kit_ref,
                   wit_ref):
    x = x_ref[...].astype(BF16)

    def mm(lo, hi):
        return jnp.dot(x, w_ref[:, lo:hi], preferred_element_type=F32)

    def mt(lo, hi):
        return lax.dot_general(wt_ref[lo:hi, :], x, (((1,), (1,)), ((), ())), preferred_element_type=F32)

    prw_ref[...] = mm(0, C_Q)
    kb_ref[...] = mm(C_K, C_V).astype(BF16)
    kdb_ref[...] = mm(C_KI, C_WI).astype(BF16)
    qt_ref[0] = (mt(C_Q, C_K) * HEAD_DIM ** -0.5).astype(BF16)
    kt_ref[0] = mt(C_K, C_V)
    vt = mt(C_V, C_QI)
    vt_ref[0] = vt
    for c in range(vt.shape[1] // LANES):
        vbt_ref[0, c] = vt[:, c * LANES:(c + 1) * LANES].astype(BF16)
    qit_ref[0] = mt(C_QI, C_KI).astype(BF16)
    kit_ref[0] = mt(C_KI, C_KI + IDX_DIM)
    wit_ref[0] = mt(C_WI, C_WI + SUBLANES)


def _proj_t(x2d, wp, bsz, t, tm):
    m = x2d.shape[0]
    nblk = t // tm
    rows = lambda w: pl.BlockSpec((tm, w), lambda b, i: (b * nblk + i, 0))
    cols = lambda r: pl.BlockSpec((1, r, tm), lambda b, i: (b, 0, i))
    tshape = lambda r, dt: jax.ShapeDtypeStruct((bsz, r, t), dt)
    return pl.pallas_call(
        _proj_t_kernel,
        grid=(bsz, nblk),
        in_specs=[rows(D_MODEL), _const_spec((D_MODEL, C_END)), _const_spec((C_END, D_MODEL))],
        out_specs=[rows(C_Q), cols(ATT_W), rows(ATT_W), cols(ATT_W), cols(ATT_W),
                   pl.BlockSpec((1, tm // LANES, ATT_W, LANES), lambda b, i: (b, i, 0, 0)),
                   cols(IDX_HEADS * IDX_DIM), rows(LANES), cols(IDX_DIM), cols(SUBLANES)],
        out_shape=[jax.ShapeDtypeStruct((m, C_Q), F32), tshape(ATT_W, BF16), jax.ShapeDtypeStruct((m, ATT_W), BF16),
                   tshape(ATT_W, F32), tshape(ATT_W, F32), jax.ShapeDtypeStruct((bsz, t // LANES, ATT_W, LANES), BF16),
                   tshape(IDX_HEADS * IDX_DIM, BF16), jax.ShapeDtypeStruct((m, LANES), BF16),
                   tshape(IDX_DIM, F32), tshape(SUBLANES, F32)],
        compiler_params=_cparams(("arbitrary", "arbitrary")),
        name="proj_t",
    )(x2d, wp, wp.T)


def _softplus(z):
    return jnp.maximum(z, 0.0) + jnp.log1p(jnp.exp(-jnp.abs(z)))


def _sigmoid(z):
    return 1.0 / (1.0 + jnp.exp(-z))


def _seg_sum(x, ones_bd):
    hi = x.astype(BF16)
    mid = (x - hi.astype(F32)).astype(BF16)
    parts = [jnp.dot(jnp.concatenate([hi[:, g * LANES:(g + 1) * LANES], mid[:, g * LANES:(g + 1) * LANES]], axis=1),
                     ones_bd, preferred_element_type=F32) for g in range(x.shape[1] // LANES)]
    return jnp.concatenate(parts, axis=1)


def _rwkv_kernel(p_ref, sp_ref, s0_ref, mu_ref, w0_ref, lw_ref, a0_ref, gup_ref, kk_ref, ka_ref, rk_ref,
                 gng_ref, gnb_ref, ones2_ref, diag_ref,
                 out_ref, sfin_ref, shift_ref,
                 pad_ref, carry_ref, s_ref, kkv_ref, dec_ref, b_ref, k2_ref, r_ref, v_ref, g_ref, bon_ref, y_ref,
                 *, nb, tc_len):
    tc = pl.program_id(1)
    last = pl.num_programs(1) - 1
    npair = RW_HEADS // 2

    @pl.when(tc == 0)
    def _():
        carry_ref[...] = sp_ref[...]
        for b in range(nb):
            for hp in range(npair):
                s_ref[b, hp] = jnp.concatenate([s0_ref[b, 2 * hp], s0_ref[b, 2 * hp + 1]], axis=1)

    rows = lax.broadcasted_iota(I32, (2 * LANES, LANES), 0) % LANES
    cols = lax.broadcasted_iota(I32, (2 * LANES, LANES), 1)
    ones_bd = (rows // HEAD_DIM == cols // HEAD_DIM).astype(BF16)
    lane = lax.broadcasted_iota(I32, (tc_len, LANES), 1)

    for b in range(nb):
        x = p_ref[b]
        pad_ref[b, 7:8, :] = carry_ref[b]
        pad_ref[b, 8:8 + tc_len, :] = x
        xm1 = pad_ref[b, 7:7 + tc_len, :]
        ps = x + (xm1 - x) * mu_ref[...]
        carry_ref[b] = x[tc_len - 1:tc_len, :]
        r = ps[:, 0:RW_W]
        k = ps[:, RW_W:2 * RW_W]
        v = ps[:, 2 * RW_W:3 * RW_W]
        o = 3 * RW_W
        z = ps[:, o:o + LANES]
        z = jnp.where(lane < W_LORA, jnp.tanh(z), z)
        lo = jnp.dot(z.astype(BF16), lw_ref[...], preferred_element_type=F32)
        w_log = -_softplus(-(w0_ref[...] + lo[:, 0:RW_W])) - 0.5
        dec = jnp.exp(-jnp.exp(w_log))
        a = _sigmoid(a0_ref[...] + lo[:, RW_W:2 * RW_W])
        gd = ps[:, o + LANES:o + 2 * LANES]
        g = jnp.dot(_sigmoid(gd).astype(BF16), gup_ref[...], preferred_element_type=F32)
        kkx = k * kk_ref[...]
        ss = _seg_sum(kkx * kkx, ones_bd)
        kkn = kkx * lax.rsqrt(jnp.maximum(ss, 1e-24))
        k2 = k * (1.0 + (a - 1.0) * ka_ref[...])
        bonus = _seg_sum(r * k2 * rk_ref[...], ones_bd) * v
        kkv_ref[b, 0:tc_len] = kkn
        dec_ref[b, 0:tc_len] = dec
        b_ref[b, 0:tc_len] = kkn * a
        k2_ref[b, 0:tc_len] = k2
        r_ref[b, 0:tc_len] = r
        v_ref[b, 0:tc_len] = v
        g_ref[b, 0:tc_len] = g
        bon_ref[b, 0:tc_len] = bonus

    diag = diag_ref[...]
    diag_b = diag.astype(BF16)
    drow = lax.broadcasted_iota(I32, (HEAD_DIM, LANES), 0)
    dlane = lax.broadcasted_iota(I32, (HEAD_DIM, LANES), 1)
    diag2_b = ((drow + 1) % HEAD_DIM == dlane % HEAD_DIM).astype(BF16)
    seg_start = lax.broadcasted_iota(I32, (SUBLANES, LANES), 1) % HEAD_DIM == 0

    sub = min(SUBLANES, tc_len)
    riota = lax.broadcasted_iota(I32, (SUBLANES, LANES), 0)

    tiles = [(b, hp) for b in range(nb) for hp in range(npair)]
    nt = len(tiles)

    def bsum(xs):
        lhs = jnp.concatenate([jnp.concatenate([xs[2 * m], xs[2 * m + 1]], axis=1) for m in range(nt // 2)], axis=0)
        red = jnp.dot(lhs, ones2_ref[...], preferred_element_type=F32)
        return [red[(k // 2) * HEAD_DIM:(k // 2 + 1) * HEAD_DIM, (k % 2) * LANES:(k % 2 + 1) * LANES]
                for k in range(nt)]

    def group(gi, carry):
        t0 = pl.multiple_of(gi * SUBLANES, SUBLANES)
        rows8 = [[ref[b, pl.ds(t0, SUBLANES), hp * LANES:(hp + 1) * LANES]
                  for ref in (kkv_ref, dec_ref, b_ref, k2_ref, r_ref, v_ref)] for b, hp in tiles]
        vhi8 = [rows8[k][5].astype(BF16).astype(F32) for k in range(nt)]
        vmid8 = [rows8[k][5] - vhi8[k] for k in range(nt)]
        vmid8 = [jnp.where(seg_start, pltpu.roll(x, LANES - HEAD_DIM + 1, 1), pltpu.roll(x, 1, 1)) for x in vmid8]
        s = [s_ref[b, hp] for b, hp in tiles]
        y8 = [jnp.zeros((SUBLANES, LANES), F32) for _ in tiles]
        tile_of = lambda row: jnp.broadcast_to(row, (HEAD_DIM, LANES)).astype(BF16)
        for i in range(sub):
            kkr, wr, br, kr, rr = [[rows8[k][q][i:i + 1, :] for k in range(nt)] for q in range(5)]
            sa = bsum([(s[k] * kkr[k]).astype(BF16) for k in range(nt)])
            vb = bsum([tile_of(vhi8[k][i:i + 1, :]) * diag_b + tile_of(vmid8[k][i:i + 1, :]) * diag2_b
                       for k in range(nt)])
            for k in range(nt):
                s[k] = s[k] * wr[k] - sa[k] * br[k] + vb[k] * kr[k]
            yb = bsum([(s[k] * rr[k]).astype(BF16) for k in range(nt)])
            for k in range(nt):
                yrow = jnp.sum(yb[k] * diag, axis=0, keepdims=True)
                y8[k] = jnp.where(riota == i, yrow, y8[k])
        for k, (b, hp) in enumerate(tiles):
            s_ref[b, hp] = s[k]
            y_ref[b, pl.ds(t0, SUBLANES), hp * LANES:(hp + 1) * LANES] = y8[k]
        return carry

    lax.fori_loop(0, max(1, tc_len // SUBLANES), group, 0)

    for b in range(nb):
        y = y_ref[b, 0:tc_len]
        mu = _seg_sum(y, ones_bd) * (1.0 / HEAD_DIM)
        yc = y - mu
        var = _seg_sum(yc * yc, ones_bd) * (1.0 / HEAD_DIM)
        yn = yc * lax.rsqrt(var + GN_EPS) * gng_ref[...] + gnb_ref[...]
        out_ref[b] = (yn + bon_ref[b, 0:tc_len]) * g_ref[b, 0:tc_len]

    @pl.when(tc == last)
    def _():
        shift_ref[...] = carry_ref[...]
        for b in range(nb):
            for hp in range(npair):
                s = s_ref[b, hp]
                sfin_ref[b, 2 * hp] = s[:, 0:HEAD_DIM]
                sfin_ref[b, 2 * hp + 1] = s[:, HEAD_DIM:2 * HEAD_DIM]


def _rwkv(p, shift_prev, wkv_prev, lw):
    bsz, t, _ = p.shape
    nb = 4
    tc_len = min(128, t)
    row = lambda a: a.reshape(1, -1)
    w_up, a_up = lw["rw_w_up"], lw["rw_a_up"]
    lora = jnp.zeros((LANES, 2 * RW_W), F32)
    lora = lora.at[0:W_LORA, 0:RW_W].set(w_up).at[W_LORA:LANES, RW_W:2 * RW_W].set(a_up)
    rr = jnp.arange(2 * LANES)[:, None]
    cc = jnp.arange(2 * LANES)[None, :]
    ones2 = (rr // HEAD_DIM == cc // HEAD_DIM).astype(BF16)
    diag = (jnp.arange(HEAD_DIM)[:, None] == cc[:, 0:LANES] % HEAD_DIM).astype(F32)
    blk3 = lambda w: pl.BlockSpec((nb, tc_len, w), lambda i, j: (i, j, 0))
    vec = lambda w: _const_spec((1, w))
    assert tc_len < SUBLANES or tc_len % SUBLANES == 0
    scr = lambda w: pltpu.VMEM((nb, max(SUBLANES, tc_len), w), F32)
    out, sfin, shift = pl.pallas_call(
        functools.partial(_rwkv_kernel, nb=nb, tc_len=tc_len),
        grid=(bsz // nb, t // tc_len),
        in_specs=[blk3(SHIFT_COLS),
                  pl.BlockSpec((nb, 1, SHIFT_COLS), lambda i, j: (i, 0, 0)),
                  pl.BlockSpec((nb, RW_HEADS, HEAD_DIM, HEAD_DIM), lambda i, j: (i, 0, 0, 0)),
                  vec(SHIFT_COLS), vec(RW_W), _const_spec((LANES, 2 * RW_W)), vec(RW_W),
                  _const_spec((G_LORA, RW_W)), vec(RW_W), vec(RW_W), vec(RW_W), vec(RW_W), vec(RW_W),
                  _const_spec((2 * LANES, 2 * LANES)), _const_spec((HEAD_DIM, LANES))],
        out_specs=[blk3(RW_W),
                   pl.BlockSpec((nb, RW_HEADS, HEAD_DIM, HEAD_DIM), lambda i, j: (i, 0, 0, 0)),
                   pl.BlockSpec((nb, 1, SHIFT_COLS), lambda i, j: (i, 0, 0))],
        out_shape=[jax.ShapeDtypeStruct((bsz, t, RW_W), F32),
                   jax.ShapeDtypeStruct((bsz, RW_HEADS, HEAD_DIM, HEAD_DIM), F32),
                   jax.ShapeDtypeStruct((bsz, 1, SHIFT_COLS), F32)],
        scratch_shapes=[pltpu.VMEM((nb, tc_len + 8, SHIFT_COLS), F32),
                        pltpu.VMEM((nb, 1, SHIFT_COLS), F32),
                        pltpu.VMEM((nb, RW_HEADS // 2, HEAD_DIM, LANES), F32)] + [scr(RW_W)] * 9,
        compiler_params=_cparams(("arbitrary", "arbitrary")),
        name="rwkv",
    )(p, shift_prev, wkv_prev, row(lw["shift_mu"]), row(lw["rw_w0"]), lora.astype(BF16), row(lw["rw_a0"]), lw["rw_g_up"].astype(BF16),
      row(lw["rw_k_k"]), row(lw["rw_k_a"]), row(lw["rw_r_k"]), row(lw["rw_gn_g"]), row(lw["rw_gn_b"]), ones2, diag)
    return out, sfin, shift


def _ordinal_to_score(o):
    o = jnp.maximum(o, ORD_NEG_INF)
    bits = jnp.where(o >= 0, o, o ^ jnp.int32(0x7FFFFFFF))
    return lax.bitcast_convert_type(bits, F32)


def _select_params(score_ref, nblk, blk, topk, nq, pos_bits):
    kf = jnp.float32(topk)

    def count(pred):
        def one(j, acc):
            off = pl.multiple_of(j * blk, blk)
            kb = score_ref[pl.ds(off, blk), :]
            pos = off + lax.broadcasted_iota(I32, (blk, nq), 0)
            hit = pred(kb, pos).astype(F32)
            return acc + jnp.sum(hit.reshape(blk // SUBLANES, SUBLANES, nq), axis=0)

        acc = lax.fori_loop(0, nblk // 2, lambda jj, acc: one(2 * jj + 1, one(2 * jj, acc)),
                            jnp.zeros((SUBLANES, nq), F32))
        acc = lax.cond(nblk % 2 == 1, lambda acc: one(nblk - 1, acc), lambda acc: acc, acc)
        return jnp.sum(acc, axis=0, keepdims=True)

    zero = jnp.zeros((1, nq), I32)
    c0 = count(lambda kb, pos: kb >= 0.0)
    ordinal = jnp.where(c0 >= kf, zero, jnp.full((1, nq), INT_MIN, I32))
    def bit_body(i, ordinal):
        cand = ordinal | jnp.left_shift(jnp.int32(1), jnp.int32(30) - i)
        cand_score = _ordinal_to_score(cand)
        c = count(lambda kb, pos: kb >= cand_score)
        return jnp.where(c >= kf, cand, ordinal)

    ordinal = lax.fori_loop(0, 31, bit_body, ordinal)
    theta = _ordinal_to_score(ordinal)
    big = jnp.full((1, nq), 2 ** 30, I32)

    def with_ties():
        n_gt = count(lambda kb, pos: kb > theta)
        n_eq = count(lambda kb, pos: kb == theta)
        need = kf - n_gt

        def tie_cut():
            def pbody(i, cut):
                cand = cut | jnp.left_shift(jnp.int32(1), jnp.int32(pos_bits - 1) - i)
                c = count(lambda kb, pos: (kb == theta) & (pos < cand))
                return jnp.where(c < need, cand, cut)
            return lax.fori_loop(0, pos_bits, pbody, jnp.zeros((1, nq), I32))

        return lax.cond(jnp.max(n_eq - need) > 0.0, tie_cut, lambda: big)

    return theta, with_ties()


def _t5_bucket(dist):
    n = jnp.maximum(dist, 0)
    max_exact = NUM_BUCKETS // 2
    nf = jnp.maximum(n, 1).astype(F32)
    large = max_exact + jnp.floor(jnp.log(nf / max_exact) / math.log(MAX_DISTANCE / max_exact)
                                  * (NUM_BUCKETS - max_exact)).astype(I32)
    large = jnp.minimum(large, NUM_BUCKETS - 1)
    return jnp.where(n < max_exact, n, large)


def _near_bias_kernel(rb_ref, o_ref, *, tq):
    c = lax.broadcasted_iota(I32, (2 * tq, 2 * tq), 0)
    l = lax.broadcasted_iota(I32, (2 * tq, 2 * tq), 1)
    first = l < tq
    bucket = _t5_bucket(jnp.where(first, l, l - tq) - c + tq)
    for p in range(ATT_HEADS // 2):
        acc = jnp.zeros((2 * tq, 2 * tq), F32)
        for bk in range(NUM_BUCKETS):
            val = jnp.where(first, rb_ref[bk, 2 * p], rb_ref[bk, 2 * p + 1])
            acc = jnp.where(bucket == bk, val, acc)
        o_ref[p] = acc


def _near_bias(rel_bias, tq):
    return pl.pallas_call(
        functools.partial(_near_bias_kernel, tq=tq),
        in_specs=[pl.BlockSpec(memory_space=pltpu.SMEM)],
        out_specs=pl.BlockSpec(memory_space=pltpu.VMEM),
        out_shape=jax.ShapeDtypeStruct((ATT_HEADS // 2, 2 * tq, 2 * tq), F32),
        compiler_params=pltpu.CompilerParams(vmem_limit_bytes=VMEM_LIMIT),
        name="near_bias",
    )(rel_bias)


def _attn_prompt_kernel(rb_ref, qt_ref, qit_ref, wit_ref, kb_ref, vbt_ref, kdb_ref, near_ref, o_ref,
                        key_ref, qbd_ref, qibd_ref, m_ref, l_ref, acc_ref, *, tq, topk, pos_bits):
    i = pl.program_id(1)
    npair = ATT_HEADS // 2
    upper = lax.broadcasted_iota(I32, (LANES, tq), 0) < HEAD_DIM

    def stack_pair(x):
        zero = jnp.zeros_like(x)
        return jnp.concatenate([jnp.where(upper, x, zero), jnp.where(upper, zero, x)], axis=1)

    for p in range(npair):
        qbd_ref[p] = stack_pair(qt_ref[0, p * LANES:(p + 1) * LANES, :])
    for p in range(IDX_HEADS // 2):
        qibd_ref[p] = stack_pair(qit_ref[0, p * LANES:(p + 1) * LANES, :])
    wrow = wit_ref[0] * (IDX_HEADS ** -0.5 * IDX_DIM ** -0.5)

    kpos = lax.broadcasted_iota(I32, (tq, tq), 0)
    qpos = i * tq + lax.broadcasted_iota(I32, (tq, tq), 1)

    def idx_body(j, carry):
        off = pl.multiple_of(j * tq, tq)
        kd = kdb_ref[0, pl.ds(off, tq), :]
        idx = jnp.zeros((tq, tq), F32)
        for p in range(IDX_HEADS // 2):
            sc = jnp.maximum(jnp.dot(kd, qibd_ref[p], preferred_element_type=F32), 0.0)
            idx = idx + sc[:, 0:tq] * wrow[2 * p:2 * p + 1, :] + sc[:, tq:2 * tq] * wrow[2 * p + 1:2 * p + 2, :]
        idx = jnp.where(idx == 0.0, 0.0, idx)
        key_ref[pl.ds(off, tq), :] = jnp.where(off + kpos <= qpos, idx, -jnp.inf)
        return carry

    lax.fori_loop(0, i + 1, idx_body, 0)

    theta, cut = _select_params(key_ref, i + 1, tq, topk, tq, pos_bits)

    m_ref[...] = jnp.full(m_ref.shape, NEG_BIG, F32)
    l_ref[...] = jnp.zeros(l_ref.shape, F32)
    acc_ref[...] = jnp.zeros(acc_ref.shape, F32)
    first = lax.broadcasted_iota(I32, (1, 2 * tq), 1) < tq
    ksub = LANES
    ones_rows = jnp.ones((2 * SUBLANES, ksub), BF16)
    kpos_s = lax.broadcasted_iota(I32, (ksub, tq), 0)
    qpos_s = i * tq + lax.broadcasted_iota(I32, (ksub, tq), 1)

    def attend(j, near_rows):
        for half in range(tq // ksub):
            off = pl.multiple_of(j * tq + half * ksub, ksub)
            kb = key_ref[pl.ds(off, ksub), :]
            pos = off + kpos_s
            sel = ((kb > theta) | ((kb == theta) & (pos <= cut))) & (pos <= qpos_s)
            mb = jnp.where(sel, 0.0, -jnp.inf)
            mb = jnp.concatenate([mb, mb], axis=1)
            ss = [jnp.dot(kb_ref[0, pl.ds(off, ksub), p * LANES:(p + 1) * LANES], qbd_ref[p],
                          preferred_element_type=F32) for p in range(npair)]
            prs, alphas = [], []
            for p in range(npair):
                if near_rows is None:
                    logits = ss[p] + mb
                    const = jnp.where(first, rb_ref[NUM_BUCKETS - 1, 2 * p], rb_ref[NUM_BUCKETS - 1, 2 * p + 1])
                else:
                    r0 = near_rows + half * ksub
                    logits = ss[p] + (near_ref[p, r0:r0 + ksub, :] + mb)
                    const = jnp.zeros((1, 2 * tq), F32)
                m_old = m_ref[p]
                m_new = jnp.maximum(m_old, jnp.max(logits, axis=0, keepdims=True) + const)
                alphas.append(jnp.exp(m_old - m_new))
                prs.append(jnp.exp(logits - (m_new[0:1, :] - const)).astype(BF16))
                m_ref[p] = m_new
            for p in range(npair):
                vt = jnp.concatenate([vbt_ref[0, j * (tq // ksub) + half, p * LANES:(p + 1) * LANES, :], ones_rows],
                                     axis=0)
                pv = jnp.dot(vt, prs[p], preferred_element_type=F32)
                l_ref[p] = alphas[p] * l_ref[p] + pv[LANES:LANES + SUBLANES, :]
                acc_ref[p] = alphas[p][0:1, :] * acc_ref[p] + pv[0:LANES, :]

    def far_body(j, carry):
        attend(j, None)
        return carry

    lax.fori_loop(0, jnp.maximum(i - 1, 0), far_body, 0)

    @pl.when(i >= 1)
    def _():
        attend(i - 1, 0)

    attend(i, tq)

    for p in range(npair):
        o = acc_ref[p] / l_ref[p][0:1, :]
        ot = jnp.concatenate([o[0:HEAD_DIM, 0:tq], o[HEAD_DIM:LANES, tq:2 * tq]], axis=0)
        o_ref[0, :, p * LANES:(p + 1) * LANES] = ot.T


def _attn_prompt(qt, qit, wit, kb, vbt, kdb, rel_bias, bsz, t, tq):
    topk = min(MAX_TOPK, t // 4)
    pos_bits = max(1, (t - 1).bit_length())
    near = _near_bias(rel_bias, tq)
    npair = ATT_HEADS // 2
    nblk = t // tq
    qtile_t = lambda rows: pl.BlockSpec((1, rows, tq), lambda b, i: (b, 0, i))
    full = lambda w: pl.BlockSpec((1, t, w), lambda b, i: (b, 0, 0))
    return pl.pallas_call(
        functools.partial(_attn_prompt_kernel, tq=tq, topk=topk, pos_bits=pos_bits),
        grid=(bsz, nblk),
        in_specs=[pl.BlockSpec(memory_space=pltpu.SMEM),
                  qtile_t(ATT_W), qtile_t(IDX_HEADS * IDX_DIM), qtile_t(SUBLANES),
                  full(ATT_W), pl.BlockSpec((1, t // LANES, ATT_W, LANES), lambda b, i: (b, 0, 0, 0)), full(LANES),
                  _const_spec((npair, 2 * tq, 2 * tq))],
        out_specs=pl.BlockSpec((1, tq, ATT_W), lambda b, i: (b, i, 0)),
        out_shape=jax.ShapeDtypeStruct((bsz, t, ATT_W), F32),
        scratch_shapes=[pltpu.VMEM((t, tq), F32),
                        pltpu.VMEM((npair, LANES, 2 * tq), BF16),
                        pltpu.VMEM((IDX_HEADS // 2, LANES, 2 * tq), BF16),
                        pltpu.VMEM((npair, SUBLANES, 2 * tq), F32),
                        pltpu.VMEM((npair, SUBLANES, 2 * tq), F32),
                        pltpu.VMEM((npair, LANES, 2 * tq), F32)],
        compiler_params=_cparams(("arbitrary", "arbitrary")),
        name="attn_prompt",
    )(rel_bias, qt, qit, wit, kb.reshape(bsz, t, ATT_W), vbt, kdb.reshape(bsz, t, LANES), near)


def _sample_index_kernel(pt_ref, qi_ref, wi_ref, *rest, n_pages, ts):
    page_refs = rest[:n_pages + 1]
    o_ref = rest[n_pages + 1]
    qi = qi_ref[0]
    q16 = jnp.concatenate([qi[:, h * IDX_DIM:(h + 1) * IDX_DIM] for h in range(IDX_HEADS)], axis=0).astype(BF16)
    wi = wi_ref[0]
    scale = IDX_HEADS ** -0.5 * IDX_DIM ** -0.5
    w16 = jnp.concatenate([wi[:, h:h + 1] for h in range(IDX_HEADS)], axis=0) * scale
    trow = lax.broadcasted_iota(I32, (ts, PAGE_SIZE), 0)
    ncol = lax.broadcasted_iota(I32, (ts, PAGE_SIZE), 1)
    for j in range(n_pages + 1):
        kp = page_refs[j][0].astype(BF16)
        if j < n_pages:
            sc = jnp.dot(q16, kp, preferred_element_type=F32)
        else:
            sc = lax.dot_general(q16, kp, (((1,), (1,)), ((), ())), preferred_element_type=F32)
        rw = jnp.maximum(sc, 0.0) * w16
        idx = rw[0:ts]
        for h in range(1, IDX_HEADS):
            idx = idx + rw[h * ts:(h + 1) * ts]
        if j == n_pages:
            idx = jnp.where(ncol <= trow, idx, -jnp.inf)
        o_ref[0, :, j * PAGE_SIZE:(j + 1) * PAGE_SIZE] = idx


def _sample_select_kernel(idx_ref, sel_ref, key_ref, *, topk, pos_bits):
    lp, nq = idx_ref.shape
    idx = idx_ref[...]
    key_ref[...] = jnp.where(idx == 0.0, 0.0, idx)
    theta, cut = _select_params(key_ref, 1, lp, topk, nq, pos_bits)
    kb = key_ref[...]
    pos = lax.broadcasted_iota(I32, (lp, nq), 0)
    sel = ((kb > theta) | ((kb == theta) & (pos <= cut))) & (idx_ref[...] > -jnp.inf)
    sel_ref[...] = sel.astype(F32)


def _sample_bias_kernel(rb_ref, near_ref, new_ref, far_ref, *, ts):
    rows = ts * ATT_HEADS
    r = lax.broadcasted_iota(I32, (rows, PAGE_SIZE), 0)
    c = lax.broadcasted_iota(I32, (rows, PAGE_SIZE), 1)
    t = r // ATT_HEADS
    rhead = r % ATT_HEADS

    def gather(bucket):
        acc = jnp.zeros((rows, PAGE_SIZE), F32)
        for h in range(ATT_HEADS):
            for bk in range(NUM_BUCKETS):
                acc = jnp.where((bucket == bk) & (rhead == h), rb_ref[bk, h], acc)
        return acc

    near_ref[...] = gather(_t5_bucket(PAGE_SIZE + t - c))
    ok = (c <= t) & (c < ts)
    new_ref[...] = jnp.where(ok, gather(_t5_bucket(t - c)), -jnp.inf)
    far_ref[...] = gather(jnp.full((rows, PAGE_SIZE), NUM_BUCKETS - 1, I32))


def _sample_attn_kernel(pt_ref, q_ref, sel_ref, near_ref, new_ref, far_ref, *rest, n_pages, ts):
    k_refs = rest[:n_pages + 1]
    v_refs = rest[n_pages + 1:2 * n_pages + 2]
    o_ref = rest[2 * n_pages + 2]
    l_ref = rest[2 * n_pages + 3]
    rows = ts * ATT_HEADS
    rep = lambda a: jnp.concatenate([jnp.broadcast_to(a[t:t + 1], (ATT_HEADS, a.shape[1])) for t in range(ts)], axis=0)
    r = lax.broadcasted_iota(I32, (rows, ATT_W), 0)
    c = lax.broadcasted_iota(I32, (rows, ATT_W), 1)
    own = (r % ATT_HEADS) == (c // HEAD_DIM)
    q = rep(q_ref[0].astype(F32) * HEAD_DIM ** -0.5)
    q = jnp.where(own, q, 0.0).astype(BF16)
    sel = rep(sel_ref[0]) > 0.5

    for j in range(n_pages):
        kt = k_refs[j][0].reshape(ATT_W, PAGE_SIZE).astype(BF16)
        s = jnp.dot(q, kt, preferred_element_type=F32)
        bias = near_ref[...] if j == n_pages - 1 else far_ref[...]
        sl = slice(j * PAGE_SIZE, (j + 1) * PAGE_SIZE)
        l_ref[:, sl] = jnp.where(sel[:, sl], s + bias, -jnp.inf)
    kn = k_refs[n_pages][0].astype(BF16)
    s = lax.dot_general(q, kn, (((1,), (1,)), ((), ())), preferred_element_type=F32)
    sl = slice(n_pages * PAGE_SIZE, (n_pages + 1) * PAGE_SIZE)
    l_ref[:, sl] = jnp.where(sel[:, sl], s + new_ref[...], -jnp.inf)

    logits = l_ref[...]
    m = jnp.max(logits, axis=1, keepdims=True)
    pr = jnp.exp(logits - m)
    den = jnp.sum(pr, axis=1, keepdims=True)
    prb = pr.astype(BF16)
    acc = jnp.dot(prb[:, sl], v_refs[n_pages][0].astype(BF16), preferred_element_type=F32)
    for j in range(n_pages):
        vt = v_refs[j][0].reshape(ATT_W, PAGE_SIZE).astype(BF16)
        acc = acc + lax.dot_general(prb[:, j * PAGE_SIZE:(j + 1) * PAGE_SIZE], vt, (((1,), (1,)), ((), ())),
                                    preferred_element_type=F32)
    acc = jnp.where(own, acc / den, 0.0)
    o_ref[0] = jnp.sum(acc.reshape(ts, ATT_HEADS, ATT_W), axis=1)


def _attn_sample(qb, qi, wi, k_new, v_new, kd_new, cache_k, cache_v, cache_kidx, page_table, rel_bias, bsz, ts):
    n_pages = page_table.shape[1]
    past_len = n_pages * PAGE_SIZE
    topk = min(MAX_TOPK, (past_len + ts) // 4)
    lp = (n_pages + 1) * PAGE_SIZE
    rows = ts * ATT_HEADS
    pt = page_table.reshape(-1).astype(I32)

    ki_pool = jnp.swapaxes(cache_kidx, 1, 2)
    ki_new = jnp.zeros((bsz, PAGE_SIZE, IDX_DIM), F32).at[:, 0:ts].set(kd_new.reshape(bsz, ts, LANES)[..., 0:IDX_DIM])
    page_spec = lambda j, shp: pl.BlockSpec((1,) + shp, lambda b, ptr: (ptr[b * n_pages + j],) + (0,) * len(shp))
    per_b = lambda shp: pl.BlockSpec((1,) + shp, lambda b, ptr: (b,) + (0,) * len(shp))
    idx = pl.pallas_call(
        functools.partial(_sample_index_kernel, n_pages=n_pages, ts=ts),
        grid_spec=pltpu.PrefetchScalarGridSpec(
            num_scalar_prefetch=1, grid=(bsz,),
            in_specs=[per_b((ts, IDX_HEADS * IDX_DIM)), per_b((ts, LANES))]
                     + [page_spec(j, (IDX_DIM, PAGE_SIZE)) for j in range(n_pages)]
                     + [per_b((PAGE_SIZE, IDX_DIM))],
            out_specs=per_b((ts, lp))),
        out_shape=jax.ShapeDtypeStruct((bsz, ts, lp), F32),
        compiler_params=_cparams(("arbitrary",)),
        name="sample_index",
    )(pt, qi.reshape(bsz, ts, -1), wi.reshape(bsz, ts, LANES), *([ki_pool] * n_pages), ki_new)

    sel = pl.pallas_call(
        functools.partial(_sample_select_kernel, topk=topk, pos_bits=max(1, (lp - 1).bit_length())),
        out_shape=jax.ShapeDtypeStruct((lp, bsz * ts), F32),
        scratch_shapes=[pltpu.VMEM((lp, bsz * ts), F32)],
        compiler_params=pltpu.CompilerParams(vmem_limit_bytes=VMEM_LIMIT),
        name="sample_select",
    )(idx.reshape(bsz * ts, lp).T)
    sel = sel.T.reshape(bsz, ts, lp)

    near, newb, far = pl.pallas_call(
        functools.partial(_sample_bias_kernel, ts=ts),
        in_specs=[pl.BlockSpec(memory_space=pltpu.SMEM)],
        out_shape=[jax.ShapeDtypeStruct((rows, PAGE_SIZE), F32)] * 3,
        compiler_params=pltpu.CompilerParams(vmem_limit_bytes=VMEM_LIMIT),
        name="sample_bias",
    )(rel_bias)

    page4 = (ATT_HEADS, HEAD_DIM, PAGE_SIZE)
    k_pool = jnp.transpose(cache_k, (0, 2, 3, 1))
    v_pool = jnp.transpose(cache_v, (0, 2, 3, 1))
    pad_new = lambda a: jnp.zeros((bsz, PAGE_SIZE, ATT_W), F32).at[:, 0:ts].set(a.reshape(bsz, ts, ATT_W))
    const = lambda shp: pl.BlockSpec(shp, lambda b, ptr: (0,) * len(shp))
    out = pl.pallas_call(
        functools.partial(_sample_attn_kernel, n_pages=n_pages, ts=ts),
        grid_spec=pltpu.PrefetchScalarGridSpec(
            num_scalar_prefetch=1, grid=(bsz,),
            in_specs=[per_b((ts, ATT_W)), per_b((ts, lp)),
                      const((rows, PAGE_SIZE)), const((rows, PAGE_SIZE)), const((rows, PAGE_SIZE))]
                     + [page_spec(j, page4) for j in range(n_pages)] + [per_b((PAGE_SIZE, ATT_W))]
                     + [page_spec(j, page4) for j in range(n_pages)] + [per_b((PAGE_SIZE, ATT_W))],
            out_specs=per_b((ts, ATT_W)),
            scratch_shapes=[pltpu.VMEM((rows, lp), F32)]),
        out_shape=jax.ShapeDtypeStruct((bsz, ts, ATT_W), F32),
        compiler_params=_cparams(("arbitrary",)),
        name="sample_attn",
    )(pt, qb.reshape(bsz, ts, ATT_W), sel, near, newb, far,
      *([k_pool] * n_pages), pad_new(k_new), *([v_pool] * n_pages), pad_new(v_new))
    return out.reshape(bsz * ts, ATT_W)


def _layer_norm(x, g, b):
    mu = jnp.mean(x, axis=-1, keepdims=True)
    xc = x - mu
    var = jnp.mean(xc * xc, axis=-1, keepdims=True)
    return xc * lax.rsqrt(var + LN_EPS) * g + b


def _ffn_kernel(x_ref, rw_ref, att_ref, prev_ref, wo_ref, g1_ref, b1_ref, wup_ref, cw_ref, cb_ref, wdn_ref,
                g2_ref, b2_ref, y_ref, nc_ref, pad_ref, *, tm, shift, off, alpha):
    ti = pl.program_id(1)
    data = off + 2 * shift

    @pl.when(ti == 0)
    def _():
        pad_ref[off:data, :] = prev_ref[0]

    x = x_ref[0]
    mix = (jnp.dot(rw_ref[0].astype(BF16), wo_ref[0:RW_W, :], preferred_element_type=F32)
           + jnp.dot(att_ref[0].astype(BF16), wo_ref[RW_W:RW_W + ATT_W, :], preferred_element_type=F32))
    x1 = _layer_norm(alpha * x + mix, g1_ref[...], b1_ref[...])
    x1b = x1.astype(BF16)
    u_conv = jnp.dot(x1b, wup_ref[:, 0:D_FF], preferred_element_type=F32)
    u_lin = jnp.dot(x1b, wup_ref[:, D_FF:2 * D_FF], preferred_element_type=F32)
    pad_ref[data:data + tm, :] = u_conv
    conv = (pad_ref[off:off + tm, :] * cw_ref[0:1, :] + pad_ref[off + shift:off + shift + tm, :] * cw_ref[1:2, :]
            + u_conv * cw_ref[2:3, :] + cb_ref[...])
    h = conv * _sigmoid(conv) * u_lin
    down = jnp.dot(h.astype(BF16), wdn_ref[...], preferred_element_type=F32)
    y_ref[0] = _layer_norm(alpha * x1 + down, g2_ref[...], b2_ref[...])
    tail = pad_ref[off + tm:data + tm, :]
    pad_ref[off:data, :] = tail
    nc_ref[0] = tail


def _ffn(x, rw, att, prev, lw, alpha, shift):
    nb, rows, _ = x.shape
    tm = min(256, rows) if shift == 1 else rows
    off = 6 if shift == 1 else 0
    row = lambda a: a.reshape(1, -1)
    tile = lambda w: pl.BlockSpec((1, tm, w), lambda b, i: (b, i, 0))
    y, nc = pl.pallas_call(
        functools.partial(_ffn_kernel, tm=tm, shift=shift, off=off, alpha=alpha),
        grid=(nb, rows // tm),
        in_specs=[tile(D_MODEL), tile(RW_W), tile(ATT_W),
                  pl.BlockSpec((1, 2 * shift, D_FF), lambda b, i: (b, 0, 0)),
                  _const_spec((D_MODEL, D_MODEL)), _const_spec((1, D_MODEL)), _const_spec((1, D_MODEL)),
                  _const_spec((D_MODEL, 2 * D_FF)), _const_spec((CONV_W, D_FF)), _const_spec((1, D_FF)),
                  _const_spec((D_FF, D_MODEL)), _const_spec((1, D_MODEL)), _const_spec((1, D_MODEL))],
        out_specs=[tile(D_MODEL), pl.BlockSpec((1, 2 * shift, D_FF), lambda b, i: (b, 0, 0))],
        out_shape=[jax.ShapeDtypeStruct((nb, rows, D_MODEL), F32), jax.ShapeDtypeStruct((nb, 2 * shift, D_FF), F32)],
        scratch_shapes=[pltpu.VMEM((off + 2 * shift + tm, D_FF), F32)],
        compiler_params=_cparams(("arbitrary", "arbitrary")),
        name="ffn",
    )(x, rw, att, prev, lw["w_out"].astype(BF16), row(lw["ln1_g"]), row(lw["ln1_b"]),
      lw["ffn_w_up"].astype(BF16), lw["ffn_conv_w"], row(lw["ffn_conv_b"]), lw["ffn_w_down"].astype(BF16),
      row(lw["ln2_g"]), row(lw["ln2_b"]))
    return y, nc


def _layer(x, shift_prev, wkv_prev, conv_prev, past, rel_bias, lw, alpha):
    bsz, t, _ = x.shape
    w_in = lw["w_in"]
    w_ki = w_in[:, C_KI:C_KI + IDX_DIM]
    w_wi = w_in[:, C_KI + IDX_DIM:]
    wp = jnp.concatenate([w_in[:, 0:C_KI], w_ki, w_ki, w_wi,
                          jnp.zeros((D_MODEL, LANES - IDX_HEADS), F32)], axis=1).astype(BF16)
    x2d = x.reshape(bsz * t, D_MODEL)
    if past is None:
        tq = min(256, t)
        prw, qt, kb, kt, vt, vbt, qit, kdb, kit, wit = _proj_t(x2d, wp, bsz, t, tq)
        rw_out, new_wkv, new_shift = _rwkv(prw.reshape(bsz, t, SHIFT_COLS), shift_prev, wkv_prev, lw)
        att = _attn_prompt(qt, qit, wit, kb, vbt, kdb, rel_bias, bsz, t, tq)
        y, new_conv = _ffn(x, rw_out, att, conv_prev, lw, alpha, shift=1)
        heads = lambda a: jnp.transpose(a.reshape(bsz, ATT_HEADS, HEAD_DIM, t), (0, 3, 1, 2))
        return y, (heads(kt), heads(vt), jnp.swapaxes(kit, 1, 2), new_wkv, new_shift, new_conv)
    prw, qb, k, v, qi, kd, wi = _proj(x2d, wp)
    rw_out, new_wkv, new_shift = _rwkv(prw.reshape(bsz, t, SHIFT_COLS), shift_prev, wkv_prev, lw)
    att = _attn_sample(qb, qi, wi, k, v, kd, *past, rel_bias, bsz, t)
    tmaj = lambda a: jnp.swapaxes(a.reshape(bsz, t, -1), 0, 1).reshape(1, t * bsz, -1)
    prev = jnp.swapaxes(conv_prev, 0, 1).reshape(1, (CONV_W - 1) * bsz, D_FF)
    y, new_conv = _ffn(tmaj(x), tmaj(rw_out), tmaj(att), prev, lw, alpha, shift=bsz)
    y = jnp.swapaxes(y.reshape(t, bsz, D_MODEL), 0, 1)
    new_conv = jnp.swapaxes(new_conv.reshape(CONV_W - 1, bsz, D_FF), 0, 1)
    state = (k.reshape(bsz, t, ATT_HEADS, HEAD_DIM), v.reshape(bsz, t, ATT_HEADS, HEAD_DIM),
             kd.reshape(bsz, t, LANES)[..., 0:IDX_DIM], new_wkv, new_shift, new_conv)
    return y, state


_LAYER_WEIGHTS = ("w_in", "shift_mu", "rw_w0", "rw_w_up", "rw_a0", "rw_a_up", "rw_g_up", "rw_k_k", "rw_k_a",
                  "rw_r_k", "rw_gn_g", "rw_gn_b", "w_out", "ln1_g", "ln1_b", "ffn_w_up", "ffn_conv_w",
                  "ffn_conv_b", "ffn_w_down", "ln2_g", "ln2_b")


def kernel(x_prompt, x_sample, cache_k, cache_v, cache_kidx, page_table, state_wkv, state_shift, state_conv,
           rel_bias, w_in, shift_mu, rw_w0, rw_w_up, rw_a0, rw_a_up, rw_g_up, rw_k_k, rw_k_a, rw_r_k, rw_gn_g,
           rw_gn_b, w_out, ln1_g, ln1_b, ffn_w_up, ffn_conv_w, ffn_conv_b, ffn_w_down, ln2_g, ln2_b):
    stacked = dict(zip(_LAYER_WEIGHTS, (w_in, shift_mu, rw_w0, rw_w_up, rw_a0, rw_a_up, rw_g_up, rw_k_k, rw_k_a,
                                        rw_r_k, rw_gn_g, rw_gn_b, w_out, ln1_g, ln1_b, ffn_w_up, ffn_conv_w,
                                        ffn_conv_b, ffn_w_down, ln2_g, ln2_b)))
    depth = w_in.shape[0]
    alpha = (2 * depth) ** 0.25
    bp = x_prompt.shape[0]
    yp, ys = x_prompt, x_sample
    outs_p, outs_s = [], []
    for l in range(depth):
        lw = {name: w[l] for name, w in stacked.items()}
        yp, st_p = _layer(yp, jnp.zeros((bp, 1, SHIFT_COLS), F32),
                          jnp.zeros((bp, RW_HEADS, HEAD_DIM, HEAD_DIM), F32),
                          jnp.zeros((bp, CONV_W - 1, D_FF), F32), None, rel_bias, lw, alpha)
        ys, st_s = _layer(ys, state_shift[l], state_wkv[l], state_conv[l],
                          (cache_k[l], cache_v[l], cache_kidx[l], page_table), rel_bias, lw, alpha)
        outs_p.append(st_p)
        outs_s.append(st_s)
    st = lambda outs, i: jnp.stack([o[i] for o in outs])
    return (yp, ys) + tuple(st(outs_p, i) for i in range(6)) + tuple(st(outs_s, i) for i in range(6))
```
